```python
import math
import jax, jax.numpy as jnp
from jax import lax
import numpy as np

D_MODEL = 1024
BATCH = 8
SEQ = 2048
DEPTH = 2
DEC_BATCH = 128
DEC_SEQ = 4
PAST_LEN = 16384
PAGE_SIZE = 128

S5_WIDTH = D_MODEL // 4
S5_GROUP = 16
S5_N_GROUPS = S5_WIDTH // S5_GROUP
S5_STATE = 64
LRU_WIDTH = D_MODEL // 4
LRU_BLOCKS = 8
LRU_BLOCK = LRU_WIDTH // LRU_BLOCKS
LRU_C = 8.0
CONV_WIDTH = 4
GDN_WIDTH = D_MODEL // 2
GDN_HEAD_DIM = 128
GDN_HEADS = GDN_WIDTH // GDN_HEAD_DIM
GDN_CHUNK = 64
MIX_WIDTH = S5_WIDTH + LRU_WIDTH + GDN_WIDTH
IN_SPLITS = (S5_WIDTH, LRU_WIDTH, LRU_WIDTH, 3 * GDN_WIDTH, GDN_HEADS, GDN_HEADS, GDN_WIDTH)
IN_COLS = S5_WIDTH + 2 * LRU_WIDTH + 4 * GDN_WIDTH + 2 * GDN_HEADS
FFN_HIDDEN = ((8 * D_MODEL // 3 + 255) // 256) * 256
NORM_EPS = 1e-6

kernel_name = "hybrid_s5_rglru_gdn_decode_step"


def rms_norm(x, w):
    xf = x.astype(jnp.float32)
    y = xf * lax.rsqrt(jnp.mean(xf * xf, axis=-1, keepdims=True) + NORM_EPS)
    return (y * w.astype(jnp.float32)).astype(x.dtype)


def l2_normalize(x):
    xf = x.astype(jnp.float32)
    return xf * lax.rsqrt(jnp.sum(xf * xf, axis=-1, keepdims=True) + NORM_EPS)


def causal_conv(x, buf, w):
    L = x.shape[1]
    xp = jnp.concatenate([buf.astype(x.dtype), x], axis=1)
    y = xp[:, 0:L] * w[0]
    for j in range(1, w.shape[0]):
        y = y + xp[:, j:j + L] * w[j]
    return y, xp[:, L:]


def _cmul(ar, ai, br, bi):
    return ar * br - ai * bi, ar * bi + ai * br


def _s5_combine(e1, e2):
    a1r, a1i, b1r, b1i = e1
    a2r, a2i, b2r, b2i = e2
    ar, ai = _cmul(a2r, a2i, a1r, a1i)
    br, bi = _cmul(a2r, a2i, b1r, b1i)
    return ar, ai, br + b2r, bi + b2i


def _lru_combine(e1, e2):
    a1, b1 = e1
    a2, b2 = e2
    return a1 * a2, a2 * b1 + b2


def s5_mixer(u, h0_re, h0_im, lam_re, lam_im, log_dt, b_re, b_im, c_re, c_im, d, w_glu, b_glu):
    f32 = jnp.float32
    Bsz, L, _ = u.shape
    uf = u.astype(f32)
    ug = uf.reshape(Bsz, L, S5_N_GROUPS, S5_GROUP)
    lr = jnp.minimum(lam_re.astype(f32), -1e-4)
    li = lam_im.astype(f32)
    dt = jnp.exp(log_dt.astype(f32))[:, None]
    mag = jnp.exp(lr * dt)
    ar, ai = mag * jnp.cos(li * dt), mag * jnp.sin(li * dt)
    den = lr * lr + li * li
    fr = ((ar - 1.0) * lr + ai * li) / den
    fi = (ai * lr - (ar - 1.0) * li) / den
    bbr = fr[..., None] * b_re - fi[..., None] * b_im
    bbi = fr[..., None] * b_im + fi[..., None] * b_re
    bu_r = jnp.einsum("blgh,gph->blgp", ug, bbr)
    bu_i = jnp.einsum("blgh,gph->blgp", ug, bbi)
    ir, ii = _cmul(ar, ai, h0_re.astype(f32), h0_im.astype(f32))
    bu_r = bu_r.at[:, 0].add(ir)
    bu_i = bu_i.at[:, 0].add(ii)
    a_r = jnp.broadcast_to(ar, bu_r.shape)
    a_i = jnp.broadcast_to(ai, bu_i.shape)
    _, _, xr, xi = lax.associative_scan(_s5_combine, (a_r, a_i, bu_r, bu_i), axis=1)
    y = jnp.einsum("blgp,ghp->blgh", xr, c_re) - jnp.einsum("blgp,ghp->blgh", xi, c_im)
    y = y.reshape(Bsz, L, S5_WIDTH) + d * uf
    y = jax.nn.gelu(y)
    y = y * jax.nn.sigmoid(y @ w_glu + b_glu)
    return y, xr[:, -1], xi[:, -1]


def lru_mixer(xb, gate, pos, h0, conv_buf, conv_w, conv_b, w_a, b_a, w_x, b_x, lam):
    f32 = jnp.float32
    Bsz, L, _ = xb.shape
    xc, new_buf = causal_conv(xb, conv_buf, conv_w)
    xc = (xc + conv_b).astype(f32)
    xh = xc.reshape(Bsz, L, LRU_BLOCKS, LRU_BLOCK)
    r = jax.nn.sigmoid(jnp.einsum("blhi,hij->blhj", xh, w_a).reshape(Bsz, L, LRU_WIDTH) + b_a)
    gi = jax.nn.sigmoid(jnp.einsum("blhi,hij->blhj", xh, w_x).reshape(Bsz, L, LRU_WIDTH) + b_x)
    log_a = -LRU_C * r * jax.nn.softplus(-lam.astype(f32))
    a = jnp.exp(log_a)
    mult = jnp.sqrt(-jnp.expm1(2.0 * log_a))
    mult = jnp.where((pos == 0)[None, :, None], 1.0, mult)
    b = mult * gi * xc
    b = b.at[:, 0].add(a[:, 0] * h0.astype(f32))
    _, h = lax.associative_scan(_lru_combine, (a, b), axis=1)
    y = h * jax.nn.gelu(gate.astype(f32))
    return y, h[:, -1], new_buf


def _pad_seq(t, pad):
    cfg = [(0, 0)] * t.ndim
    cfg[1] = (0, pad)
    return jnp.pad(t, cfg)


def gated_delta_chunked(q, k, v, g, beta, S0):
    f32 = jnp.float32
    Bsz, L, H, dk = q.shape
    dv = v.shape[-1]
    C = min(GDN_CHUNK, L)
    n = -(-L // C)
    pad = n * C - L
    q = q.astype(f32) * (dk ** -0.5)
    k, v, g, beta = k.astype(f32), v.astype(f32), g.astype(f32), beta.astype(f32)
    if pad:
        q, k, v, g, beta = (_pad_seq(t, pad) for t in (q, k, v, g, beta))

    def chunks(t):
        return jnp.moveaxis(t.reshape((Bsz, n, C, H) + t.shape[3:]), (1, 3), (0, 2))

    qc, kc, vc, gc, bc = (chunks(t) for t in (q, k, v, g, beta))
    gcum = jnp.cumsum(gc, axis=-1)
    idx = jnp.arange(C)
    incl = idx[:, None] >= idx[None, :]
    strict = idx[:, None] > idx[None, :]
    decay = jnp.exp(jnp.where(incl, gcum[..., :, None] - gcum[..., None, :], -jnp.inf))
    kb = kc * bc[..., None]
    vb = vc * bc[..., None]
    M = jnp.where(strict, jnp.einsum("nbhik,nbhjk->nbhij", kb, kc) * decay, 0.0)
    eye = jnp.eye(C, dtype=f32)
    T = lax.linalg.triangular_solve(eye + M, jnp.broadcast_to(eye, M.shape),
                                    left_side=True, lower=True, unit_diagonal=True)
    w_v = T @ vb
    w_k = T @ (kb * jnp.exp(gcum)[..., None])
    q_dec = qc * jnp.exp(gcum)[..., None]
    attn = jnp.einsum("nbhik,nbhjk->nbhij", qc, kc) * decay
    g_last = gcum[..., -1]
    k_dec = kc * jnp.exp(g_last[..., None] - gcum)[..., None]

    def step(S, xs):
        w_v_i, w_k_i, q_dec_i, attn_i, k_dec_i, g_last_i = xs
        v_new = w_v_i - jnp.einsum("bhck,bhkv->bhcv", w_k_i, S)
        o = jnp.einsum("bhck,bhkv->bhcv", q_dec_i, S) + jnp.einsum("bhij,bhjv->bhiv", attn_i, v_new)
        S = S * jnp.exp(g_last_i)[..., None, None] + jnp.einsum("bhck,bhcv->bhkv", k_dec_i, v_new)
        return S, o

    S, o = lax.scan(step, S0.astype(f32), (w_v, w_k, q_dec, attn, k_dec, g_last))
    o = jnp.moveaxis(o, (0, 2), (1, 3)).reshape(Bsz, n * C, H, dv)[:, :L]
    return o, S


def layer(x, pos, st, p):
    s5_re, s5_im, lru_h, lru_conv, gdn_S, gdn_conv = st
    Bsz, L, _ = x.shape
    h = rms_norm(x, p["norm_mix_pre"])
    proj = h @ p["w_in"]
    cuts = [int(c) for c in np.cumsum(IN_SPLITS)[:-1]]
    u_s5, x_lru, g_lru, qkv, a_gdn, b_gdn, z_gdn = jnp.split(proj, cuts, axis=-1)
    y_s5, n_s5_re, n_s5_im = s5_mixer(u_s5, s5_re, s5_im, p["s5_lambda_re"], p["s5_lambda_im"],
                                      p["s5_log_dt"], p["s5_b_re"], p["s5_b_im"], p["s5_c_re"],
                                      p["s5_c_im"], p["s5_d"], p["s5_w_glu"], p["s5_b_glu"])
    y_lru, n_lru_h, n_lru_conv = lru_mixer(x_lru, g_lru, pos, lru_h, lru_conv, p["lru_conv_w"],
                                           p["lru_conv_b"], p["lru_w_a"], p["lru_b_a"],
                                           p["lru_w_x"], p["lru_b_x"], p["lru_lambda"])
    qkv_c, n_gdn_conv = causal_conv(qkv, gdn_conv, p["gdn_conv_w"])
    qkv_c = jax.nn.silu(qkv_c.astype(jnp.float32))
    q, k, v = jnp.split(qkv_c, 3, axis=-1)
    q = l2_normalize(q.reshape(Bsz, L, GDN_HEADS, GDN_HEAD_DIM))
    k = l2_normalize(k.reshape(Bsz, L, GDN_HEADS, GDN_HEAD_DIM))
    v = v.reshape(Bsz, L, GDN_HEADS, GDN_HEAD_DIM)
    g = -jnp.exp(p["gdn_a_log"].astype(jnp.float32)) * jax.nn.softplus(a_gdn.astype(jnp.float32) + p["gdn_dt_bias"])
    beta = jax.nn.sigmoid(b_gdn.astype(jnp.float32))
    o, n_gdn_S = gated_delta_chunked(q, k, v, g, beta, gdn_S)
    z = z_gdn.astype(jnp.float32).reshape(Bsz, L, GDN_HEADS, GDN_HEAD_DIM)
    y_gdn = (rms_norm(o, p["gdn_norm_w"]) * jax.nn.silu(z)).reshape(Bsz, L, GDN_WIDTH)
    mix = jnp.concatenate([y_s5, y_lru, y_gdn], axis=-1).astype(x.dtype) @ p["w_out"]
    x = x + rms_norm(mix, p["norm_mix_post"]).astype(x.dtype)
    h = rms_norm(x, p["norm_ffn_pre"])
    gt, up = jnp.split(h @ p["ffn_w_gate_up"], 2, axis=-1)
    f = (jax.nn.silu(gt) * up) @ p["ffn_w_down"]
    x = x + rms_norm(f, p["norm_ffn_post"]).astype(x.dtype)
    return x, (n_s5_re, n_s5_im, n_lru_h, n_lru_conv, n_gdn_S, n_gdn_conv)


def setup_inputs(seed: int = 0) -> dict:
    key = jax.random.key(seed)
    ks = jax.random.split(key, 40)
    f32 = jnp.float32

    def nrm(k, shape, s):
        return s * jax.random.normal(k, shape, f32)

    G, P, H = S5_N_GROUPS, S5_STATE, S5_GROUP
    gdn_dt = jnp.exp(jax.random.uniform(ks[31], (DEPTH, GDN_HEADS), f32, minval=math.log(1e-3), maxval=math.log(1e-1)))
    lru_u = jax.random.uniform(ks[36], (DEPTH, LRU_WIDTH), f32, minval=0.9, maxval=0.999)
    lru_a0 = lru_u ** (1.0 / LRU_C)
    return {
        "x_prompt": nrm(ks[0], (BATCH, SEQ, D_MODEL), 1.0),
        "x_sample": nrm(ks[1], (DEC_BATCH, DEC_SEQ, D_MODEL), 1.0),
        "state_s5_re": nrm(ks[2], (DEPTH, DEC_BATCH, G, P), 0.3),
        "state_s5_im": nrm(ks[3], (DEPTH, DEC_BATCH, G, P), 0.3),
        "state_lru_h": nrm(ks[4], (DEPTH, DEC_BATCH, LRU_WIDTH), 0.3),
        "state_lru_conv": nrm(ks[5], (DEPTH, DEC_BATCH, CONV_WIDTH - 1, LRU_WIDTH), 1.0),
        "state_gdn_S": nrm(ks[6], (DEPTH, DEC_BATCH, GDN_HEADS, GDN_HEAD_DIM, GDN_HEAD_DIM), 0.05),
        "state_gdn_conv": nrm(ks[7], (DEPTH, DEC_BATCH, CONV_WIDTH - 1, 3 * GDN_WIDTH), 1.0),
        "norm_mix_pre": 1.0 + nrm(ks[8], (DEPTH, D_MODEL), 0.05),
        "norm_mix_post": 1.0 + nrm(ks[9], (DEPTH, D_MODEL), 0.05),
        "norm_ffn_pre": 1.0 + nrm(ks[10], (DEPTH, D_MODEL), 0.05),
        "norm_ffn_post": 1.0 + nrm(ks[11], (DEPTH, D_MODEL), 0.05),
        "w_in": nrm(ks[12], (DEPTH, D_MODEL, IN_COLS), D_MODEL ** -0.5),
        "s5_lambda_re": -0.5 + nrm(ks[13], (DEPTH, G, P), 0.01),
        "s5_lambda_im": jnp.pi * jnp.arange(P, dtype=f32) + nrm(ks[14], (DEPTH, G, P), 0.01),
        "s5_log_dt": jax.random.uniform(ks[15], (DEPTH, G), f32, minval=math.log(1e-3), maxval=math.log(1e-1)),
        "s5_b_re": nrm(ks[16], (DEPTH, G, P, H), (2 * H) ** -0.5),
        "s5_b_im": nrm(ks[17], (DEPTH, G, P, H), (2 * H) ** -0.5),
        "s5_c_re": nrm(ks[18], (DEPTH, G, H, P), P ** -0.5),
        "s5_c_im": nrm(ks[19], (DEPTH, G, H, P), P ** -0.5),
        "s5_d": nrm(ks[20], (DEPTH, S5_WIDTH), 1.0),
        "s5_w_glu": nrm(ks[21], (DEPTH, S5_WIDTH, S5_WIDTH), S5_WIDTH ** -0.5),
        "s5_b_glu": nrm(ks[22], (DEPTH, S5_WIDTH), 0.01),
        "lru_conv_w": nrm(ks[23], (DEPTH, CONV_WIDTH, LRU_WIDTH), CONV_WIDTH ** -0.5),
        "lru_conv_b": nrm(ks[24], (DEPTH, LRU_WIDTH), 0.01),
        "lru_w_a": nrm(ks[25], (DEPTH, LRU_BLOCKS, LRU_BLOCK, LRU_BLOCK), LRU_BLOCK ** -0.5),
        "lru_b_a": nrm(ks[26], (DEPTH, LRU_WIDTH), 0.01),
        "lru_w_x": nrm(ks[27], (DEPTH, LRU_BLOCKS, LRU_BLOCK, LRU_BLOCK), LRU_BLOCK ** -0.5),
        "lru_b_x": nrm(ks[28], (DEPTH, LRU_WIDTH), 0.01),
        "lru_lambda": jnp.log(lru_a0) - jnp.log1p(-lru_a0),
        "gdn_conv_w": nrm(ks[29], (DEPTH, CONV_WIDTH, 3 * GDN_WIDTH), CONV_WIDTH ** -0.5),
        "gdn_a_log": jnp.log(jax.random.uniform(ks[30], (DEPTH, GDN_HEADS), f32, minval=1.0, maxval=16.0)),
        "gdn_dt_bias": gdn_dt + jnp.log(-jnp.expm1(-gdn_dt)),
        "gdn_norm_w": 1.0 + nrm(ks[32], (DEPTH, GDN_HEAD_DIM), 0.05),
        "w_out": nrm(ks[33], (DEPTH, MIX_WIDTH, D_MODEL), MIX_WIDTH ** -0.5),
        "ffn_w_gate_up": nrm(ks[34], (DEPTH, D_MODEL, 2 * FFN_HIDDEN), D_MODEL ** -0.5),
        "ffn_w_down": nrm(ks[35], (DEPTH, FFN_HIDDEN, D_MODEL), FFN_HIDDEN ** -0.5),
    }


def reference(x_prompt, x_sample, state_s5_re, state_s5_im, state_lru_h, state_lru_conv,
              state_gdn_S, state_gdn_conv, norm_mix_pre, norm_mix_post, norm_ffn_pre,
              norm_ffn_post, w_in, s5_lambda_re, s5_lambda_im, s5_log_dt, s5_b_re, s5_b_im,
              s5_c_re, s5_c_im, s5_d, s5_w_glu, s5_b_glu, lru_conv_w, lru_conv_b, lru_w_a,
              lru_b_a, lru_w_x, lru_b_x, lru_lambda, gdn_conv_w, gdn_a_log, gdn_dt_bias,
              gdn_norm_w, w_out, ffn_w_gate_up, ffn_w_down):
    f32 = jnp.float32
    bp = x_prompt.shape[0]
    pos_prompt = jnp.arange(x_prompt.shape[1], dtype=jnp.int32)
    pos_sample = PAST_LEN + jnp.arange(x_sample.shape[1], dtype=jnp.int32)
    y_p, y_s = x_prompt, x_sample
    new_p = [[] for _ in range(6)]
    new_s = [[] for _ in range(6)]
    for l in range(DEPTH):
        p = {
            "norm_mix_pre": norm_mix_pre[l], "norm_mix_post": norm_mix_post[l],
            "norm_ffn_pre": norm_ffn_pre[l], "norm_ffn_post": norm_ffn_post[l],
            "w_in": w_in[l], "s5_lambda_re": s5_lambda_re[l], "s5_lambda_im": s5_lambda_im[l],
            "s5_log_dt": s5_log_dt[l], "s5_b_re": s5_b_re[l], "s5_b_im": s5_b_im[l],
            "s5_c_re": s5_c_re[l], "s5_c_im": s5_c_im[l], "s5_d": s5_d[l],
            "s5_w_glu": s5_w_glu[l], "s5_b_glu": s5_b_glu[l],
            "lru_conv_w": lru_conv_w[l], "lru_conv_b": lru_conv_b[l], "lru_w_a": lru_w_a[l],
            "lru_b_a": lru_b_a[l], "lru_w_x": lru_w_x[l], "lru_b_x": lru_b_x[l],
            "lru_lambda": lru_lambda[l], "gdn_conv_w": gdn_conv_w[l], "gdn_a_log": gdn_a_log[l],
            "gdn_dt_bias": gdn_dt_bias[l], "gdn_norm_w": gdn_norm_w[l], "w_out": w_out[l],
            "ffn_w_gate_up": ffn_w_gate_up[l], "ffn_w_down": ffn_w_down[l],
        }
        st_p = (jnp.zeros((bp, S5_N_GROUPS, S5_STATE), f32),
                jnp.zeros((bp, S5_N_GROUPS, S5_STATE), f32),
                jnp.zeros((bp, LRU_WIDTH), f32),
                jnp.zeros((bp, CONV_WIDTH - 1, LRU_WIDTH), x_prompt.dtype),
                jnp.zeros((bp, GDN_HEADS, GDN_HEAD_DIM, GDN_HEAD_DIM), f32),
                jnp.zeros((bp, CONV_WIDTH - 1, 3 * GDN_WIDTH), x_prompt.dtype))
        st_s = (state_s5_re[l], state_s5_im[l], state_lru_h[l], state_lru_conv[l],
                state_gdn_S[l], state_gdn_conv[l])
        y_p, sp = layer(y_p, pos_prompt, st_p, p)
        y_s, ss = layer(y_s, pos_sample, st_s, p)
        for i in range(6):
            new_p[i].append(sp[i])
            new_s[i].append(ss[i])
    p_s5_re, p_s5_im, p_lru_h, p_lru_conv, p_gdn_S, p_gdn_conv = (jnp.stack(t, axis=0) for t in new_p)
    s_s5_re, s_s5_im, s_lru_h, s_lru_conv, s_gdn_S, s_gdn_conv = (jnp.stack(t, axis=0) for t in new_s)
    return (y_p, y_s, p_s5_re, p_s5_im, p_lru_h, p_lru_conv, p_gdn_S, p_gdn_conv,
            s_s5_re, s_s5_im, s_lru_h, s_lru_conv, s_gdn_S, s_gdn_conv)
```

```python
import functools

import jax
import jax.numpy as jnp
from jax import lax
from jax.experimental import pallas as pl
from jax.experimental.pallas import tpu as pltpu

F32 = jnp.float32
BF16 = jnp.bfloat16
HIGHEST = lax.Precision.HIGHEST

NORM_EPS = 1e-6
LRU_C = 8.0
GDN_CHUNK = 64
PAST_LEN = 16384
LANES = 128
SUBLANES = 8
VMEM_LIMIT = 56 * 1024 * 1024


def _rms(x, w):
    return x * lax.rsqrt(jnp.mean(x * x, axis=-1, keepdims=True) + NORM_EPS) * w


def _sigmoid(x):
    return jax.nn.sigmoid(x)


def _softplus(x):
    return jnp.maximum(x, 0.0) + jnp.log1p(jnp.exp(-jnp.abs(x)))


def _mm(a, b):
    return jnp.dot(a.astype(BF16), b.astype(BF16), preferred_element_type=F32)


def _mm_nt(a, b):
    return lax.dot_general(a.astype(BF16), b.astype(BF16), (((1,), (1,)), ((), ())),
                           preferred_element_type=F32)


def _mm_tn(a, b):
    return lax.dot_general(a.astype(BF16), b.astype(BF16), (((0,), (0,)), ((), ())),
                           preferred_element_type=F32)


def _const_spec(shape):
    n = len(shape)
    return pl.BlockSpec(shape, lambda *_: (0,) * n, pipeline_mode=pl.Buffered(1))


def _in_proj_kernel(x_ref, nw_ref, w_ref, sl_ref, gd_ref, *, n_sl, to_time_major):
    nb, tc, d = x_ref.shape
    x = x_ref[...].reshape(nb * tc, d)
    h = _rms(x, nw_ref[...])
    p = _mm(h, w_ref[...])
    sl = p[:, :n_sl].reshape(nb, tc, n_sl)
    if to_time_major:
        sl = jnp.swapaxes(sl, 0, 1)
    sl_ref[...] = sl
    gd_ref[...] = p[:, n_sl:].reshape(nb, tc, p.shape[1] - n_sl)


def _in_proj(x3, nw, w, *, n_sl, tc, to_time_major):
    nb, length, d = x3.shape
    n_tot = w.shape[1]
    n_gd = n_tot - n_sl
    if to_time_major:
        sl_shape, sl_spec = (length, nb, n_sl), pl.BlockSpec((tc, nb, n_sl), lambda i: (i, 0, 0))
    else:
        sl_shape, sl_spec = (nb, length, n_sl), pl.BlockSpec((nb, tc, n_sl), lambda i: (0, i, 0))
    return pl.pallas_call(
        functools.partial(_in_proj_kernel, n_sl=n_sl, to_time_major=to_time_major),
        grid=(length // tc,),
        in_specs=[pl.BlockSpec((nb, tc, d), lambda i: (0, i, 0)),
                  _const_spec((1, d)), _const_spec((d, n_tot))],
        out_specs=[sl_spec, pl.BlockSpec((nb, tc, n_gd), lambda i: (0, i, 0))],
        out_shape=[jax.ShapeDtypeStruct(sl_shape, F32), jax.ShapeDtypeStruct((nb, length, n_gd), F32)],
        compiler_params=pltpu.CompilerParams(dimension_semantics=("arbitrary",),
                                             vmem_limit_bytes=VMEM_LIMIT),
        name="in_proj",
    )(x3, nw, w)


def _scan_kernel(sl_ref, lre_ref, lim_ref, ldt_ref, bre_ref, bim_ref, cre_ref, cim_ref, d_ref,
                 wglu_ref, bglu_ref, cw_ref, cb_ref, wa_ref, ba_ref, wx_ref, bx_ref, lam_ref,
                 xr0_ref, xi0_ref, h0_ref, cv0_ref,
                 y_ref, xr_out, xi_out, h_out, cv_out,
                 ar_s, ai_s, bbr_s, bbi_s, crb_s, cib_s, xr_c, xi_c, h_c, xr_s, xi_s, xpad_s, a_s, b_s,
                 *, nb, tc, s5w, lw, first_pos_is_zero):
    i = pl.program_id(0)
    last = pl.num_programs(0) - 1
    rows = nb * tc

    @pl.when(i == 0)
    def _init():
        lr = jnp.minimum(lre_ref[...], -1e-4)
        li = lim_ref[...]
        dt = jnp.exp(ldt_ref[...])
        mag = jnp.exp(lr * dt)
        ar = mag * jnp.cos(li * dt)
        ai = mag * jnp.sin(li * dt)
        den = lr * lr + li * li
        fr = ((ar - 1.0) * lr + ai * li) / den
        fi = (ai * lr - (ar - 1.0) * li) / den
        ar_s[...] = ar
        ai_s[...] = ai
        bre = bre_ref[...]
        bim = bim_ref[...]
        bbr_s[...] = (fr * bre - fi * bim).astype(BF16)
        bbi_s[...] = (fr * bim + fi * bre).astype(BF16)
        crb_s[...] = cre_ref[...].astype(BF16)
        cib_s[...] = cim_ref[...].astype(BF16)
        xr_c[...] = xr0_ref[...]
        xi_c[...] = xi0_ref[...]
        h_c[...] = h0_ref[...]
        xpad_s[0:3 * nb, :] = cv0_ref[...]

    u = sl_ref[:, 0:s5w]
    xb = sl_ref[:, s5w:s5w + lw]
    gate = sl_ref[:, s5w + lw:s5w + 2 * lw]

    ub = u.astype(BF16)
    xr_s[...] = jnp.dot(ub, bbr_s[...], preferred_element_type=F32)
    xi_s[...] = jnp.dot(ub, bbi_s[...], preferred_element_type=F32)

    xpad_s[3 * nb:3 * nb + rows, :] = xb
    cw = cw_ref[...]
    xc = (xpad_s[0:rows, :] * cw[0:1] + xpad_s[nb:nb + rows, :] * cw[1:2]
          + xpad_s[2 * nb:2 * nb + rows, :] * cw[2:3] + xb * cw[3:4]) + cb_ref[...]
    new_cv = xpad_s[rows:rows + 3 * nb, :]
    xpad_s[0:3 * nb, :] = new_cv
    r = _sigmoid(_mm(xc, wa_ref[...]) + ba_ref[...])
    gi = _sigmoid(_mm(xc, wx_ref[...]) + bx_ref[...])
    log_a = -LRU_C * r * _softplus(-lam_ref[...])
    a = jnp.exp(log_a)
    mult = jnp.sqrt(-jnp.tanh(log_a) * (a * a + 1.0))
    if first_pos_is_zero:
        rid = lax.broadcasted_iota(jnp.int32, (rows, 1), 0)
        mult = jnp.where(jnp.logical_and(rid < nb, i == 0), 1.0, mult)
    a_s[...] = a
    b_s[...] = mult * gi * xc

    ar = jnp.broadcast_to(ar_s[...], (nb, ar_s.shape[1]))
    ai = jnp.broadcast_to(ai_s[...], (nb, ai_s.shape[1]))

    def step(t, carry):
        xr, xi, h = carry
        sl_t = pl.ds(pl.multiple_of(t * nb, nb), nb)
        nxr = ar * xr - ai * xi + xr_s[sl_t, :]
        nxi = ar * xi + ai * xr + xi_s[sl_t, :]
        nh = a_s[sl_t, :] * h + b_s[sl_t, :]
        xr_s[sl_t, :] = nxr
        xi_s[sl_t, :] = nxi
        b_s[sl_t, :] = nh
        return nxr, nxi, nh

    xr, xi, h = lax.fori_loop(0, tc, step, (xr_c[...], xi_c[...], h_c[...]),
                              unroll=(True if tc <= 8 else 8))
    xr_c[...] = xr
    xi_c[...] = xi
    h_c[...] = h

    y = (jnp.dot(xr_s[...].astype(BF16), crb_s[...], preferred_element_type=F32)
         - jnp.dot(xi_s[...].astype(BF16), cib_s[...], preferred_element_type=F32))
    y = jax.nn.gelu(y + d_ref[...] * u)
    y = y * _sigmoid(_mm(y, wglu_ref[...]) + bglu_ref[...])
    y_ref[:, 0:s5w] = y
    y_ref[:, s5w:s5w + lw] = b_s[...] * jax.nn.gelu(gate)

    @pl.when(i == last)
    def _fin():
        xr_out[...] = xr
        xi_out[...] = xi
        h_out[...] = h
        cv_out[...] = new_cv


def _scan(sl2, p, xr0, xi0, h0, cv0, *, nb, tc, first_pos_is_zero):
    rows_total, n_sl = sl2.shape
    s5w = p["s5_d"].shape[1]
    lw = p["lru_lambda"].shape[1]
    n_state = p["s5_lre"].shape[1]
    rows = nb * tc
    params = [p["s5_lre"], p["s5_lim"], p["s5_ldt"], p["s5_bre"], p["s5_bim"], p["s5_cre"], p["s5_cim"],
              p["s5_d"], p["s5_wglu"], p["s5_bglu"], p["lru_cw"], p["lru_cb"], p["lru_wa"], p["lru_ba"],
              p["lru_wx"], p["lru_bx"], p["lru_lambda"]]
    states = [xr0, xi0, h0, cv0]
    return pl.pallas_call(
        functools.partial(_scan_kernel, nb=nb, tc=tc, s5w=s5w, lw=lw, first_pos_is_zero=first_pos_is_zero),
        grid=(rows_total // rows,),
        in_specs=[pl.BlockSpec((rows, n_sl), lambda i: (i, 0))]
        + [_const_spec(a.shape) for a in params + states],
        out_specs=[pl.BlockSpec((rows, s5w + lw), lambda i: (i, 0)),
                   pl.BlockSpec((nb, n_state), lambda i: (0, 0)),
                   pl.BlockSpec((nb, n_state), lambda i: (0, 0)),
                   pl.BlockSpec((nb, lw), lambda i: (0, 0)),
                   pl.BlockSpec((3 * nb, lw), lambda i: (0, 0))],
        out_shape=[jax.ShapeDtypeStruct((rows_total, s5w + lw), F32),
                   jax.ShapeDtypeStruct((nb, n_state), F32),
                   jax.ShapeDtypeStruct((nb, n_state), F32),
                   jax.ShapeDtypeStruct((nb, lw), F32),
                   jax.ShapeDtypeStruct((3 * nb, lw), F32)],
        scratch_shapes=[pltpu.VMEM((1, n_state), F32), pltpu.VMEM((1, n_state), F32),
                        pltpu.VMEM((s5w, n_state), BF16), pltpu.VMEM((s5w, n_state), BF16),
                        pltpu.VMEM((n_state, s5w), BF16), pltpu.VMEM((n_state, s5w), BF16),
                        pltpu.VMEM((nb, n_state), F32), pltpu.VMEM((nb, n_state), F32),
                        pltpu.VMEM((nb, lw), F32),
                        pltpu.VMEM((rows, n_state), F32), pltpu.VMEM((rows, n_state), F32),
                        pltpu.VMEM((rows + 3 * nb, lw), F32),
                        pltpu.VMEM((rows, lw), F32), pltpu.VMEM((rows, lw), F32)],
        compiler_params=pltpu.CompilerParams(dimension_semantics=("arbitrary",),
                                             vmem_limit_bytes=VMEM_LIMIT),
        name="scan",
    )(sl2, *params, *states)


def _gdn_prompt_kernel(gd_ref, cw_ref, alog_ref, dtb_ref, nw_ref, y_ref, s_out, cv_out,
                       s_s, xp_s, *, rg, nh, hd):
    i = pl.program_id(1)
    last = pl.num_programs(1) - 1
    w3 = 3 * nh * hd
    wz = nh * hd
    c = GDN_CHUNK

    @pl.when(i == 0)
    def _init():
        s_s[...] = jnp.zeros_like(s_s)
        xp_s[0:SUBLANES, :] = jnp.zeros((SUBLANES, w3), F32)

    qkv = gd_ref[:, 0:w3]
    xp_s[SUBLANES:SUBLANES + rg, :] = qkv
    cw = cw_ref[...]
    xc = (xp_s[5:5 + rg, :] * cw[0:1] + xp_s[6:6 + rg, :] * cw[1:2]
          + xp_s[7:7 + rg, :] * cw[2:3] + qkv * cw[3:4])
    tail = xp_s[rg:rg + SUBLANES, :]
    xp_s[0:SUBLANES, :] = tail
    xc = xc * _sigmoid(xc)

    ab = gd_ref[:, w3 + wz:w3 + wz + LANES]
    g_all = -jnp.exp(alog_ref[...]) * _softplus(ab + dtb_ref[...])
    beta_all = _sigmoid(ab)

    ri = lax.broadcasted_iota(jnp.int32, (c, c), 0)
    ci = lax.broadcasted_iota(jnp.int32, (c, c), 1)
    incl = ri >= ci
    strict = ri > ci
    ltri = incl.astype(F32)
    eye = (ri == ci).astype(F32)
    scale = hd ** -0.5

    s_cur = [s_s[h] for h in range(nh)]
    for cc in range(rg // c):
        r0 = cc * c
        gcum = jnp.dot(ltri, g_all[r0:r0 + c, :], precision=HIGHEST, preferred_element_type=F32)
        gcum_t = gcum.T
        for h in range(nh):
            q = xc[r0:r0 + c, h * hd:(h + 1) * hd]
            k = xc[r0:r0 + c, wz + h * hd:wz + (h + 1) * hd]
            v = xc[r0:r0 + c, 2 * wz + h * hd:2 * wz + (h + 1) * hd]
            q = q * (lax.rsqrt(jnp.sum(q * q, axis=-1, keepdims=True) + NORM_EPS) * scale)
            k = k * lax.rsqrt(jnp.sum(k * k, axis=-1, keepdims=True) + NORM_EPS)
            beta = beta_all[r0:r0 + c, nh + h:nh + h + 1]
            gc = gcum[:, h:h + 1]
            gcr = gcum_t[h:h + 1, :]
            g_last = gcum[c - 1:c, h:h + 1]
            eg = jnp.exp(gc)
            decay = jnp.exp(jnp.where(incl, gc - gcr, -jnp.inf))
            kb = k * beta
            vb = v * beta
            kk = _mm_nt(jnp.concatenate([kb, q], axis=0), k)
            m = jnp.where(strict, kk[0:c] * decay, 0.0)
            attn = kk[c:2 * c] * decay
            x = -m
            t_inv = eye + x
            for _ in range(5):
                x = _mm(x, x)
                t_inv = t_inv + _mm(t_inv, x)
            w = _mm(t_inv, jnp.concatenate([vb, kb * eg], axis=1))
            w_v = w[:, 0:hd]
            w_k = w[:, hd:2 * hd]
            q_dec = q * eg
            k_dec = k * jnp.exp(g_last - gc)
            s_h = s_cur[h]
            v_new = w_v - _mm(w_k, s_h)
            o = _mm(q_dec, s_h) + _mm(attn, v_new)
            s_cur[h] = s_h * jnp.exp(g_last) + _mm_tn(k_dec, v_new)
            z = gd_ref[r0:r0 + c, w3 + h * hd:w3 + (h + 1) * hd]
            y_ref[r0:r0 + c, h * hd:(h + 1) * hd] = _rms(o, nw_ref[...]) * (z * _sigmoid(z))
    for h in range(nh):
        s_s[h] = s_cur[h]

    @pl.when(i == last)
    def _fin():
        for h in range(nh):
            s_out[h] = s_cur[h]
        cv_out[...] = tail[5:8, :]


def _gdn_prompt(gd3, p, *, rg):
    nb, length, n_gd = gd3.shape
    nh = p["gdn_nh"]
    hd = p["gdn_nw"].shape[1]
    w3 = 3 * nh * hd
    return pl.pallas_call(
        functools.partial(_gdn_prompt_kernel, rg=rg, nh=nh, hd=hd),
        grid=(nb, length // rg),
        in_specs=[pl.BlockSpec((None, rg, n_gd), lambda b, i: (b, i, 0)),
                  pl.BlockSpec((4, w3), lambda b, i: (0, 0)),
                  pl.BlockSpec((1, LANES), lambda b, i: (0, 0)),
                  pl.BlockSpec((1, LANES), lambda b, i: (0, 0)),
                  pl.BlockSpec((1, hd), lambda b, i: (0, 0))],
        out_specs=[pl.BlockSpec((None, rg, nh * hd), lambda b, i: (b, i, 0)),
                   pl.BlockSpec((None, nh, hd, hd), lambda b, i: (b, 0, 0, 0)),
                   pl.BlockSpec((None, 3, w3), lambda b, i: (b, 0, 0))],
        out_shape=[jax.ShapeDtypeStruct((nb, length, nh * hd), F32),
                   jax.ShapeDtypeStruct((nb, nh, hd, hd), F32),
                   jax.ShapeDtypeStruct((nb, 3, w3), F32)],
        scratch_shapes=[pltpu.VMEM((nh, hd, hd), F32), pltpu.VMEM((rg + SUBLANES, w3), F32)],
        compiler_params=pltpu.CompilerParams(dimension_semantics=("arbitrary", "arbitrary"),
                                             vmem_limit_bytes=VMEM_LIMIT),
        name="gdn_prompt",
    )(gd3, p["gdn_cw"], p["gdn_alog"], p["gdn_dtb"], p["gdn_nw"])


def _gdn_sample_kernel(gd_ref, cv0_ref, s0_ref, cw_ref, alog_ref, dtb_ref, nw_ref,
                       y_ref, s_out, cv_out, kq_s, v_s, gb_s, o_s, *, nt, nbk, nh, hd):
    w3 = 3 * nh * hd
    wz = nh * hd
    cw = cw_ref[...]
    scale = hd ** -0.5
    xs = [cv0_ref[j] for j in range(3)] + [gd_ref[t, :, 0:w3] for t in range(nt)]
    ks, qs, vs, gs, bs = [], [], [], [], []
    for t in range(nt):
        xc = xs[t] * cw[0:1] + xs[t + 1] * cw[1:2] + xs[t + 2] * cw[2:3] + xs[t + 3] * cw[3:4]
        xc = xc * _sigmoid(xc)
        ab = gd_ref[t, :, w3 + wz:w3 + wz + LANES]
        eg_all = jnp.exp(-jnp.exp(alog_ref[...]) * _softplus(ab + dtb_ref[...]))
        beta_all = _sigmoid(ab)
        qh, kh, gh, bh = [], [], [], []
        for h in range(nh):
            q = xc[:, h * hd:(h + 1) * hd]
            k = xc[:, wz + h * hd:wz + (h + 1) * hd]
            qh.append(q * (lax.rsqrt(jnp.sum(q * q, axis=-1, keepdims=True) + NORM_EPS) * scale))
            kh.append(k * lax.rsqrt(jnp.sum(k * k, axis=-1, keepdims=True) + NORM_EPS))
            gh.append(jnp.broadcast_to(eg_all[:, h:h + 1], (nbk, hd)))
            bh.append(jnp.broadcast_to(beta_all[:, nh + h:nh + h + 1], (nbk, hd)))
        qs.append(jnp.concatenate(qh, axis=-1))
        ks.append(jnp.concatenate(kh, axis=-1))
        gs.append(jnp.concatenate(gh, axis=-1))
        bs.append(jnp.concatenate(bh, axis=-1))
        vs.append(xc[:, 2 * wz:3 * wz])
    for j in range(3):
        cv_out[j] = xs[nt + j]
    to_seq = lambda a, b: jnp.swapaxes(jnp.stack(a + b, axis=0), 0, 1)
    kq_s[...] = to_seq(ks, qs)
    v_s[...] = to_seq(vs, vs)
    gb_s[...] = to_seq(gs, bs)

    def per_seq(b, _):
        kq = kq_s[b]
        vv = v_s[b]
        gb = gb_s[b]
        for h in range(nh):
            cols = slice(h * hd, (h + 1) * hd)
            kq_t = kq[:, cols].T
            s = s0_ref[b, h]
            outs = []
            for t in range(nt):
                kcol = kq_t[:, t:t + 1]
                qcol = kq_t[:, nt + t:nt + t + 1]
                s = s * gb[t:t + 1, cols]
                u = jnp.sum(s * kcol, axis=0, keepdims=True)
                r = gb[nt + t:nt + t + 1, cols] * (vv[t:t + 1, cols] - u)
                s = s + kcol * r
                outs.append(jnp.sum(s * qcol, axis=0, keepdims=True))
            s_out[b, h] = s
            o_s[b, :, cols] = jnp.concatenate(outs + outs, axis=0)
        return 0

    lax.fori_loop(0, nbk, per_seq, 0)

    o_tm = jnp.swapaxes(o_s[...], 0, 1)
    for t in range(nt):
        for h in range(nh):
            cols = slice(h * hd, (h + 1) * hd)
            z = gd_ref[t, :, w3 + h * hd:w3 + (h + 1) * hd]
            y_ref[t, :, cols] = _rms(o_tm[t][:, cols], nw_ref[...]) * (z * _sigmoid(z))


def _gdn_sample(gd3, cv0, s0, p, *, nbk):
    nt, nb, n_gd = gd3.shape
    nh = p["gdn_nh"]
    hd = p["gdn_nw"].shape[1]
    w3 = 3 * nh * hd
    wz = nh * hd
    return pl.pallas_call(
        functools.partial(_gdn_sample_kernel, nt=nt, nbk=nbk, nh=nh, hd=hd),
        grid=(nb // nbk,),
        in_specs=[pl.BlockSpec((nt, nbk, n_gd), lambda i: (0, i, 0)),
                  pl.BlockSpec((3, nbk, w3), lambda i: (0, i, 0)),
                  pl.BlockSpec((nbk, nh, hd, hd), lambda i: (i, 0, 0, 0)),
                  pl.BlockSpec((4, w3), lambda i: (0, 0)),
                  pl.BlockSpec((1, LANES), lambda i: (0, 0)),
                  pl.BlockSpec((1, LANES), lambda i: (0, 0)),
                  pl.BlockSpec((1, hd), lambda i: (0, 0))],
        out_specs=[pl.BlockSpec((nt, nbk, wz), lambda i: (0, i, 0)),
                   pl.BlockSpec((nbk, nh, hd, hd), lambda i: (i, 0, 0, 0)),
                   pl.BlockSpec((3, nbk, w3), lambda i: (0, i, 0))],
        out_shape=[jax.ShapeDtypeStruct((nt, nb, wz), F32),
                   jax.ShapeDtypeStruct((nb, nh, hd, hd), F32),
                   jax.ShapeDtypeStruct((3, nb, w3), F32)],
        scratch_shapes=[pltpu.VMEM((nbk, 2 * nt, wz), F32) for _ in range(4)],
        compiler_params=pltpu.CompilerParams(dimension_semantics=("arbitrary",),
                                             vmem_limit_bytes=VMEM_LIMIT),
        name="gdn_sample",
    )(gd3, cv0, s0, p["gdn_cw"], p["gdn_alog"], p["gdn_dtb"], p["gdn_nw"])


def _ffn_kernel(x_ref, ysl_ref, yg_ref, wo_ref, nmp_ref, nfp_ref, nfq_ref, wgu_ref, wd_ref, o_ref,
                *, from_time_major):
    nb, tc, d = x_ref.shape
    rows = nb * tc
    x = x_ref[...].reshape(rows, d)
    ysl = ysl_ref[...]
    if from_time_major:
        ysl = jnp.swapaxes(ysl, 0, 1)
    ysl = ysl.reshape(rows, ysl.shape[-1])
    yg = yg_ref[...].reshape(rows, yg_ref.shape[-1])
    mix = jnp.concatenate([ysl, yg], axis=-1)
    x = x + _rms(_mm(mix, wo_ref[...]), nmp_ref[...])
    h = _rms(x, nfp_ref[...])
    gu = _mm(h, wgu_ref[...])
    hid = gu.shape[1] // 2
    gt = gu[:, :hid]
    f = _mm(gt * _sigmoid(gt) * gu[:, hid:], wd_ref[...])
    x = x + _rms(f, nfq_ref[...])
    o_ref[...] = x.reshape(nb, tc, d)


def _ffn(x3, ysl3, yg3, p, *, tc, from_time_major):
    nb, length, d = x3.shape
    n1 = ysl3.shape[-1]
    n2 = yg3.shape[-1]
    if from_time_major:
        ysl_spec = pl.BlockSpec((tc, nb, n1), lambda i: (i, 0, 0))
    else:
        ysl_spec = pl.BlockSpec((nb, tc, n1), lambda i: (0, i, 0))
    return pl.pallas_call(
        functools.partial(_ffn_kernel, from_time_major=from_time_major),
        grid=(length // tc,),
        in_specs=[pl.BlockSpec((nb, tc, d), lambda i: (0, i, 0)), ysl_spec,
                  pl.BlockSpec((nb, tc, n2), lambda i: (0, i, 0)),
                  _const_spec(p["w_out"].shape), _const_spec((1, d)), _const_spec((1, d)),
                  _const_spec((1, d)), _const_spec(p["w_gu"].shape), _const_spec(p["w_down"].shape)],
        out_specs=pl.BlockSpec((nb, tc, d), lambda i: (0, i, 0)),
        out_shape=jax.ShapeDtypeStruct((nb, length, d), F32),
        compiler_params=pltpu.CompilerParams(dimension_semantics=("arbitrary",),
                                             vmem_limit_bytes=VMEM_LIMIT),
        name="ffn",
    )(x3, ysl3, yg3, p["w_out"], p["n_mix_post"], p["n_ffn_pre"], p["n_ffn_post"], p["w_gu"], p["w_down"])


def _block_diag(blocks):
    n, r, c = blocks.shape
    eye = jnp.eye(n, dtype=blocks.dtype)
    return (eye[:, None, :, None] * blocks[:, :, None, :]).reshape(n * r, n * c)


def _pad_lanes(v, offset):
    return jnp.zeros((1, LANES), F32).at[0, offset:offset + v.shape[0]].set(v)


def _layer_params(l, norm_mix_pre, norm_mix_post, norm_ffn_pre, norm_ffn_post, w_in, s5_lambda_re,
                  s5_lambda_im, s5_log_dt, s5_b_re, s5_b_im, s5_c_re, s5_c_im, s5_d, s5_w_glu, s5_b_glu,
                  lru_conv_w, lru_conv_b, lru_w_a, lru_b_a, lru_w_x, lru_b_x, lru_lambda, gdn_conv_w,
                  gdn_a_log, gdn_dt_bias, gdn_norm_w, w_out, ffn_w_gate_up, ffn_w_down):
    g, pst, hgrp = s5_b_re.shape[1:]
    s5w = g * hgrp
    lw = lru_lambda.shape[1]
    nh = gdn_a_log.shape[1]
    hd = gdn_norm_w.shape[1]
    gw = nh * hd
    row = lambda v: v[l][None, :]
    w = w_in[l]
    n_sl = s5w + 2 * lw
    c_qkv, c_a, c_b, c_z = n_sl, n_sl + 3 * gw, n_sl + 3 * gw + nh, n_sl + 3 * gw + 2 * nh
    w_r = jnp.concatenate([w[:, :c_a], w[:, c_z:c_z + gw], w[:, c_a:c_z],
                           jnp.zeros((w.shape[0], LANES - 2 * nh), w.dtype)], axis=1).astype(BF16)
    return {
        "n_sl": n_sl,
        "n_mix_pre": row(norm_mix_pre), "n_mix_post": row(norm_mix_post),
        "n_ffn_pre": row(norm_ffn_pre), "n_ffn_post": row(norm_ffn_post),
        "w_in": w_r,
        "s5_lre": s5_lambda_re[l].reshape(1, g * pst), "s5_lim": s5_lambda_im[l].reshape(1, g * pst),
        "s5_ldt": jnp.repeat(s5_log_dt[l], pst)[None, :],
        "s5_bre": _block_diag(jnp.swapaxes(s5_b_re[l], 1, 2)),
        "s5_bim": _block_diag(jnp.swapaxes(s5_b_im[l], 1, 2)),
        "s5_cre": _block_diag(jnp.swapaxes(s5_c_re[l], 1, 2)),
        "s5_cim": _block_diag(jnp.swapaxes(s5_c_im[l], 1, 2)),
        "s5_d": row(s5_d), "s5_wglu": s5_w_glu[l].astype(BF16), "s5_bglu": row(s5_b_glu),
        "lru_cw": lru_conv_w[l], "lru_cb": row(lru_conv_b),
        "lru_wa": _block_diag(lru_w_a[l]).astype(BF16), "lru_ba": row(lru_b_a),
        "lru_wx": _block_diag(lru_w_x[l]).astype(BF16), "lru_bx": row(lru_b_x),
        "lru_lambda": row(lru_lambda),
        "gdn_cw": gdn_conv_w[l], "gdn_alog": _pad_lanes(gdn_a_log[l], 0),
        "gdn_dtb": _pad_lanes(gdn_dt_bias[l], 0), "gdn_nw": row(gdn_norm_w), "gdn_nh": nh,
        "w_out": w_out[l].astype(BF16), "w_gu": ffn_w_gate_up[l].astype(BF16),
        "w_down": ffn_w_down[l].astype(BF16),
    }


def _prompt_layer(x, p, *, tc, rg):
    nb, length, _ = x.shape
    sl_tm, gd = _in_proj(x, p["n_mix_pre"], p["w_in"], n_sl=p["n_sl"], tc=tc, to_time_major=True)
    n_state = p["s5_lre"].shape[1]
    lw = p["lru_lambda"].shape[1]
    z = lambda *s: jnp.zeros(s, F32)
    ysl, xr, xi, hl, cv = _scan(sl_tm.reshape(length * nb, -1), p, z(nb, n_state), z(nb, n_state),
                                z(nb, lw), z(3 * nb, lw), nb=nb, tc=tc, first_pos_is_zero=True)
    yg, s_new, gcv = _gdn_prompt(gd, p, rg=rg)
    x = _ffn(x, ysl.reshape(length, nb, -1), yg, p, tc=tc, from_time_major=True)
    lcv = jnp.swapaxes(cv.reshape(3, nb, lw), 0, 1)
    return x, (xr, xi, hl, lcv, s_new, gcv)


def _sample_layer(x_tm, st, p, *, nbk):
    s5_re, s5_im, lru_h, lru_conv, gdn_s, gdn_conv = st
    nb = s5_re.shape[0]
    rows = x_tm.shape[1]
    nt = rows // nb
    sl, gd = _in_proj(x_tm, p["n_mix_pre"], p["w_in"], n_sl=p["n_sl"], tc=rows, to_time_major=False)
    lw = p["lru_lambda"].shape[1]
    cv0 = jnp.swapaxes(lru_conv, 0, 1).reshape(3 * nb, lw)
    ysl, xr, xi, hl, cv = _scan(sl.reshape(rows, -1), p, s5_re.reshape(nb, -1), s5_im.reshape(nb, -1),
                                lru_h, cv0, nb=nb, tc=nt, first_pos_is_zero=(PAST_LEN == 0))
    yg, s_new, gcv = _gdn_sample(gd.reshape(nt, nb, -1), jnp.swapaxes(gdn_conv, 0, 1), gdn_s, p, nbk=nbk)
    x_tm = _ffn(x_tm, ysl.reshape(1, rows, -1), yg.reshape(1, rows, -1), p, tc=rows, from_time_major=False)
    lcv = jnp.swapaxes(cv.reshape(3, nb, lw), 0, 1)
    return x_tm, (xr, xi, hl, lcv, s_new, jnp.swapaxes(gcv, 0, 1))


def kernel(x_prompt, x_sample, state_s5_re, state_s5_im, state_lru_h, state_lru_conv, state_gdn_S, state_gdn_conv, norm_mix_pre, norm_mix_post, norm_ffn_pre, norm_ffn_post, w_in, s5_lambda_re, s5_lambda_im, s5_log_dt, s5_b_re, s5_b_im, s5_c_re, s5_c_im, s5_d, s5_w_glu, s5_b_glu, lru_conv_w, lru_conv_b, lru_w_a, lru_b_a, lru_w_x, lru_b_x, lru_lambda, gdn_conv_w, gdn_a_log, gdn_dt_bias, gdn_norm_w, w_out, ffn_w_gate_up, ffn_w_down):
    depth = w_in.shape[0]
    bp, lp, d = x_prompt.shape
    bs, ls, _ = x_sample.shape
    g, pst = s5_lambda_re.shape[1:]
    tc = min(64, lp)
    rg = min(2 * GDN_CHUNK, lp)
    y_p = x_prompt
    y_s = jnp.swapaxes(x_sample, 0, 1).reshape(1, ls * bs, d)
    new_p = [[] for _ in range(6)]
    new_s = [[] for _ in range(6)]
    for l in range(depth):
        p = _layer_params(l, norm_mix_pre, norm_mix_post, norm_ffn_pre, norm_ffn_post, w_in, s5_lambda_re,
                          s5_lambda_im, s5_log_dt, s5_b_re, s5_b_im, s5_c_re, s5_c_im, s5_d, s5_w_glu,
                          s5_b_glu, lru_conv_w, lru_conv_b, lru_w_a, lru_b_a, lru_w_x, lru_b_x, lru_lambda,
                          gdn_conv_w, gdn_a_log, gdn_dt_bias, gdn_norm_w, w_out, ffn_w_gate_up, ffn_w_down)
        y_p, sp = _prompt_layer(y_p, p, tc=tc, rg=rg)
        st_s = (state_s5_re[l], state_s5_im[l], state_lru_h[l], state_lru_conv[l],
                state_gdn_S[l], state_gdn_conv[l])
        y_s, ss = _sample_layer(y_s, st_s, p, nbk=min(8, bs))
        for j in range(6):
            new_p[j].append(sp[j])
            new_s[j].append(ss[j])
    y_s = jnp.swapaxes(y_s.reshape(ls, bs, d), 0, 1)
    outs_p = [jnp.stack(t, axis=0) for t in new_p]
    outs_s = [jnp.stack(t, axis=0) for t in new_s]
    outs_p[0] = outs_p[0].reshape(depth, bp, g, pst)
    outs_p[1] = outs_p[1].reshape(depth, bp, g, pst)
    outs_s[0] = outs_s[0].reshape(depth, bs, g, pst)
    outs_s[1] = outs_s[1].reshape(depth, bs, g, pst)
    return (y_p, y_s, *outs_p, *outs_s)
```

```python
import functools

import jax
import jax.numpy as jnp
from jax import lax
from jax.experimental import pallas as pl
from jax.experimental.pallas import tpu as pltpu

F32 = jnp.float32
BF16 = jnp.bfloat16
HIGHEST = lax.Precision.HIGHEST

NORM_EPS = 1e-6
LRU_C = 8.0
GDN_CHUNK = 64
PAST_LEN = 16384
LANES = 128
SUBLANES = 8
VMEM_LIMIT = 56 * 1024 * 1024


def _rms(x, w):
    return x * lax.rsqrt(jnp.mean(x * x, axis=-1, keepdims=True) + NORM_EPS) * w


def _sigmoid(x):
    return jax.nn.sigmoid(x)


def _softplus(x):
    return jnp.maximum(x, 0.0) + jnp.log1p(jnp.exp(-jnp.abs(x)))


def _mm(a, b):
    return jnp.dot(a.astype(BF16), b.astype(BF16), preferred_element_type=F32)


def _mm_nt(a, b):
    return lax.dot_general(a.astype(BF16), b.astype(BF16), (((1,), (1,)), ((), ())),
                           preferred_element_type=F32)


def _mm_tn(a, b):
    return lax.dot_general(a.astype(BF16), b.astype(BF16), (((0,), (0,)), ((), ())),
                           preferred_element_type=F32)


def _const_spec(shape):
    n = len(shape)
    return pl.BlockSpec(shape, lambda *_: (0,) * n, pipeline_mode=pl.Buffered(1))


def _in_proj_kernel(x_ref, nw_ref, w_ref, sl_ref, gd_ref, *, n_sl, to_time_major):
    nb, tc, d = x_ref.shape
    x = x_ref[...].reshape(nb * tc, d)
    h = _rms(x, nw_ref[...])
    p = _mm(h, w_ref[...])
    sl = p[:, :n_sl].reshape(nb, tc, n_sl)
    if to_time_major:
        sl = jnp.swapaxes(sl, 0, 1)
    sl_ref[...] = sl
    gd_ref[...] = p[:, n_sl:].reshape(nb, tc, p.shape[1] - n_sl)


def _in_proj(x3, nw, w, *, n_sl, tc, to_time_major):
    nb, length, d = x3.shape
    n_tot = w.shape[1]
    n_gd = n_tot - n_sl
    if to_time_major:
        sl_shape, sl_spec = (length, nb, n_sl), pl.BlockSpec((tc, nb, n_sl), lambda i: (i, 0, 0))
    else:
        sl_shape, sl_spec = (nb, length, n_sl), pl.BlockSpec((nb, tc, n_sl), lambda i: (0, i, 0))
    return pl.pallas_call(
        functools.partial(_in_proj_kernel, n_sl=n_sl, to_time_major=to_time_major),
        grid=(length // tc,),
        in_specs=[pl.BlockSpec((nb, tc, d), lambda i: (0, i, 0)),
                  _const_spec((1, d)), _const_spec((d, n_tot))],
        out_specs=[sl_spec, pl.BlockSpec((nb, tc, n_gd), lambda i: (0, i, 0))],
        out_shape=[jax.ShapeDtypeStruct(sl_shape, F32), jax.ShapeDtypeStruct((nb, length, n_gd), F32)],
        compiler_params=pltpu.CompilerParams(dimension_semantics=("arbitrary",),
                                             vmem_limit_bytes=VMEM_LIMIT),
        name="in_proj",
    )(x3, nw, w)


def _scan_kernel(sl_ref, lre_ref, lim_ref, ldt_ref, bre_ref, bim_ref, cre_ref, cim_ref, d_ref,
                 wglu_ref, bglu_ref, cw_ref, cb_ref, wa_ref, ba_ref, wx_ref, bx_ref, lam_ref,
                 xr0_ref, xi0_ref, h0_ref, cv0_ref,
                 y_ref, xr_out, xi_out, h_out, cv_out,
                 ar_s, ai_s, bbr_s, bbi_s, crb_s, cib_s, xr_c, xi_c, h_c, xr_s, xi_s, xpad_s, a_s, b_s,
                 *, nb, tc, s5w, lw, first_pos_is_zero):
    i = pl.program_id(0)
    last = pl.num_programs(0) - 1
    rows = nb * tc

    @pl.when(i == 0)
    def _init():
        lr = jnp.minimum(lre_ref[...], -1e-4)
        li = lim_ref[...]
        dt = jnp.exp(ldt_ref[...])
        mag = jnp.exp(lr * dt)
        ar = mag * jnp.cos(li * dt)
        ai = mag * jnp.sin(li * dt)
        den = lr * lr + li * li
        fr = ((ar - 1.0) * lr + ai * li) / den
        fi = (ai * lr - (ar - 1.0) * li) / den
        ar_s[...] = ar
        ai_s[...] = ai
        bre = bre_ref[...]
        bim = bim_ref[...]
        bbr_s[...] = (fr * bre - fi * bim).astype(BF16)
        bbi_s[...] = (fr * bim + fi * bre).astype(BF16)
        crb_s[...] = cre_ref[...].astype(BF16)
        cib_s[...] = cim_ref[...].astype(BF16)
        xr_c[...] = xr0_ref[...]
        xi_c[...] = xi0_ref[...]
        h_c[...] = h0_ref[...]
        xpad_s[0:3 * nb, :] = cv0_ref[...]

    u = sl_ref[:, 0:s5w]
    xb = sl_ref[:, s5w:s5w + lw]
    gate = sl_ref[:, s5w + lw:s5w + 2 * lw]

    ub = u.astype(BF16)
    xr_s[...] = jnp.dot(ub, bbr_s[...], preferred_element_type=F32)
    xi_s[...] = jnp.dot(ub, bbi_s[...], preferred_element_type=F32)

    xpad_s[3 * nb:3 * nb + rows, :] = xb
    cw = cw_ref[...]
    xc = (xpad_s[0:rows, :] * cw[0:1] + xpad_s[nb:nb + rows, :] * cw[1:2]
          + xpad_s[2 * nb:2 * nb + rows, :] * cw[2:3] + xb * cw[3:4]) + cb_ref[...]
    new_cv = xpad_s[rows:rows + 3 * nb, :]
    xpad_s[0:3 * nb, :] = new_cv
    r = _sigmoid(_mm(xc, wa_ref[...]) + ba_ref[...])
    gi = _sigmoid(_mm(xc, wx_ref[...]) + bx_ref[...])
    log_a = -LRU_C * r * _softplus(-lam_ref[...])
    a = jnp.exp(log_a)
    mult = jnp.sqrt(-jnp.tanh(log_a) * (a * a + 1.0))
    if first_pos_is_zero:
        rid = lax.broadcasted_iota(jnp.int32, (rows, 1), 0)
        mult = jnp.where(jnp.logical_and(rid < nb, i == 0), 1.0, mult)
    a_s[...] = a
    b_s[...] = mult * gi * xc

    ar = jnp.broadcast_to(ar_s[...], (nb, ar_s.shape[1]))
    ai = jnp.broadcast_to(ai_s[...], (nb, ai_s.shape[1]))

    def step(t, carry):
        xr, xi, h = carry
        sl_t = pl.ds(pl.multiple_of(t * nb, nb), nb)
        nxr = ar * xr - ai * xi + xr_s[sl_t, :]
        nxi = ar * xi + ai * xr + xi_s[sl_t, :]
        nh = a_s[sl_t, :] * h + b_s[sl_t, :]
        xr_s[sl_t, :] = nxr
        xi_s[sl_t, :] = nxi
        b_s[sl_t, :] = nh
        return nxr, nxi, nh

    xr, xi, h = lax.fori_loop(0, tc, step, (xr_c[...], xi_c[...], h_c[...]),
                              unroll=(True if tc <= 8 else 8))
    xr_c[...] = xr
    xi_c[...] = xi
    h_c[...] = h

    y = (jnp.dot(xr_s[...].astype(BF16), crb_s[...], preferred_element_type=F32)
         - jnp.dot(xi_s[...].astype(BF16), cib_s[...], preferred_element_type=F32))
    y = jax.nn.gelu(y + d_ref[...] * u)
    y = y * _sigmoid(_mm(y, wglu_ref[...]) + bglu_ref[...])
    y_ref[:, 0:s5w] = y
    y_ref[:, s5w:s5w + lw] = b_s[...] * jax.nn.gelu(gate)

    @pl.when(i == last)
    def _fin():
        xr_out[...] = xr
        xi_out[...] = xi
        h_out[...] = h
        cv_out[...] = new_cv


def _scan(sl2, p, xr0, xi0, h0, cv0, *, nb, tc, first_pos_is_zero):
    rows_total, n_sl = sl2.shape
    s5w = p["s5_d"].shape[1]
    lw = p["lru_lambda"].shape[1]
    n_state = p["s5_lre"].shape[1]
    rows = nb * tc
    params = [p["s5_lre"], p["s5_lim"], p["s5_ldt"], p["s5_bre"], p["s5_bim"], p["s5_cre"], p["s5_cim"],
              p["s5_d"], p["s5_wglu"], p["s5_bglu"], p["lru_cw"], p["lru_cb"], p["lru_wa"], p["lru_ba"],
              p["lru_wx"], p["lru_bx"], p["lru_lambda"]]
    states = [xr0, xi0, h0, cv0]
    return pl.pallas_call(
        functools.partial(_scan_kernel, nb=nb, tc=tc, s5w=s5w, lw=lw, first_pos_is_zero=first_pos_is_zero),
        grid=(rows_total // rows,),
        in_specs=[pl.BlockSpec((rows, n_sl), lambda i: (i, 0))]
        + [_const_spec(a.shape) for a in params + states],
        out_specs=[pl.BlockSpec((rows, s5w + lw), lambda i: (i, 0)),
                   pl.BlockSpec((nb, n_state), lambda i: (0, 0)),
                   pl.BlockSpec((nb, n_state), lambda i: (0, 0)),
                   pl.BlockSpec((nb, lw), lambda i: (0, 0)),
                   pl.BlockSpec((3 * nb, lw), lambda i: (0, 0))],
        out_shape=[jax.ShapeDtypeStruct((rows_total, s5w + lw), F32),
                   jax.ShapeDtypeStruct((nb, n_state), F32),
                   jax.ShapeDtypeStruct((nb, n_state), F32),
                   jax.ShapeDtypeStruct((nb, lw), F32),
                   jax.ShapeDtypeStruct((3 * nb, lw), F32)],
        scratch_shapes=[pltpu.VMEM((1, n_state), F32), pltpu.VMEM((1, n_state), F32),
                        pltpu.VMEM((s5w, n_state), BF16), pltpu.VMEM((s5w, n_state), BF16),
                        pltpu.VMEM((n_state, s5w), BF16), pltpu.VMEM((n_state, s5w), BF16),
                        pltpu.VMEM((nb, n_state), F32), pltpu.VMEM((nb, n_state), F32),
                        pltpu.VMEM((nb, lw), F32),
                        pltpu.VMEM((rows, n_state), F32), pltpu.VMEM((rows, n_state), F32),
                        pltpu.VMEM((rows + 3 * nb, lw), F32),
                        pltpu.VMEM((rows, lw), F32), pltpu.VMEM((rows, lw), F32)],
        compiler_params=pltpu.CompilerParams(dimension_semantics=("arbitrary",),
                                             vmem_limit_bytes=VMEM_LIMIT),
        name="scan",
    )(sl2, *params, *states)


def _gdn_prompt_kernel(gd_ref, cw_ref, alog_ref, dtb_ref, nw_ref, y_ref, s_out, cv_out,
                       s_s, xp_s, *, rg, nh, hd):
    i = pl.program_id(1)
    last = pl.num_programs(1) - 1
    w3 = 3 * nh * hd
    wz = nh * hd
    c = GDN_CHUNK
    nc = rg // c
    wc = nh * c

    @pl.when(i == 0)
    def _init():
        s_s[...] = jnp.zeros_like(s_s)
        xp_s[0:SUBLANES, :] = jnp.zeros((SUBLANES, w3), F32)

    qkv = gd_ref[:, 0:w3]
    xp_s[SUBLANES:SUBLANES + rg, :] = qkv
    cw = cw_ref[...]
    xc = (xp_s[5:5 + rg, :] * cw[0:1] + xp_s[6:6 + rg, :] * cw[1:2]
          + xp_s[7:7 + rg, :] * cw[2:3] + qkv * cw[3:4])
    tail = xp_s[rg:rg + SUBLANES, :]
    xp_s[0:SUBLANES, :] = tail
    xc = xc * _sigmoid(xc)

    ab = gd_ref[:, w3 + wz:w3 + wz + LANES]
    g_all = -jnp.exp(alog_ref[...]) * _softplus(ab + dtb_ref[...])
    beta_all = _sigmoid(ab)
    scale = hd ** -0.5

    def per_head(fn):
        return jnp.concatenate([fn(h) for h in range(nh)], axis=-1)

    def l2n(x, mul):
        def one(h):
            xh = x[:, h * hd:(h + 1) * hd]
            return xh * (lax.rsqrt(jnp.sum(xh * xh, axis=-1, keepdims=True) + NORM_EPS) * mul)
        return per_head(one)

    q_all = l2n(xc[:, 0:wz], scale)
    k_all = l2n(xc[:, wz:2 * wz], 1.0)
    v_all = xc[:, 2 * wz:3 * wz]
    g_w = per_head(lambda h: jnp.broadcast_to(g_all[:, h:h + 1], (rg, hd)))
    beta_w = per_head(lambda h: jnp.broadcast_to(beta_all[:, nh + h:nh + h + 1], (rg, hd)))
    g_c = per_head(lambda h: jnp.broadcast_to(g_all[:, h:h + 1], (rg, c)))
    kb_all = k_all * beta_w
    vb_all = v_all * beta_w

    ri = lax.broadcasted_iota(jnp.int32, (c, wc), 0)
    cj = lax.broadcasted_iota(jnp.int32, (c, wc), 1) % c
    incl = ri >= cj
    strict = ri > cj
    eye = (ri == cj).astype(F32)
    r1 = lax.broadcasted_iota(jnp.int32, (c, c), 0)
    c1 = lax.broadcasted_iota(jnp.int32, (c, c), 1)
    ltri = (r1 >= c1).astype(F32)
    lo = jnp.concatenate([ltri, jnp.full((c, c), -1.0, F32)], axis=1)
    bd_cc = (lax.broadcasted_iota(jnp.int32, (wc, wc), 0) // c
             == lax.broadcasted_iota(jnp.int32, (wc, wc), 1) // c)
    bd_cd = (lax.broadcasted_iota(jnp.int32, (wc, wz), 0) // c
             == lax.broadcasted_iota(jnp.int32, (wc, wz), 1) // hd)

    def bdiag(x, mask):
        return jnp.where(mask, jnp.concatenate([x] * nh, axis=0), 0.0).astype(BF16)

    rows = [slice(cc * c, (cc + 1) * c) for cc in range(nc)]
    cgc = [jnp.dot(ltri, g_w[r], precision=HIGHEST, preferred_element_type=F32) for r in rows]
    diff = [jnp.dot(lo, jnp.concatenate([g_c[r], jnp.where(ri <= cj, g_c[r], 0.0)], axis=0),
                    precision=HIGHEST, preferred_element_type=F32) for r in rows]
    decay = [jnp.exp(jnp.where(incl, d, -jnp.inf)) for d in diff]
    eg = [jnp.exp(x) for x in cgc]
    g_last = [x[c - 1:c, :] for x in cgc]
    kk = [_mm_nt(jnp.concatenate([kb_all[r], q_all[r]], axis=0), bdiag(k_all[r], bd_cd)) for r in rows]
    attn = [kk[n][c:2 * c] * decay[n] for n in range(nc)]
    rj = [-jnp.where(strict, kk[n][0:c] * decay[n], 0.0) for n in range(nc)]
    sj = [eye + rj[n] for n in range(nc)]
    rj = [_mm(rj[n], bdiag(rj[n], bd_cc)) for n in range(nc)]
    for _ in range(4):
        rs = [_mm(jnp.concatenate([rj[n], sj[n]], axis=0), bdiag(rj[n], bd_cc)) for n in range(nc)]
        rj = [x[0:c] for x in rs]
        sj = [sj[n] + rs[n][c:2 * c] for n in range(nc)]
    sj = [sj[n] + _mm(sj[n], bdiag(rj[n], bd_cc)) for n in range(nc)]
    w = [_mm(sj[n], jnp.concatenate([bdiag(vb_all[rows[n]], bd_cd),
                                     bdiag(kb_all[rows[n]] * eg[n], bd_cd)], axis=1)) for n in range(nc)]
    q_dec = [q_all[rows[n]] * eg[n] for n in range(nc)]
    k_dec = [k_all[rows[n]] * jnp.exp(g_last[n] - cgc[n]) for n in range(nc)]
    s_dec = [jnp.exp(x) for x in g_last]

    s_cur = [s_s[h] for h in range(nh)]
    hs = [slice(h * hd, (h + 1) * hd) for h in range(nh)]
    for n in range(nc):
        ws = [_mm(jnp.concatenate([w[n][:, wz + h * hd:wz + (h + 1) * hd], q_dec[n][:, hs[h]]], axis=0),
                  s_cur[h]) for h in range(nh)]
        v_new = jnp.concatenate([w[n][:, hs[h]] - ws[h][0:c] for h in range(nh)], axis=-1)
        o = jnp.concatenate([x[c:2 * c] for x in ws], axis=-1) + _mm(attn[n], bdiag(v_new, bd_cd))
        s_cur = [s_cur[h] * s_dec[n][:, hs[h]] + _mm_tn(k_dec[n][:, hs[h]], v_new[:, hs[h]])
                 for h in range(nh)]
        for h in range(nh):
            z = gd_ref[rows[n], w3 + h * hd:w3 + (h + 1) * hd]
            y_ref[rows[n], hs[h]] = _rms(o[:, hs[h]], nw_ref[...]) * (z * _sigmoid(z))
    for h in range(nh):
        s_s[h] = s_cur[h]

    @pl.when(i == last)
    def _fin():
        for h in range(nh):
            s_out[h] = s_cur[h]
        cv_out[...] = xp_s[rg + 5:rg + 8, :]


def _gdn_prompt(gd3, p, *, rg):
    nb, length, n_gd = gd3.shape
    nh = p["gdn_nh"]
    hd = p["gdn_nw"].shape[1]
    w3 = 3 * nh * hd
    return pl.pallas_call(
        functools.partial(_gdn_prompt_kernel, rg=rg, nh=nh, hd=hd),
        grid=(nb, length // rg),
        in_specs=[pl.BlockSpec((None, rg, n_gd), lambda b, i: (b, i, 0)),
                  pl.BlockSpec((4, w3), lambda b, i: (0, 0)),
                  pl.BlockSpec((1, LANES), lambda b, i: (0, 0)),
                  pl.BlockSpec((1, LANES), lambda b, i: (0, 0)),
                  pl.BlockSpec((1, hd), lambda b, i: (0, 0))],
        out_specs=[pl.BlockSpec((None, rg, nh * hd), lambda b, i: (b, i, 0)),
                   pl.BlockSpec((None, nh, hd, hd), lambda b, i: (b, 0, 0, 0)),
                   pl.BlockSpec((None, 3, w3), lambda b, i: (b, 0, 0))],
        out_shape=[jax.ShapeDtypeStruct((nb, length, nh * hd), F32),
                   jax.ShapeDtypeStruct((nb, nh, hd, hd), F32),
                   jax.ShapeDtypeStruct((nb, 3, w3), F32)],
        scratch_shapes=[pltpu.VMEM((nh, hd, hd), F32), pltpu.VMEM((rg + SUBLANES, w3), F32)],
        compiler_params=pltpu.CompilerParams(dimension_semantics=("arbitrary", "arbitrary"),
                                             vmem_limit_bytes=VMEM_LIMIT),
        name="gdn_prompt",
    )(gd3, p["gdn_cw"], p["gdn_alog"], p["gdn_dtb"], p["gdn_nw"])


def _gdn_sample_kernel(gd_ref, cv0_ref, s0_ref, cw_ref, alog_ref, dtb_ref, nw_ref,
                       y_ref, s_out, cv_out, kq_s, v_s, gb_s, o_s, *, nt, nbk, nh, hd):
    w3 = 3 * nh * hd
    wz = nh * hd
    cw = cw_ref[...]
    scale = hd ** -0.5
    xs = [cv0_ref[j] for j in range(3)] + [gd_ref[t, :, 0:w3] for t in range(nt)]
    ks, qs, vs, gs, bs = [], [], [], [], []
    for t in range(nt):
        xc = xs[t] * cw[0:1] + xs[t + 1] * cw[1:2] + xs[t + 2] * cw[2:3] + xs[t + 3] * cw[3:4]
        xc = xc * _sigmoid(xc)
        ab = gd_ref[t, :, w3 + wz:w3 + wz + LANES]
        eg_all = jnp.exp(-jnp.exp(alog_ref[...]) * _softplus(ab + dtb_ref[...]))
        beta_all = _sigmoid(ab)
        qh, kh, gh, bh = [], [], [], []
        for h in range(nh):
            q = xc[:, h * hd:(h + 1) * hd]
            k = xc[:, wz + h * hd:wz + (h + 1) * hd]
            qh.append(q * (lax.rsqrt(jnp.sum(q * q, axis=-1, keepdims=True) + NORM_EPS) * scale))
            kh.append(k * lax.rsqrt(jnp.sum(k * k, axis=-1, keepdims=True) + NORM_EPS))
            gh.append(jnp.broadcast_to(eg_all[:, h:h + 1], (nbk, hd)))
            bh.append(jnp.broadcast_to(beta_all[:, nh + h:nh + h + 1], (nbk, hd)))
        qs.append(jnp.concatenate(qh, axis=-1))
        ks.append(jnp.concatenate(kh, axis=-1))
        gs.append(jnp.concatenate(gh, axis=-1))
        bs.append(jnp.concatenate(bh, axis=-1))
        vs.append(xc[:, 2 * wz:3 * wz])
    for j in range(3):
        cv_out[j] = xs[nt + j]
    to_seq = lambda a, b: jnp.swapaxes(jnp.stack(a + b, axis=0), 0, 1)
    kq_s[...] = to_seq(ks, qs)
    v_s[...] = to_seq(vs, vs)
    gb_s[...] = to_seq(gs, bs)

    def per_seq(b, _):
        kq = kq_s[b]
        vv = v_s[b]
        gb = gb_s[b]
        for h in range(nh):
            cols = slice(h * hd, (h + 1) * hd)
            kq_t = kq[:, cols].T
            s = s0_ref[b, h]
            outs = []
            for t in range(nt):
                kcol = kq_t[:, t:t + 1]
                qcol = kq_t[:, nt + t:nt + t + 1]
                s = s * gb[t:t + 1, cols]
                u = jnp.sum(s * kcol, axis=0, keepdims=True)
                r = gb[nt + t:nt + t + 1, cols] * (vv[t:t + 1, cols] - u)
                s = s + kcol * r
                outs.append(jnp.sum(s * qcol, axis=0, keepdims=True))
            s_out[b, h] = s
            o_s[b, :, cols] = jnp.concatenate(outs + outs, axis=0)
        return 0

    lax.fori_loop(0, nbk, per_seq, 0)

    o_tm = jnp.swapaxes(o_s[...], 0, 1)
    for t in range(nt):
        for h in range(nh):
            cols = slice(h * hd, (h + 1) * hd)
            z = gd_ref[t, :, w3 + h * hd:w3 + (h + 1) * hd]
            y_ref[t, :, cols] = _rms(o_tm[t][:, cols], nw_ref[...]) * (z * _sigmoid(z))


def _gdn_sample(gd3, cv0, s0, p, *, nbk):
    nt, nb, n_gd = gd3.shape
    nh = p["gdn_nh"]
    hd = p["gdn_nw"].shape[1]
    w3 = 3 * nh * hd
    wz = nh * hd
    return pl.pallas_call(
        functools.partial(_gdn_sample_kernel, nt=nt, nbk=nbk, nh=nh, hd=hd),
        grid=(nb // nbk,),
        in_specs=[pl.BlockSpec((nt, nbk, n_gd), lambda i: (0, i, 0)),
                  pl.BlockSpec((3, nbk, w3), lambda i: (0, i, 0)),
                  pl.BlockSpec((nbk, nh, hd, hd), lambda i: (i, 0, 0, 0)),
                  pl.BlockSpec((4, w3), lambda i: (0, 0)),
                  pl.BlockSpec((1, LANES), lambda i: (0, 0)),
                  pl.BlockSpec((1, LANES), lambda i: (0, 0)),
                  pl.BlockSpec((1, hd), lambda i: (0, 0))],
        out_specs=[pl.BlockSpec((nt, nbk, wz), lambda i: (0, i, 0)),
                   pl.BlockSpec((nbk, nh, hd, hd), lambda i: (i, 0, 0, 0)),
                   pl.BlockSpec((3, nbk, w3), lambda i: (0, i, 0))],
        out_shape=[jax.ShapeDtypeStruct((nt, nb, wz), F32),
                   jax.ShapeDtypeStruct((nb, nh, hd, hd), F32),
                   jax.ShapeDtypeStruct((3, nb, w3), F32)],
        scratch_shapes=[pltpu.VMEM((nbk, 2 * nt, wz), F32) for _ in range(4)],
        compiler_params=pltpu.CompilerParams(dimension_semantics=("arbitrary",),
                                             vmem_limit_bytes=VMEM_LIMIT),
        name="gdn_sample",
    )(gd3, cv0, s0, p["gdn_cw"], p["gdn_alog"], p["gdn_dtb"], p["gdn_nw"])


def _ffn_kernel(x_ref, ysl_ref, yg_ref, wo_ref, nmp_ref, nfp_ref, nfq_ref, wgu_ref, wd_ref, o_ref,
                *, from_time_major):
    nb, tc, d = x_ref.shape
    rows = nb * tc
    x = x_ref[...].reshape(rows, d)
    ysl = ysl_ref[...]
    if from_time_major:
        ysl = jnp.swapaxes(ysl, 0, 1)
    ysl = ysl.reshape(rows, ysl.shape[-1])
    yg = yg_ref[...].reshape(rows, yg_ref.shape[-1])
    mix = jnp.concatenate([ysl, yg], axis=-1)
    x = x + _rms(_mm(mix, wo_ref[...]), nmp_ref[...])
    h = _rms(x, nfp_ref[...])
    gu = _mm(h, wgu_ref[...])
    hid = gu.shape[1] // 2
    gt = gu[:, :hid]
    f = _mm(gt * _sigmoid(gt) * gu[:, hid:], wd_ref[...])
    x = x + _rms(f, nfq_ref[...])
    o_ref[...] = x.reshape(nb, tc, d)


def _ffn(x3, ysl3, yg3, p, *, tc, from_time_major):
    nb, length, d = x3.shape
    n1 = ysl3.shape[-1]
    n2 = yg3.shape[-1]
    if from_time_major:
        ysl_spec = pl.BlockSpec((tc, nb, n1), lambda i: (i, 0, 0))
    else:
        ysl_spec = pl.BlockSpec((nb, tc, n1), lambda i: (0, i, 0))
    return pl.pallas_call(
        functools.partial(_ffn_kernel, from_time_major=from_time_major),
        grid=(length // tc,),
        in_specs=[pl.BlockSpec((nb, tc, d), lambda i: (0, i, 0)), ysl_spec,
                  pl.BlockSpec((nb, tc, n2), lambda i: (0, i, 0)),
                  _const_spec(p["w_out"].shape), _const_spec((1, d)), _const_spec((1, d)),
                  _const_spec((1, d)), _const_spec(p["w_gu"].shape), _const_spec(p["w_down"].shape)],
        out_specs=pl.BlockSpec((nb, tc, d), lambda i: (0, i, 0)),
        out_shape=jax.ShapeDtypeStruct((nb, length, d), F32),
        compiler_params=pltpu.CompilerParams(dimension_semantics=("arbitrary",),
                                             vmem_limit_bytes=VMEM_LIMIT),
        name="ffn",
    )(x3, ysl3, yg3, p["w_out"], p["n_mix_post"], p["n_ffn_pre"], p["n_ffn_post"], p["w_gu"], p["w_down"])


def _block_diag(blocks):
    n, r, c = blocks.shape
    eye = jnp.eye(n, dtype=blocks.dtype)
    return (eye[:, None, :, None] * blocks[:, :, None, :]).reshape(n * r, n * c)


def _pad_lanes(v, offset):
    return jnp.zeros((1, LANES), F32).at[0, offset:offset + v.shape[0]].set(v)


def _layer_params(l, norm_mix_pre, norm_mix_post, norm_ffn_pre, norm_ffn_post, w_in, s5_lambda_re,
                  s5_lambda_im, s5_log_dt, s5_b_re, s5_b_im, s5_c_re, s5_c_im, s5_d, s5_w_glu, s5_b_glu,
                  lru_conv_w, lru_conv_b, lru_w_a, lru_b_a, lru_w_x, lru_b_x, lru_lambda, gdn_conv_w,
                  gdn_a_log, gdn_dt_bias, gdn_norm_w, w_out, ffn_w_gate_up, ffn_w_down):
    g, pst, hgrp = s5_b_re.shape[1:]
    s5w = g * hgrp
    lw = lru_lambda.shape[1]
    nh = gdn_a_log.shape[1]
    hd = gdn_norm_w.shape[1]
    gw = nh * hd
    row = lambda v: v[l][None, :]
    w = w_in[l]
    n_sl = s5w + 2 * lw
    c_qkv, c_a, c_b, c_z = n_sl, n_sl + 3 * gw, n_sl + 3 * gw + nh, n_sl + 3 * gw + 2 * nh
    w_r = jnp.concatenate([w[:, :c_a], w[:, c_z:c_z + gw], w[:, c_a:c_z],
                           jnp.zeros((w.shape[0], LANES - 2 * nh), w.dtype)], axis=1).astype(BF16)
    return {
        "n_sl": n_sl,
        "n_mix_pre": row(norm_mix_pre), "n_mix_post": row(norm_mix_post),
        "n_ffn_pre": row(norm_ffn_pre), "n_ffn_post": row(norm_ffn_post),
        "w_in": w_r,
        "s5_lre": s5_lambda_re[l].reshape(1, g * pst), "s5_lim": s5_lambda_im[l].reshape(1, g * pst),
        "s5_ldt": jnp.repeat(s5_log_dt[l], pst)[None, :],
        "s5_bre": _block_diag(jnp.swapaxes(s5_b_re[l], 1, 2)),
        "s5_bim": _block_diag(jnp.swapaxes(s5_b_im[l], 1, 2)),
        "s5_cre": _block_diag(jnp.swapaxes(s5_c_re[l], 1, 2)),
        "s5_cim": _block_diag(jnp.swapaxes(s5_c_im[l], 1, 2)),
        "s5_d": row(s5_d), "s5_wglu": s5_w_glu[l].astype(BF16), "s5_bglu": row(s5_b_glu),
        "lru_cw": lru_conv_w[l], "lru_cb": row(lru_conv_b),
        "lru_wa": _block_diag(lru_w_a[l]).astype(BF16), "lru_ba": row(lru_b_a),
        "lru_wx": _block_diag(lru_w_x[l]).astype(BF16), "lru_bx": row(lru_b_x),
        "lru_lambda": row(lru_lambda),
        "gdn_cw": gdn_conv_w[l], "gdn_alog": _pad_lanes(gdn_a_log[l], 0),
        "gdn_dtb": _pad_lanes(gdn_dt_bias[l], 0), "gdn_nw": row(gdn_norm_w), "gdn_nh": nh,
        "w_out": w_out[l].astype(BF16), "w_gu": ffn_w_gate_up[l].astype(BF16),
        "w_down": ffn_w_down[l].astype(BF16),
    }


def _prompt_layer(x, p, *, tc, rg):
    nb, length, _ = x.shape
    sl_tm, gd = _in_proj(x, p["n_mix_pre"], p["w_in"], n_sl=p["n_sl"], tc=tc, to_time_major=True)
    n_state = p["s5_lre"].shape[1]
    lw = p["lru_lambda"].shape[1]
    z = lambda *s: jnp.zeros(s, F32)
    ysl, xr, xi, hl, cv = _scan(sl_tm.reshape(length * nb, -1), p, z(nb, n_state), z(nb, n_state),
                                z(nb, lw), z(3 * nb, lw), nb=nb, tc=tc, first_pos_is_zero=True)
    yg, s_new, gcv = _gdn_prompt(gd, p, rg=rg)
    x = _ffn(x, ysl.reshape(length, nb, -1), yg, p, tc=tc, from_time_major=True)
    lcv = jnp.swapaxes(cv.reshape(3, nb, lw), 0, 1)
    return x, (xr, xi, hl, lcv, s_new, gcv)


def _sample_layer(x_tm, st, p, *, nbk):
    s5_re, s5_im, lru_h, lru_conv, gdn_s, gdn_conv = st
    nb = s5_re.shape[0]
    rows = x_tm.shape[1]
    nt = rows // nb
    sl, gd = _in_proj(x_tm, p["n_mix_pre"], p["w_in"], n_sl=p["n_sl"], tc=rows, to_time_major=False)
    lw = p["lru_lambda"].shape[1]
    cv0 = jnp.swapaxes(lru_conv, 0, 1).reshape(3 * nb, lw)
    ysl, xr, xi, hl, cv = _scan(sl.reshape(rows, -1), p, s5_re.reshape(nb, -1), s5_im.reshape(nb, -1),
                                lru_h, cv0, nb=nb, tc=nt, first_pos_is_zero=(PAST_LEN == 0))
    yg, s_new, gcv = _gdn_sample(gd.reshape(nt, nb, -1), jnp.swapaxes(gdn_conv, 0, 1), gdn_s, p, nbk=nbk)
    x_tm = _ffn(x_tm, ysl.reshape(1, rows, -1), yg.reshape(1, rows, -1), p, tc=rows, from_time_major=False)
    lcv = jnp.swapaxes(cv.reshape(3, nb, lw), 0, 1)
    return x_tm, (xr, xi, hl, lcv, s_new, jnp.swapaxes(gcv, 0, 1))


def kernel(x_prompt, x_sample, state_s5_re, state_s5_im, state_lru_h, state_lru_conv, state_gdn_S, state_gdn_conv, norm_mix_pre, norm_mix_post, norm_ffn_pre, norm_ffn_post, w_in, s5_lambda_re, s5_lambda_im, s5_log_dt, s5_b_re, s5_b_im, s5_c_re, s5_c_im, s5_d, s5_w_glu, s5_b_glu, lru_conv_w, lru_conv_b, lru_w_a, lru_b_a, lru_w_x, lru_b_x, lru_lambda, gdn_conv_w, gdn_a_log, gdn_dt_bias, gdn_norm_w, w_out, ffn_w_gate_up, ffn_w_down):
    depth = w_in.shape[0]
    bp, lp, d = x_prompt.shape
    bs, ls, _ = x_sample.shape
    g, pst = s5_lambda_re.shape[1:]
    tc = min(64, lp)
    rg = min(4 * GDN_CHUNK, lp)
    y_p = x_prompt
    y_s = jnp.swapaxes(x_sample, 0, 1).reshape(1, ls * bs, d)
    new_p = [[] for _ in range(6)]
    new_s = [[] for _ in range(6)]
    for l in range(depth):
        p = _layer_params(l, norm_mix_pre, norm_mix_post, norm_ffn_pre, norm_ffn_post, w_in, s5_lambda_re,
                          s5_lambda_im, s5_log_dt, s5_b_re, s5_b_im, s5_c_re, s5_c_im, s5_d, s5_w_glu,
                          s5_b_glu, lru_conv_w, lru_conv_b, lru_w_a, lru_b_a, lru_w_x, lru_b_x, lru_lambda,
                          gdn_conv_w, gdn_a_log, gdn_dt_bias, gdn_norm_w, w_out, ffn_w_gate_up, ffn_w_down)
        y_p, sp = _prompt_layer(y_p, p, tc=tc, rg=rg)
        st_s = (state_s5_re[l], state_s5_im[l], state_lru_h[l], state_lru_conv[l],
                state_gdn_S[l], state_gdn_conv[l])
        y_s, ss = _sample_layer(y_s, st_s, p, nbk=min(8, bs))
        for j in range(6):
            new_p[j].append(sp[j])
            new_s[j].append(ss[j])
    y_s = jnp.swapaxes(y_s.reshape(ls, bs, d), 0, 1)
    outs_p = [jnp.stack(t, axis=0) for t in new_p]
    outs_s = [jnp.stack(t, axis=0) for t in new_s]
    outs_p[0] = outs_p[0].reshape(depth, bp, g, pst)
    outs_p[1] = outs_p[1].reshape(depth, bp, g, pst)
    outs_s[0] = outs_s[0].reshape(depth, bs, g, pst)
    outs_s[1] = outs_s[1].reshape(depth, bs, g, pst)
    return (y_p, y_s, *outs_p, *outs_s)
```

```python
import functools

import jax
import jax.numpy as jnp
from jax import lax
from jax.experimental import pallas as pl
from jax.experimental.pallas import tpu as pltpu

F32 = jnp.float32
BF16 = jnp.bfloat16
HIGHEST = lax.Precision.HIGHEST

NORM_EPS = 1e-6
LRU_C = 8.0
GDN_CHUNK = 64
PAST_LEN = 16384
LANES = 128
SUBLANES = 8
VMEM_LIMIT = 56 * 1024 * 1024


def _rms(x, w):
    return x * lax.rsqrt(jnp.mean(x * x, axis=-1, keepdims=True) + NORM_EPS) * w


def _sigmoid(x):
    return jax.nn.sigmoid(x)


def _softplus(x):
    return jnp.maximum(x, 0.0) + jnp.log1p(jnp.exp(-jnp.abs(x)))


def _mm(a, b):
    return jnp.dot(a.astype(BF16), b.astype(BF16), preferred_element_type=F32)


def _mm_nt(a, b):
    return lax.dot_general(a.astype(BF16), b.astype(BF16), (((1,), (1,)), ((), ())),
                           preferred_element_type=F32)


def _mm_tn(a, b):
    return lax.dot_general(a.astype(BF16), b.astype(BF16), (((0,), (0,)), ((), ())),
                           preferred_element_type=F32)


def _const_spec(shape):
    n = len(shape)
    return pl.BlockSpec(shape, lambda *_: (0,) * n, pipeline_mode=pl.Buffered(1))


def _in_proj_kernel(x_ref, nw_ref, w_ref, sl_ref, gd_ref, *, n_sl, to_time_major):
    nb, tc, d = x_ref.shape
    x = x_ref[...].reshape(nb * tc, d)
    h = _rms(x, nw_ref[...])
    p = _mm(h, w_ref[...])
    sl = p[:, :n_sl].reshape(nb, tc, n_sl)
    if to_time_major:
        sl = jnp.swapaxes(sl, 0, 1)
    sl_ref[...] = sl
    gd_ref[...] = p[:, n_sl:].reshape(nb, tc, p.shape[1] - n_sl)


def _in_proj(x3, nw, w, *, n_sl, tc, to_time_major):
    nb, length, d = x3.shape
    n_tot = w.shape[1]
    n_gd = n_tot - n_sl
    if to_time_major:
        sl_shape, sl_spec = (length, nb, n_sl), pl.BlockSpec((tc, nb, n_sl), lambda i: (i, 0, 0))
    else:
        sl_shape, sl_spec = (nb, length, n_sl), pl.BlockSpec((nb, tc, n_sl), lambda i: (0, i, 0))
    return pl.pallas_call(
        functools.partial(_in_proj_kernel, n_sl=n_sl, to_time_major=to_time_major),
        grid=(length // tc,),
        in_specs=[pl.BlockSpec((nb, tc, d), lambda i: (0, i, 0)),
                  _const_spec((1, d)), _const_spec((d, n_tot))],
        out_specs=[sl_spec, pl.BlockSpec((nb, tc, n_gd), lambda i: (0, i, 0))],
        out_shape=[jax.ShapeDtypeStruct(sl_shape, F32), jax.ShapeDtypeStruct((nb, length, n_gd), F32)],
        compiler_params=pltpu.CompilerParams(dimension_semantics=("arbitrary",),
                                             vmem_limit_bytes=VMEM_LIMIT),
        name="in_proj",
    )(x3, nw, w)


def _scan_kernel(sl_ref, lre_ref, lim_ref, ldt_ref, bre_ref, bim_ref, cre_ref, cim_ref, d_ref,
                 wglu_ref, bglu_ref, cw_ref, cb_ref, wa_ref, ba_ref, wx_ref, bx_ref, lam_ref,
                 xr0_ref, xi0_ref, h0_ref, cv0_ref,
                 y_ref, xr_out, xi_out, h_out, cv_out,
                 ar_s, ai_s, bbr_s, bbi_s, crb_s, cib_s, xr_c, xi_c, h_c, xr_s, xi_s, xpad_s, a_s, b_s,
                 *, nb, tc, s5w, lw, first_pos_is_zero):
    i = pl.program_id(0)
    last = pl.num_programs(0) - 1
    rows = nb * tc

    @pl.when(i == 0)
    def _init():
        lr = jnp.minimum(lre_ref[...], -1e-4)
        li = lim_ref[...]
        dt = jnp.exp(ldt_ref[...])
        mag = jnp.exp(lr * dt)
        ar = mag * jnp.cos(li * dt)
        ai = mag * jnp.sin(li * dt)
        den = lr * lr + li * li
        fr = ((ar - 1.0) * lr + ai * li) / den
        fi = (ai * lr - (ar - 1.0) * li) / den
        ar_s[...] = ar
        ai_s[...] = ai
        bre = bre_ref[...]
        bim = bim_ref[...]
        bbr_s[...] = (fr * bre - fi * bim).astype(BF16)
        bbi_s[...] = (fr * bim + fi * bre).astype(BF16)
        crb_s[...] = cre_ref[...].astype(BF16)
        cib_s[...] = cim_ref[...].astype(BF16)
        xr_c[...] = xr0_ref[...]
        xi_c[...] = xi0_ref[...]
        h_c[...] = h0_ref[...]
        xpad_s[0:3 * nb, :] = cv0_ref[...]

    u = sl_ref[:, 0:s5w]
    xb = sl_ref[:, s5w:s5w + lw]
    gate = sl_ref[:, s5w + lw:s5w + 2 * lw]

    ub = u.astype(BF16)
    xr_s[...] = jnp.dot(ub, bbr_s[...], preferred_element_type=F32)
    xi_s[...] = jnp.dot(ub, bbi_s[...], preferred_element_type=F32)

    xpad_s[3 * nb:3 * nb + rows, :] = xb
    cw = cw_ref[...]
    xc = (xpad_s[0:rows, :] * cw[0:1] + xpad_s[nb:nb + rows, :] * cw[1:2]
          + xpad_s[2 * nb:2 * nb + rows, :] * cw[2:3] + xb * cw[3:4]) + cb_ref[...]
    new_cv = xpad_s[rows:rows + 3 * nb, :]
    xpad_s[0:3 * nb, :] = new_cv
    r = _sigmoid(_mm(xc, wa_ref[...]) + ba_ref[...])
    gi = _sigmoid(_mm(xc, wx_ref[...]) + bx_ref[...])
    log_a = -LRU_C * r * _softplus(-lam_ref[...])
    a = jnp.exp(log_a)
    mult = jnp.sqrt(-jnp.tanh(log_a) * (a * a + 1.0))
    if first_pos_is_zero:
        rid = lax.broadcasted_iota(jnp.int32, (rows, 1), 0)
        mult = jnp.where(jnp.logical_and(rid < nb, i == 0), 1.0, mult)
    a_s[...] = a
    b_s[...] = mult * gi * xc

    ar = jnp.broadcast_to(ar_s[...], (nb, ar_s.shape[1]))
    ai = jnp.broadcast_to(ai_s[...], (nb, ai_s.shape[1]))

    def step(t, carry):
        xr, xi, h = carry
        sl_t = pl.ds(pl.multiple_of(t * nb, nb), nb)
        nxr = ar * xr - ai * xi + xr_s[sl_t, :]
        nxi = ar * xi + ai * xr + xi_s[sl_t, :]
        nh = a_s[sl_t, :] * h + b_s[sl_t, :]
        xr_s[sl_t, :] = nxr
        xi_s[sl_t, :] = nxi
        b_s[sl_t, :] = nh
        return nxr, nxi, nh

    xr, xi, h = lax.fori_loop(0, tc, step, (xr_c[...], xi_c[...], h_c[...]),
                              unroll=(True if tc <= 8 else 8))
    xr_c[...] = xr
    xi_c[...] = xi
    h_c[...] = h

    y = (jnp.dot(xr_s[...].astype(BF16), crb_s[...], preferred_element_type=F32)
         - jnp.dot(xi_s[...].astype(BF16), cib_s[...], preferred_element_type=F32))
    y = jax.nn.gelu(y + d_ref[...] * u)
    y = y * _sigmoid(_mm(y, wglu_ref[...]) + bglu_ref[...])
    y_ref[:, 0:s5w] = y
    y_ref[:, s5w:s5w + lw] = b_s[...] * jax.nn.gelu(gate)

    @pl.when(i == last)
    def _fin():
        xr_out[...] = xr
        xi_out[...] = xi
        h_out[...] = h
        cv_out[...] = new_cv


def _scan(sl2, p, xr0, xi0, h0, cv0, *, nb, tc, first_pos_is_zero):
    rows_total, n_sl = sl2.shape
    s5w = p["s5_d"].shape[1]
    lw = p["lru_lambda"].shape[1]
    n_state = p["s5_lre"].shape[1]
    rows = nb * tc
    params = [p["s5_lre"], p["s5_lim"], p["s5_ldt"], p["s5_bre"], p["s5_bim"], p["s5_cre"], p["s5_cim"],
              p["s5_d"], p["s5_wglu"], p["s5_bglu"], p["lru_cw"], p["lru_cb"], p["lru_wa"], p["lru_ba"],
              p["lru_wx"], p["lru_bx"], p["lru_lambda"]]
    states = [xr0, xi0, h0, cv0]
    return pl.pallas_call(
        functools.partial(_scan_kernel, nb=nb, tc=tc, s5w=s5w, lw=lw, first_pos_is_zero=first_pos_is_zero),
        grid=(rows_total // rows,),
        in_specs=[pl.BlockSpec((rows, n_sl), lambda i: (i, 0))]
        + [_const_spec(a.shape) for a in params + states],
        out_specs=[pl.BlockSpec((rows, s5w + lw), lambda i: (i, 0)),
                   pl.BlockSpec((nb, n_state), lambda i: (0, 0)),
                   pl.BlockSpec((nb, n_state), lambda i: (0, 0)),
                   pl.BlockSpec((nb, lw), lambda i: (0, 0)),
                   pl.BlockSpec((3 * nb, lw), lambda i: (0, 0))],
        out_shape=[jax.ShapeDtypeStruct((rows_total, s5w + lw), F32),
                   jax.ShapeDtypeStruct((nb, n_state), F32),
                   jax.ShapeDtypeStruct((nb, n_state), F32),
                   jax.ShapeDtypeStruct((nb, lw), F32),
                   jax.ShapeDtypeStruct((3 * nb, lw), F32)],
        scratch_shapes=[pltpu.VMEM((1, n_state), F32), pltpu.VMEM((1, n_state), F32),
                        pltpu.VMEM((s5w, n_state), BF16), pltpu.VMEM((s5w, n_state), BF16),
                        pltpu.VMEM((n_state, s5w), BF16), pltpu.VMEM((n_state, s5w), BF16),
                        pltpu.VMEM((nb, n_state), F32), pltpu.VMEM((nb, n_state), F32),
                        pltpu.VMEM((nb, lw), F32),
                        pltpu.VMEM((rows, n_state), F32), pltpu.VMEM((rows, n_state), F32),
                        pltpu.VMEM((rows + 3 * nb, lw), F32),
                        pltpu.VMEM((rows, lw), F32), pltpu.VMEM((rows, lw), F32)],
        compiler_params=pltpu.CompilerParams(dimension_semantics=("arbitrary",),
                                             vmem_limit_bytes=VMEM_LIMIT),
        name="scan",
    )(sl2, *params, *states)


def _gdn_prompt_kernel(gd_ref, cw_ref, alog_ref, dtb_ref, nw_ref, y_ref, s_out, cv_out,
                       s_s, xp_s, *, ns, rg, nh, hd):
    i = pl.program_id(1)
    last = pl.num_programs(1) - 1
    w3 = 3 * nh * hd
    wz = nh * hd
    c = GDN_CHUNK
    nc = rg // c
    wc = nh * c
    rt = ns * rg

    @pl.when(i == 0)
    def _init():
        s_s[...] = jnp.zeros_like(s_s)
        xp_s[...] = jnp.zeros_like(xp_s)

    cw = cw_ref[...]
    xcs = []
    for s in range(ns):
        qkv = gd_ref[s, :, 0:w3]
        xe = jnp.concatenate([xp_s[s], qkv], axis=0)
        acc = qkv * cw[3:4]
        for j in range(1, 4):
            acc = acc + pltpu.roll(xe, j, 0)[SUBLANES:, :] * cw[3 - j:4 - j]
        xp_s[s] = qkv[rg - SUBLANES:, :]
        xcs.append(acc)
    xc = jnp.concatenate(xcs, axis=0)
    xc = xc * _sigmoid(xc)

    ab = jnp.concatenate([gd_ref[s, :, w3 + wz:w3 + wz + LANES] for s in range(ns)], axis=0)
    g_all = -jnp.exp(alog_ref[...]) * _softplus(ab + dtb_ref[...])
    beta_all = _sigmoid(ab)
    scale = hd ** -0.5

    def per_head(fn):
        return jnp.concatenate([fn(h) for h in range(nh)], axis=-1)

    def l2n(x, mul):
        def one(h):
            xh = x[:, h * hd:(h + 1) * hd]
            return xh * (lax.rsqrt(jnp.sum(xh * xh, axis=-1, keepdims=True) + NORM_EPS) * mul)
        return per_head(one)

    q_all = l2n(xc[:, 0:wz], scale)
    k_all = l2n(xc[:, wz:2 * wz], 1.0)
    v_all = xc[:, 2 * wz:3 * wz]
    beta_w = per_head(lambda h: jnp.broadcast_to(beta_all[:, nh + h:nh + h + 1], (rt, hd)))
    kb_all = k_all * beta_w
    vb_all = v_all * beta_w

    ri = lax.broadcasted_iota(jnp.int32, (c, wc), 0)
    cj = lax.broadcasted_iota(jnp.int32, (c, wc), 1) % c
    incl = ri >= cj
    strict = ri > cj
    eye = (ri == cj).astype(F32)
    r1 = lax.broadcasted_iota(jnp.int32, (c, c), 0)
    c1 = lax.broadcasted_iota(jnp.int32, (c, c), 1)
    ltri = (r1 >= c1).astype(F32)
    bd_cc = (lax.broadcasted_iota(jnp.int32, (wc, wc), 0) // c
             == lax.broadcasted_iota(jnp.int32, (wc, wc), 1) // c)
    bd_cd = (lax.broadcasted_iota(jnp.int32, (wc, wz), 0) // c
             == lax.broadcasted_iota(jnp.int32, (wc, wz), 1) // hd)

    def bdiag(x, mask):
        return jnp.where(mask, jnp.concatenate([x] * nh, axis=0), 0.0).astype(BF16)

    nc_all = ns * nc
    rows = [slice(cc * c, (cc + 1) * c) for cc in range(nc_all)]
    gcum = [jnp.dot(ltri, g_all[r], precision=HIGHEST, preferred_element_type=F32) for r in rows]
    cgc = [per_head(lambda h, x=x: jnp.broadcast_to(x[:, h:h + 1], (c, hd))) for x in gcum]
    gcol = [per_head(lambda h, x=x: jnp.broadcast_to(x[:, h:h + 1], (c, c))) for x in gcum]
    grow = [jnp.concatenate([xt[h:h + 1, :] for h in range(nh)], axis=1)
            for xt in [x.T for x in gcum]]
    every = range(nc_all)
    decay = [jnp.exp(jnp.where(incl, gcol[n] - grow[n], -jnp.inf)) for n in every]
    eg = [jnp.exp(x) for x in cgc]
    g_last = [x[c - 1:c, :] for x in cgc]
    kk = [_mm_nt(jnp.concatenate([kb_all[r], q_all[r]], axis=0), bdiag(k_all[r], bd_cd)) for r in rows]
    attn = [kk[n][c:2 * c] * decay[n] for n in every]
    rj = [-jnp.where(strict, kk[n][0:c] * decay[n], 0.0) for n in every]
    sj = [eye + rj[n] for n in every]
    rj = [_mm(rj[n], bdiag(rj[n], bd_cc)) for n in every]
    for _ in range(4):
        rs = [_mm(jnp.concatenate([rj[n], sj[n]], axis=0), bdiag(rj[n], bd_cc)) for n in every]
        rj = [x[0:c] for x in rs]
        sj = [sj[n] + rs[n][c:2 * c] for n in every]
    sj = [sj[n] + _mm(sj[n], bdiag(rj[n], bd_cc)) for n in every]
    w = [_mm(sj[n], jnp.concatenate([bdiag(vb_all[rows[n]], bd_cd),
                                     bdiag(kb_all[rows[n]] * eg[n], bd_cd)], axis=1)) for n in every]
    q_dec = [q_all[rows[n]] * eg[n] for n in every]
    k_dec = [k_all[rows[n]] * jnp.exp(g_last[n] - cgc[n]) for n in every]
    s_dec = [jnp.exp(x) for x in g_last]

    s_cur = [[s_s[s, h] for h in range(nh)] for s in range(ns)]
    hs = [slice(h * hd, (h + 1) * hd) for h in range(nh)]
    for m in range(nc):
        for s in range(ns):
            n = s * nc + m
            st = s_cur[s]
            ws = [_mm(jnp.concatenate([w[n][:, wz + h * hd:wz + (h + 1) * hd], q_dec[n][:, hs[h]]], axis=0),
                      st[h]) for h in range(nh)]
            v_new = jnp.concatenate([w[n][:, hs[h]] - ws[h][0:c] for h in range(nh)], axis=-1)
            o = jnp.concatenate([x[c:2 * c] for x in ws], axis=-1) + _mm(attn[n], bdiag(v_new, bd_cd))
            s_cur[s] = [st[h] * s_dec[n][:, hs[h]] + _mm_tn(k_dec[n][:, hs[h]], v_new[:, hs[h]])
                        for h in range(nh)]
            for h in range(nh):
                z = gd_ref[s, m * c:(m + 1) * c, w3 + h * hd:w3 + (h + 1) * hd]
                y_ref[s, m * c:(m + 1) * c, hs[h]] = _rms(o[:, hs[h]], nw_ref[...]) * (z * _sigmoid(z))
    for s in range(ns):
        for h in range(nh):
            s_s[s, h] = s_cur[s][h]

    @pl.when(i == last)
    def _fin():
        for s in range(ns):
            for h in range(nh):
                s_out[s, h] = s_cur[s][h]
            cv_out[s] = xp_s[s, SUBLANES - 3:SUBLANES, :]


def _gdn_prompt(gd3, p, *, rg, ns):
    nb, length, n_gd = gd3.shape
    nh = p["gdn_nh"]
    hd = p["gdn_nw"].shape[1]
    w3 = 3 * nh * hd
    return pl.pallas_call(
        functools.partial(_gdn_prompt_kernel, ns=ns, rg=rg, nh=nh, hd=hd),
        grid=(nb // ns, length // rg),
        in_specs=[pl.BlockSpec((ns, rg, n_gd), lambda b, i: (b, i, 0)),
                  pl.BlockSpec((4, w3), lambda b, i: (0, 0)),
                  pl.BlockSpec((1, LANES), lambda b, i: (0, 0)),
                  pl.BlockSpec((1, LANES), lambda b, i: (0, 0)),
                  pl.BlockSpec((1, hd), lambda b, i: (0, 0))],
        out_specs=[pl.BlockSpec((ns, rg, nh * hd), lambda b, i: (b, i, 0)),
                   pl.BlockSpec((ns, nh, hd, hd), lambda b, i: (b, 0, 0, 0)),
                   pl.BlockSpec((ns, 3, w3), lambda b, i: (b, 0, 0))],
        out_shape=[jax.ShapeDtypeStruct((nb, length, nh * hd), F32),
                   jax.ShapeDtypeStruct((nb, nh, hd, hd), F32),
                   jax.ShapeDtypeStruct((nb, 3, w3), F32)],
        scratch_shapes=[pltpu.VMEM((ns, nh, hd, hd), F32), pltpu.VMEM((ns, SUBLANES, w3), F32)],
        compiler_params=pltpu.CompilerParams(dimension_semantics=("arbitrary", "arbitrary"),
                                             vmem_limit_bytes=VMEM_LIMIT),
        name="gdn_prompt",
    )(gd3, p["gdn_cw"], p["gdn_alog"], p["gdn_dtb"], p["gdn_nw"])


def _gdn_sample_kernel(*refs, nt, nbk, nh, hd, layer, has_acc):
    gd_ref, cv0_ref, s0_ref, cw_ref, alog_ref, dtb_ref, nw_ref = refs[:7]
    y_ref, s_out, cv_out, kq_s, v_s, gb_s, o_s = refs[7 + int(has_acc):]
    if not has_acc:
        for other in range(s_out.shape[0]):
            if other != layer:
                s_out[other] = jnp.zeros(s_out.shape[1:], F32)
        s_out = s_out.at[layer]
    w3 = 3 * nh * hd
    wz = nh * hd
    cw = cw_ref[...]
    scale = hd ** -0.5
    xs = [cv0_ref[j] for j in range(3)] + [gd_ref[t, :, 0:w3] for t in range(nt)]
    ks, qs, vs, gs, bs = [], [], [], [], []
    for t in range(nt):
        xc = xs[t] * cw[0:1] + xs[t + 1] * cw[1:2] + xs[t + 2] * cw[2:3] + xs[t + 3] * cw[3:4]
        xc = xc * _sigmoid(xc)
        ab = gd_ref[t, :, w3 + wz:w3 + wz + LANES]
        eg_all = jnp.exp(-jnp.exp(alog_ref[...]) * _softplus(ab + dtb_ref[...]))
        beta_all = _sigmoid(ab)
        qh, kh, gh, bh = [], [], [], []
        for h in range(nh):
            q = xc[:, h * hd:(h + 1) * hd]
            k = xc[:, wz + h * hd:wz + (h + 1) * hd]
            qh.append(q * (lax.rsqrt(jnp.sum(q * q, axis=-1, keepdims=True) + NORM_EPS) * scale))
            kh.append(k * lax.rsqrt(jnp.sum(k * k, axis=-1, keepdims=True) + NORM_EPS))
            gh.append(jnp.broadcast_to(eg_all[:, h:h + 1], (nbk, hd)))
            bh.append(jnp.broadcast_to(beta_all[:, nh + h:nh + h + 1], (nbk, hd)))
        qs.append(jnp.concatenate(qh, axis=-1))
        ks.append(jnp.concatenate(kh, axis=-1))
        gs.append(jnp.concatenate(gh, axis=-1))
        bs.append(jnp.concatenate(bh, axis=-1))
        vs.append(xc[:, 2 * wz:3 * wz])
    for j in range(3):
        cv_out[j] = xs[nt + j]
    to_seq = lambda a, b: jnp.swapaxes(jnp.stack(a + b, axis=0), 0, 1)
    kq_s[...] = to_seq(ks, qs)
    v_s[...] = to_seq(vs, vs)
    gb_s[...] = to_seq(gs, bs)

    def per_seq(b, _):
        kq = kq_s[b]
        vv = v_s[b]
        gb = gb_s[b]
        for h in range(nh):
            cols = slice(h * hd, (h + 1) * hd)
            kq_t = kq[:, cols].T
            s = s0_ref[b, h]
            outs = []
            for t in range(nt):
                kcol = kq_t[:, t:t + 1]
                qcol = kq_t[:, nt + t:nt + t + 1]
                s = s * gb[t:t + 1, cols]
                u = jnp.sum(s * kcol, axis=0, keepdims=True)
                r = gb[nt + t:nt + t + 1, cols] * (vv[t:t + 1, cols] - u)
                s = s + kcol * r
                outs.append(jnp.sum(s * qcol, axis=0, keepdims=True))
            s_out[b, h] = s
            o_s[b, :, cols] = jnp.concatenate(outs + outs, axis=0)
        return 0

    lax.fori_loop(0, nbk, per_seq, 0)

    o_tm = jnp.swapaxes(o_s[...], 0, 1)
    for t in range(nt):
        for h in range(nh):
            cols = slice(h * hd, (h + 1) * hd)
            z = gd_ref[t, :, w3 + h * hd:w3 + (h + 1) * hd]
            y_ref[t, :, cols] = _rms(o_tm[t][:, cols], nw_ref[...]) * (z * _sigmoid(z))


def _gdn_sample(gd3, cv0, s_all, s_acc, layer, p, *, nbk):
    nt, nb, n_gd = gd3.shape
    nh = p["gdn_nh"]
    hd = p["gdn_nw"].shape[1]
    w3 = 3 * nh * hd
    wz = nh * hd
    s_spec = pl.BlockSpec((None, nbk, nh, hd, hd), lambda i: (layer, i, 0, 0, 0))
    args = [gd3, cv0, s_all, p["gdn_cw"], p["gdn_alog"], p["gdn_dtb"], p["gdn_nw"]]
    in_specs = [pl.BlockSpec((nt, nbk, n_gd), lambda i: (0, i, 0)),
                pl.BlockSpec((3, nbk, w3), lambda i: (0, i, 0)),
                s_spec,
                pl.BlockSpec((4, w3), lambda i: (0, 0)),
                pl.BlockSpec((1, LANES), lambda i: (0, 0)),
                pl.BlockSpec((1, LANES), lambda i: (0, 0)),
                pl.BlockSpec((1, hd), lambda i: (0, 0))]
    aliases = {}
    s_out_spec = s_spec
    if s_acc is not None:
        aliases = {len(args): 1}
        args.append(s_acc)
        in_specs.append(pl.BlockSpec(memory_space=pl.ANY))
    else:
        s_out_spec = pl.BlockSpec((s_all.shape[0], nbk, nh, hd, hd), lambda i: (0, i, 0, 0, 0))
    return pl.pallas_call(
        functools.partial(_gdn_sample_kernel, nt=nt, nbk=nbk, nh=nh, hd=hd, layer=layer,
                          has_acc=s_acc is not None),
        grid=(nb // nbk,),
        in_specs=in_specs,
        out_specs=[pl.BlockSpec((nt, nbk, wz), lambda i: (0, i, 0)),
                   s_out_spec,
                   pl.BlockSpec((3, nbk, w3), lambda i: (0, i, 0))],
        out_shape=[jax.ShapeDtypeStruct((nt, nb, wz), F32),
                   jax.ShapeDtypeStruct(s_all.shape, F32),
                   jax.ShapeDtypeStruct((3, nb, w3), F32)],
        scratch_shapes=[pltpu.VMEM((nbk, 2 * nt, wz), F32) for _ in range(4)],
        input_output_aliases=aliases,
        compiler_params=pltpu.CompilerParams(dimension_semantics=("arbitrary",),
                                             vmem_limit_bytes=VMEM_LIMIT),
        name="gdn_sample",
    )(*args)


def _ffn_kernel(x_ref, ysl_ref, yg_ref, wo_ref, nmp_ref, nfp_ref, nfq_ref, wgu_ref, wd_ref, o_ref,
                *, from_time_major):
    nb, tc, d = x_ref.shape
    rows = nb * tc
    x = x_ref[...].reshape(rows, d)
    ysl = ysl_ref[...]
    if from_time_major:
        ysl = jnp.swapaxes(ysl, 0, 1)
    ysl = ysl.reshape(rows, ysl.shape[-1])
    yg = yg_ref[...].reshape(rows, yg_ref.shape[-1])
    mix = jnp.concatenate([ysl, yg], axis=-1)
    x = x + _rms(_mm(mix, wo_ref[...]), nmp_ref[...])
    h = _rms(x, nfp_ref[...])
    gu = _mm(h, wgu_ref[...])
    hid = gu.shape[1] // 2
    gt = gu[:, :hid]
    f = _mm(gt * _sigmoid(gt) * gu[:, hid:], wd_ref[...])
    x = x + _rms(f, nfq_ref[...])
    o_ref[...] = x.reshape(nb, tc, d)


def _ffn(x3, ysl3, yg3, p, *, tc, from_time_major):
    nb, length, d = x3.shape
    n1 = ysl3.shape[-1]
    n2 = yg3.shape[-1]
    if from_time_major:
        ysl_spec = pl.BlockSpec((tc, nb, n1), lambda i: (i, 0, 0))
    else:
        ysl_spec = pl.BlockSpec((nb, tc, n1), lambda i: (0, i, 0))
    return pl.pallas_call(
        functools.partial(_ffn_kernel, from_time_major=from_time_major),
        grid=(length // tc,),
        in_specs=[pl.BlockSpec((nb, tc, d), lambda i: (0, i, 0)), ysl_spec,
                  pl.BlockSpec((nb, tc, n2), lambda i: (0, i, 0)),
                  _const_spec(p["w_out"].shape), _const_spec((1, d)), _const_spec((1, d)),
                  _const_spec((1, d)), _const_spec(p["w_gu"].shape), _const_spec(p["w_down"].shape)],
        out_specs=pl.BlockSpec((nb, tc, d), lambda i: (0, i, 0)),
        out_shape=jax.ShapeDtypeStruct((nb, length, d), F32),
        compiler_params=pltpu.CompilerParams(dimension_semantics=("arbitrary",),
                                             vmem_limit_bytes=VMEM_LIMIT),
        name="ffn",
    )(x3, ysl3, yg3, p["w_out"], p["n_mix_post"], p["n_ffn_pre"], p["n_ffn_post"], p["w_gu"], p["w_down"])


def _block_diag(blocks):
    n, r, c = blocks.shape
    eye = jnp.eye(n, dtype=blocks.dtype)
    return (eye[:, None, :, None] * blocks[:, :, None, :]).reshape(n * r, n * c)


def _pad_lanes(v, offset):
    return jnp.zeros((1, LANES), F32).at[0, offset:offset + v.shape[0]].set(v)


def _layer_params(l, norm_mix_pre, norm_mix_post, norm_ffn_pre, norm_ffn_post, w_in, s5_lambda_re,
                  s5_lambda_im, s5_log_dt, s5_b_re, s5_b_im, s5_c_re, s5_c_im, s5_d, s5_w_glu, s5_b_glu,
                  lru_conv_w, lru_conv_b, lru_w_a, lru_b_a, lru_w_x, lru_b_x, lru_lambda, gdn_conv_w,
                  gdn_a_log, gdn_dt_bias, gdn_norm_w, w_out, ffn_w_gate_up, ffn_w_down):
    g, pst, hgrp = s5_b_re.shape[1:]
    s5w = g * hgrp
    lw = lru_lambda.shape[1]
    nh = gdn_a_log.shape[1]
    hd = gdn_norm_w.shape[1]
    gw = nh * hd
    row = lambda v: v[l][None, :]
    w = w_in[l]
    n_sl = s5w + 2 * lw
    c_qkv, c_a, c_b, c_z = n_sl, n_sl + 3 * gw, n_sl + 3 * gw + nh, n_sl + 3 * gw + 2 * nh
    w_r = jnp.concatenate([w[:, :c_a], w[:, c_z:c_z + gw], w[:, c_a:c_z],
                           jnp.zeros((w.shape[0], LANES - 2 * nh), w.dtype)], axis=1).astype(BF16)
    return {
        "n_sl": n_sl,
        "n_mix_pre": row(norm_mix_pre), "n_mix_post": row(norm_mix_post),
        "n_ffn_pre": row(norm_ffn_pre), "n_ffn_post": row(norm_ffn_post),
        "w_in": w_r,
        "s5_lre": s5_lambda_re[l].reshape(1, g * pst), "s5_lim": s5_lambda_im[l].reshape(1, g * pst),
        "s5_ldt": jnp.repeat(s5_log_dt[l], pst)[None, :],
        "s5_bre": _block_diag(jnp.swapaxes(s5_b_re[l], 1, 2)),
        "s5_bim": _block_diag(jnp.swapaxes(s5_b_im[l], 1, 2)),
        "s5_cre": _block_diag(jnp.swapaxes(s5_c_re[l], 1, 2)),
        "s5_cim": _block_diag(jnp.swapaxes(s5_c_im[l], 1, 2)),
        "s5_d": row(s5_d), "s5_wglu": s5_w_glu[l].astype(BF16), "s5_bglu": row(s5_b_glu),
        "lru_cw": lru_conv_w[l], "lru_cb": row(lru_conv_b),
        "lru_wa": _block_diag(lru_w_a[l]).astype(BF16), "lru_ba": row(lru_b_a),
        "lru_wx": _block_diag(lru_w_x[l]).astype(BF16), "lru_bx": row(lru_b_x),
        "lru_lambda": row(lru_lambda),
        "gdn_cw": gdn_conv_w[l], "gdn_alog": _pad_lanes(gdn_a_log[l], 0),
        "gdn_dtb": _pad_lanes(gdn_dt_bias[l], 0), "gdn_nw": row(gdn_norm_w), "gdn_nh": nh,
        "w_out": w_out[l].astype(BF16), "w_gu": ffn_w_gate_up[l].astype(BF16),
        "w_down": ffn_w_down[l].astype(BF16),
    }


def _prompt_layer(x, p, *, tc, rg):
    nb, length, _ = x.shape
    sl_tm, gd = _in_proj(x, p["n_mix_pre"], p["w_in"], n_sl=p["n_sl"], tc=tc, to_time_major=True)
    n_state = p["s5_lre"].shape[1]
    lw = p["lru_lambda"].shape[1]
    z = lambda *s: jnp.zeros(s, F32)
    ysl, xr, xi, hl, cv = _scan(sl_tm.reshape(length * nb, -1), p, z(nb, n_state), z(nb, n_state),
                                z(nb, lw), z(3 * nb, lw), nb=nb, tc=min(2 * tc, length),
                                first_pos_is_zero=True)
    yg, s_new, gcv = _gdn_prompt(gd, p, rg=rg, ns=2 if nb % 2 == 0 else 1)
    x = _ffn(x, ysl.reshape(length, nb, -1), yg, p, tc=tc, from_time_major=True)
    lcv = jnp.swapaxes(cv.reshape(3, nb, lw), 0, 1)
    return x, (xr, xi, hl, lcv, s_new, gcv)


def _sample_layer(x_tm, st, gdn_s_all, gdn_s_acc, layer, p, *, nbk):
    s5_re, s5_im, lru_h, lru_conv, gdn_conv = st
    nb = s5_re.shape[0]
    rows = x_tm.shape[1]
    nt = rows // nb
    sl, gd = _in_proj(x_tm, p["n_mix_pre"], p["w_in"], n_sl=p["n_sl"], tc=rows, to_time_major=False)
    lw = p["lru_lambda"].shape[1]
    cv0 = jnp.swapaxes(lru_conv, 0, 1).reshape(3 * nb, lw)
    ysl, xr, xi, hl, cv = _scan(sl.reshape(rows, -1), p, s5_re.reshape(nb, -1), s5_im.reshape(nb, -1),
                                lru_h, cv0, nb=nb, tc=nt, first_pos_is_zero=(PAST_LEN == 0))
    yg, s_acc, gcv = _gdn_sample(gd.reshape(nt, nb, -1), jnp.swapaxes(gdn_conv, 0, 1), gdn_s_all, gdn_s_acc,
                                 layer, p, nbk=nbk)
    x_tm = _ffn(x_tm, ysl.reshape(1, rows, -1), yg.reshape(1, rows, -1), p, tc=rows, from_time_major=False)
    lcv = jnp.swapaxes(cv.reshape(3, nb, lw), 0, 1)
    return x_tm, (xr, xi, hl, lcv, jnp.swapaxes(gcv, 0, 1)), s_acc


def kernel(x_prompt, x_sample, state_s5_re, state_s5_im, state_lru_h, state_lru_conv, state_gdn_S, state_gdn_conv, norm_mix_pre, norm_mix_post, norm_ffn_pre, norm_ffn_post, w_in, s5_lambda_re, s5_lambda_im, s5_log_dt, s5_b_re, s5_b_im, s5_c_re, s5_c_im, s5_d, s5_w_glu, s5_b_glu, lru_conv_w, lru_conv_b, lru_w_a, lru_b_a, lru_w_x, lru_b_x, lru_lambda, gdn_conv_w, gdn_a_log, gdn_dt_bias, gdn_norm_w, w_out, ffn_w_gate_up, ffn_w_down):
    depth = w_in.shape[0]
    bp, lp, d = x_prompt.shape
    bs, ls, _ = x_sample.shape
    g, pst = s5_lambda_re.shape[1:]
    tc = min(64, lp)
    rg = min(4 * GDN_CHUNK, lp)
    y_p = x_prompt
    y_s = jnp.swapaxes(x_sample, 0, 1).reshape(1, ls * bs, d)
    new_p = [[] for _ in range(6)]
    new_s = [[] for _ in range(5)]
    s_gdn_s = None
    for l in range(depth):
        p = _layer_params(l, norm_mix_pre, norm_mix_post, norm_ffn_pre, norm_ffn_post, w_in, s5_lambda_re,
                          s5_lambda_im, s5_log_dt, s5_b_re, s5_b_im, s5_c_re, s5_c_im, s5_d, s5_w_glu,
                          s5_b_glu, lru_conv_w, lru_conv_b, lru_w_a, lru_b_a, lru_w_x, lru_b_x, lru_lambda,
                          gdn_conv_w, gdn_a_log, gdn_dt_bias, gdn_norm_w, w_out, ffn_w_gate_up, ffn_w_down)
        y_p, sp = _prompt_layer(y_p, p, tc=tc, rg=rg)
        st_s = (state_s5_re[l], state_s5_im[l], state_lru_h[l], state_lru_conv[l], state_gdn_conv[l])
        y_s, ss, s_gdn_s = _sample_layer(y_s, st_s, state_gdn_S, s_gdn_s, l, p, nbk=min(8, bs))
        for j in range(6):
            new_p[j].append(sp[j])
        for j in range(5):
            new_s[j].append(ss[j])
    y_s = jnp.swapaxes(y_s.reshape(ls, bs, d), 0, 1)
    outs_p = [jnp.stack(t, axis=0) for t in new_p]
    outs_s = [jnp.stack(t, axis=0) for t in new_s]
    outs_s.insert(4, s_gdn_s)
    outs_p[0] = outs_p[0].reshape(depth, bp, g, pst)
    outs_p[1] = outs_p[1].reshape(depth, bp, g, pst)
    outs_s[0] = outs_s[0].reshape(depth, bs, g, pst)
    outs_s[1] = outs_s[1].reshape(depth, bs, g, pst)
    return (y_p, y_s, *outs_p, *outs_s)
```

```python
import functools

import jax
import jax.numpy as jnp
from jax import lax
from jax.experimental import pallas as pl
from jax.experimental.pallas import tpu as pltpu

F32 = jnp.float32
BF16 = jnp.bfloat16
HIGHEST = lax.Precision.HIGHEST

NORM_EPS = 1e-6
LRU_C = 8.0
GDN_CHUNK = 64
PAST_LEN = 16384
LANES = 128
SUBLANES = 8
VMEM_LIMIT = 56 * 1024 * 1024


def _rms(x, w):
    return x * lax.rsqrt(jnp.mean(x * x, axis=-1, keepdims=True) + NORM_EPS) * w


def _sigmoid(x):
    return jax.nn.sigmoid(x)


def _softplus(x):
    return jnp.maximum(x, 0.0) + jnp.log1p(jnp.exp(-jnp.abs(x)))


def _mm(a, b):
    return jnp.dot(a.astype(BF16), b.astype(BF16), preferred_element_type=F32)


def _mm_nt(a, b):
    return lax.dot_general(a.astype(BF16), b.astype(BF16), (((1,), (1,)), ((), ())),
                           preferred_element_type=F32)


def _mm_tn(a, b):
    return lax.dot_general(a.astype(BF16), b.astype(BF16), (((0,), (0,)), ((), ())),
                           preferred_element_type=F32)


def _const_spec(shape):
    n = len(shape)
    return pl.BlockSpec(shape, lambda *_: (0,) * n, pipeline_mode=pl.Buffered(1))


def _l2n_heads(x, nh, hd, mul):
    def one(h):
        xh = x[:, h * hd:(h + 1) * hd]
        return xh * (lax.rsqrt(jnp.sum(xh * xh, axis=-1, keepdims=True) + NORM_EPS) * mul)
    return jnp.concatenate([one(h) for h in range(nh)], axis=-1)


def _in_proj_kernel(x_ref, nw_ref, w_ref, sl_ref, gd_ref, *, n_sl, to_time_major):
    nb, tc, d = x_ref.shape
    x = x_ref[...].reshape(nb * tc, d)
    h = _rms(x, nw_ref[...])
    p = _mm(h, w_ref[...])
    sl = p[:, :n_sl].reshape(nb, tc, n_sl)
    if to_time_major:
        sl = jnp.swapaxes(sl, 0, 1)
    sl_ref[...] = sl
    gd_ref[...] = p[:, n_sl:].reshape(nb, tc, p.shape[1] - n_sl)


def _in_proj(x3, nw, w, *, n_sl, tc, to_time_major):
    nb, length, d = x3.shape
    n_tot = w.shape[1]
    n_gd = n_tot - n_sl
    if to_time_major:
        sl_shape, sl_spec = (length, nb, n_sl), pl.BlockSpec((tc, nb, n_sl), lambda i: (i, 0, 0))
    else:
        sl_shape, sl_spec = (nb, length, n_sl), pl.BlockSpec((nb, tc, n_sl), lambda i: (0, i, 0))
    return pl.pallas_call(
        functools.partial(_in_proj_kernel, n_sl=n_sl, to_time_major=to_time_major),
        grid=(length // tc,),
        in_specs=[pl.BlockSpec((nb, tc, d), lambda i: (0, i, 0)),
                  _const_spec((1, d)), _const_spec((d, n_tot))],
        out_specs=[sl_spec, pl.BlockSpec((nb, tc, n_gd), lambda i: (0, i, 0))],
        out_shape=[jax.ShapeDtypeStruct(sl_shape, F32), jax.ShapeDtypeStruct((nb, length, n_gd), F32)],
        compiler_params=pltpu.CompilerParams(dimension_semantics=("arbitrary",),
                                             vmem_limit_bytes=VMEM_LIMIT),
        name="in_proj",
    )(x3, nw, w)


def _scan_kernel(sl_ref, lre_ref, lim_ref, ldt_ref, bre_ref, bim_ref, cre_ref, cim_ref, d_ref,
                 wglu_ref, bglu_ref, cw_ref, cb_ref, wa_ref, ba_ref, wx_ref, bx_ref, lam_ref,
                 xr0_ref, xi0_ref, h0_ref, cv0_ref,
                 y_ref, xr_out, xi_out, h_out, cv_out,
                 ar_s, ai_s, bbr_s, bbi_s, crb_s, cib_s, xr_c, xi_c, h_c, xr_s, xi_s, xpad_s, a_s, b_s,
                 *, nb, tc, s5w, lw, first_pos_is_zero):
    i = pl.program_id(0)
    last = pl.num_programs(0) - 1
    rows = nb * tc

    @pl.when(i == 0)
    def _init():
        lr = jnp.minimum(lre_ref[...], -1e-4)
        li = lim_ref[...]
        dt = jnp.exp(ldt_ref[...])
        mag = jnp.exp(lr * dt)
        ar = mag * jnp.cos(li * dt)
        ai = mag * jnp.sin(li * dt)
        den = lr * lr + li * li
        fr = ((ar - 1.0) * lr + ai * li) / den
        fi = (ai * lr - (ar - 1.0) * li) / den
        ar_s[...] = ar
        ai_s[...] = ai
        bre = bre_ref[...]
        bim = bim_ref[...]
        bbr_s[...] = (fr * bre - fi * bim).astype(BF16)
        bbi_s[...] = (fr * bim + fi * bre).astype(BF16)
        crb_s[...] = cre_ref[...].astype(BF16)
        cib_s[...] = cim_ref[...].astype(BF16)
        xr_c[...] = xr0_ref[...]
        xi_c[...] = xi0_ref[...]
        h_c[...] = h0_ref[...]
        xpad_s[0:3 * nb, :] = cv0_ref[...]

    u = sl_ref[:, 0:s5w]
    xb = sl_ref[:, s5w:s5w + lw]
    gate = sl_ref[:, s5w + lw:s5w + 2 * lw]

    ub = u.astype(BF16)
    xr_s[...] = jnp.dot(ub, bbr_s[...], preferred_element_type=F32)
    xi_s[...] = jnp.dot(ub, bbi_s[...], preferred_element_type=F32)

    xpad_s[3 * nb:3 * nb + rows, :] = xb
    cw = cw_ref[...]
    xc = (xpad_s[0:rows, :] * cw[0:1] + xpad_s[nb:nb + rows, :] * cw[1:2]
          + xpad_s[2 * nb:2 * nb + rows, :] * cw[2:3] + xb * cw[3:4]) + cb_ref[...]
    new_cv = xpad_s[rows:rows + 3 * nb, :]
    xpad_s[0:3 * nb, :] = new_cv
    r = _sigmoid(_mm(xc, wa_ref[...]) + ba_ref[...])
    gi = _sigmoid(_mm(xc, wx_ref[...]) + bx_ref[...])
    log_a = -LRU_C * r * _softplus(-lam_ref[...])
    a = jnp.exp(log_a)
    mult = jnp.sqrt(-jnp.tanh(log_a) * (a * a + 1.0))
    if first_pos_is_zero:
        rid = lax.broadcasted_iota(jnp.int32, (rows, 1), 0)
        mult = jnp.where(jnp.logical_and(rid < nb, i == 0), 1.0, mult)
    a_s[...] = a
    b_s[...] = mult * gi * xc

    ar = jnp.broadcast_to(ar_s[...], (nb, ar_s.shape[1]))
    ai = jnp.broadcast_to(ai_s[...], (nb, ai_s.shape[1]))

    def step(t, carry):
        xr, xi, h = carry
        sl_t = pl.ds(pl.multiple_of(t * nb, nb), nb)
        nxr = ar * xr - ai * xi + xr_s[sl_t, :]
        nxi = ar * xi + ai * xr + xi_s[sl_t, :]
        nh = a_s[sl_t, :] * h + b_s[sl_t, :]
        xr_s[sl_t, :] = nxr
        xi_s[sl_t, :] = nxi
        b_s[sl_t, :] = nh
        return nxr, nxi, nh

    xr, xi, h = lax.fori_loop(0, tc, step, (xr_c[...], xi_c[...], h_c[...]),
                              unroll=(True if tc <= 8 else 8))
    xr_c[...] = xr
    xi_c[...] = xi
    h_c[...] = h

    y = (jnp.dot(xr_s[...].astype(BF16), crb_s[...], preferred_element_type=F32)
         - jnp.dot(xi_s[...].astype(BF16), cib_s[...], preferred_element_type=F32))
    y = jax.nn.gelu(y + d_ref[...] * u)
    y = y * _sigmoid(_mm(y, wglu_ref[...]) + bglu_ref[...])
    y_ref[:, 0:s5w] = y
    y_ref[:, s5w:s5w + lw] = b_s[...] * jax.nn.gelu(gate)

    @pl.when(i == last)
    def _fin():
        xr_out[...] = xr
        xi_out[...] = xi
        h_out[...] = h
        cv_out[...] = new_cv


def _scan(sl2, p, xr0, xi0, h0, cv0, *, nb, tc, first_pos_is_zero):
    rows_total, n_sl = sl2.shape
    s5w = p["s5_d"].shape[1]
    lw = p["lru_lambda"].shape[1]
    n_state = p["s5_lre"].shape[1]
    rows = nb * tc
    params = [p["s5_lre"], p["s5_lim"], p["s5_ldt"], p["s5_bre"], p["s5_bim"], p["s5_cre"], p["s5_cim"],
              p["s5_d"], p["s5_wglu"], p["s5_bglu"], p["lru_cw"], p["lru_cb"], p["lru_wa"], p["lru_ba"],
              p["lru_wx"], p["lru_bx"], p["lru_lambda"]]
    states = [xr0, xi0, h0, cv0]
    return pl.pallas_call(
        functools.partial(_scan_kernel, nb=nb, tc=tc, s5w=s5w, lw=lw, first_pos_is_zero=first_pos_is_zero),
        grid=(rows_total // rows,),
        in_specs=[pl.BlockSpec((rows, n_sl), lambda i: (i, 0))]
        + [_const_spec(a.shape) for a in params + states],
        out_specs=[pl.BlockSpec((rows, s5w + lw), lambda i: (i, 0)),
                   pl.BlockSpec((nb, n_state), lambda i: (0, 0)),
                   pl.BlockSpec((nb, n_state), lambda i: (0, 0)),
                   pl.BlockSpec((nb, lw), lambda i: (0, 0)),
                   pl.BlockSpec((3 * nb, lw), lambda i: (0, 0))],
        out_shape=[jax.ShapeDtypeStruct((rows_total, s5w + lw), F32),
                   jax.ShapeDtypeStruct((nb, n_state), F32),
                   jax.ShapeDtypeStruct((nb, n_state), F32),
                   jax.ShapeDtypeStruct((nb, lw), F32),
                   jax.ShapeDtypeStruct((3 * nb, lw), F32)],
        scratch_shapes=[pltpu.VMEM((1, n_state), F32), pltpu.VMEM((1, n_state), F32),
                        pltpu.VMEM((s5w, n_state), BF16), pltpu.VMEM((s5w, n_state), BF16),
                        pltpu.VMEM((n_state, s5w), BF16), pltpu.VMEM((n_state, s5w), BF16),
                        pltpu.VMEM((nb, n_state), F32), pltpu.VMEM((nb, n_state), F32),
                        pltpu.VMEM((nb, lw), F32),
                        pltpu.VMEM((rows, n_state), F32), pltpu.VMEM((rows, n_state), F32),
                        pltpu.VMEM((rows + 3 * nb, lw), F32),
                        pltpu.VMEM((rows, lw), F32), pltpu.VMEM((rows, lw), F32)],
        compiler_params=pltpu.CompilerParams(dimension_semantics=("arbitrary",),
                                             vmem_limit_bytes=VMEM_LIMIT),
        name="scan",
    )(sl2, *params, *states)


def _gdn_prompt_kernel(gd_ref, cw_ref, alog_ref, dtb_ref, nw_ref, y_ref, s_out, cv_out,
                       s_s, xp_s, *, ns, rg, nh, hd):
    i = pl.program_id(1)
    last = pl.num_programs(1) - 1
    w3 = 3 * nh * hd
    wz = nh * hd
    c = GDN_CHUNK
    nc = rg // c
    wc = nh * c
    rt = ns * rg

    @pl.when(i == 0)
    def _init():
        s_s[...] = jnp.zeros_like(s_s)
        xp_s[...] = jnp.zeros_like(xp_s)

    cw = cw_ref[...]
    xcs = []
    for s in range(ns):
        qkv = gd_ref[s, :, 0:w3]
        xe = jnp.concatenate([xp_s[s], qkv], axis=0)
        acc = qkv * cw[3:4]
        for j in range(1, 4):
            acc = acc + pltpu.roll(xe, j, 0)[SUBLANES:, :] * cw[3 - j:4 - j]
        xp_s[s] = qkv[rg - SUBLANES:, :]
        xcs.append(acc)
    xc = jnp.concatenate(xcs, axis=0)
    xc = xc * _sigmoid(xc)

    ab = jnp.concatenate([gd_ref[s, :, w3 + wz:w3 + wz + LANES] for s in range(ns)], axis=0)
    g_all = -jnp.exp(alog_ref[...]) * _softplus(ab + dtb_ref[...])
    beta_all = _sigmoid(ab)

    def per_head(fn):
        return jnp.concatenate([fn(h) for h in range(nh)], axis=-1)

    q_all = _l2n_heads(xc[:, 0:wz], nh, hd, hd ** -0.5)
    k_all = _l2n_heads(xc[:, wz:2 * wz], nh, hd, 1.0)
    v_all = xc[:, 2 * wz:3 * wz]
    beta_w = per_head(lambda h: jnp.broadcast_to(beta_all[:, nh + h:nh + h + 1], (rt, hd)))
    kb_all = k_all * beta_w
    vb_all = v_all * beta_w

    ri = lax.broadcasted_iota(jnp.int32, (c, wc), 0)
    cj = lax.broadcasted_iota(jnp.int32, (c, wc), 1) % c
    incl = ri >= cj
    strict = ri > cj
    eye = (ri == cj).astype(F32)
    r1 = lax.broadcasted_iota(jnp.int32, (c, c), 0)
    c1 = lax.broadcasted_iota(jnp.int32, (c, c), 1)
    ltri = (r1 >= c1).astype(F32)
    bd_cc = (lax.broadcasted_iota(jnp.int32, (wc, wc), 0) // c
             == lax.broadcasted_iota(jnp.int32, (wc, wc), 1) // c)
    bd_cd = (lax.broadcasted_iota(jnp.int32, (wc, wz), 0) // c
             == lax.broadcasted_iota(jnp.int32, (wc, wz), 1) // hd)

    def bdiag(x, mask):
        return jnp.where(mask, jnp.concatenate([x] * nh, axis=0), 0.0).astype(BF16)

    nc_all = ns * nc
    rows = [slice(cc * c, (cc + 1) * c) for cc in range(nc_all)]
    gcum = [jnp.dot(ltri, g_all[r], precision=HIGHEST, preferred_element_type=F32) for r in rows]
    cgc = [per_head(lambda h, x=x: jnp.broadcast_to(x[:, h:h + 1], (c, hd))) for x in gcum]
    gcol = [per_head(lambda h, x=x: jnp.broadcast_to(x[:, h:h + 1], (c, c))) for x in gcum]
    grow = [jnp.concatenate([xt[h:h + 1, :] for h in range(nh)], axis=1)
            for xt in [x.T for x in gcum]]
    every = range(nc_all)
    decay = [jnp.exp(jnp.where(incl, gcol[n] - grow[n], -jnp.inf)) for n in every]
    eg = [jnp.exp(x) for x in cgc]
    g_last = [x[c - 1:c, :] for x in cgc]
    kk = [_mm_nt(jnp.concatenate([kb_all[r], q_all[r]], axis=0), bdiag(k_all[r], bd_cd)) for r in rows]
    attn = [kk[n][c:2 * c] * decay[n] for n in every]
    rj = [-jnp.where(strict, kk[n][0:c] * decay[n], 0.0) for n in every]
    sj = [eye + rj[n] for n in every]
    rj = [_mm(rj[n], bdiag(rj[n], bd_cc)) for n in every]
    for _ in range(4):
        rs = [_mm(jnp.concatenate([rj[n], sj[n]], axis=0), bdiag(rj[n], bd_cc)) for n in every]
        rj = [x[0:c] for x in rs]
        sj = [sj[n] + rs[n][c:2 * c] for n in every]
    sj = [sj[n] + _mm(sj[n], bdiag(rj[n], bd_cc)) for n in every]
    w = [_mm(sj[n], jnp.concatenate([bdiag(vb_all[rows[n]], bd_cd),
                                     bdiag(kb_all[rows[n]] * eg[n], bd_cd)], axis=1)) for n in every]
    q_dec = [q_all[rows[n]] * eg[n] for n in every]
    k_dec = [k_all[rows[n]] * jnp.exp(g_last[n] - cgc[n]) for n in every]
    s_dec = [jnp.exp(x) for x in g_last]

    s_cur = [[s_s[s, h] for h in range(nh)] for s in range(ns)]
    hs = [slice(h * hd, (h + 1) * hd) for h in range(nh)]
    for m in range(nc):
        for s in range(ns):
            n = s * nc + m
            st = s_cur[s]
            ws = [_mm(jnp.concatenate([w[n][:, wz + h * hd:wz + (h + 1) * hd], q_dec[n][:, hs[h]]], axis=0),
                      st[h]) for h in range(nh)]
            v_new = jnp.concatenate([w[n][:, hs[h]] - ws[h][0:c] for h in range(nh)], axis=-1)
            o = jnp.concatenate([x[c:2 * c] for x in ws], axis=-1) + _mm(attn[n], bdiag(v_new, bd_cd))
            s_cur[s] = [st[h] * s_dec[n][:, hs[h]] + _mm_tn(k_dec[n][:, hs[h]], v_new[:, hs[h]])
                        for h in range(nh)]
            for h in range(nh):
                z = gd_ref[s, m * c:(m + 1) * c, w3 + h * hd:w3 + (h + 1) * hd]
                y_ref[s, m * c:(m + 1) * c, hs[h]] = _rms(o[:, hs[h]], nw_ref[...]) * (z * _sigmoid(z))
    for s in range(ns):
        for h in range(nh):
            s_s[s, h] = s_cur[s][h]

    @pl.when(i == last)
    def _fin():
        for s in range(ns):
            for h in range(nh):
                s_out[s, h] = s_cur[s][h]
            cv_out[s] = xp_s[s, SUBLANES - 3:SUBLANES, :]


def _gdn_prompt(gd3, p, *, rg, ns):
    nb, length, n_gd = gd3.shape
    nh = p["gdn_nh"]
    hd = p["gdn_nw"].shape[1]
    w3 = 3 * nh * hd
    return pl.pallas_call(
        functools.partial(_gdn_prompt_kernel, ns=ns, rg=rg, nh=nh, hd=hd),
        grid=(nb // ns, length // rg),
        in_specs=[pl.BlockSpec((ns, rg, n_gd), lambda b, i: (b, i, 0)),
                  pl.BlockSpec((4, w3), lambda b, i: (0, 0)),
                  pl.BlockSpec((1, LANES), lambda b, i: (0, 0)),
                  pl.BlockSpec((1, LANES), lambda b, i: (0, 0)),
                  pl.BlockSpec((1, hd), lambda b, i: (0, 0))],
        out_specs=[pl.BlockSpec((ns, rg, nh * hd), lambda b, i: (b, i, 0)),
                   pl.BlockSpec((ns, nh, hd, hd), lambda b, i: (b, 0, 0, 0)),
                   pl.BlockSpec((ns, 3, w3), lambda b, i: (b, 0, 0))],
        out_shape=[jax.ShapeDtypeStruct((nb, length, nh * hd), F32),
                   jax.ShapeDtypeStruct((nb, nh, hd, hd), F32),
                   jax.ShapeDtypeStruct((nb, 3, w3), F32)],
        scratch_shapes=[pltpu.VMEM((ns, nh, hd, hd), F32), pltpu.VMEM((ns, SUBLANES, w3), F32)],
        compiler_params=pltpu.CompilerParams(dimension_semantics=("arbitrary", "arbitrary"),
                                             vmem_limit_bytes=VMEM_LIMIT),
        name="gdn_prompt",
    )(gd3, p["gdn_cw"], p["gdn_alog"], p["gdn_dtb"], p["gdn_nw"])


def _gdn_sample_kernel(*refs, nt, nbk, nh, hd, layer, has_acc):
    gd_ref, cv0_ref, s0_ref, cw_ref, alog_ref, dtb_ref, nw_ref = refs[:7]
    y_ref, s_out, cv_out, wk_s, kd_s, wv_s, ws_s = refs[7 + int(has_acc):]
    if not has_acc:
        for other in range(s_out.shape[0]):
            if other != layer:
                s_out[other] = jnp.zeros(s_out.shape[1:], F32)
        s_out = s_out.at[layer]
    w3 = 3 * nh * hd
    wz = nh * hd
    cw = cw_ref[...]
    scale = hd ** -0.5
    xs = [cv0_ref[j] for j in range(3)] + [gd_ref[t, :, 0:w3] for t in range(nt)]
    for j in range(3):
        cv_out[j] = xs[nt + j]
    q, k, v, gc, beta = [], [], [], [], []
    for t in range(nt):
        xc = xs[t] * cw[0:1] + xs[t + 1] * cw[1:2] + xs[t + 2] * cw[2:3] + xs[t + 3] * cw[3:4]
        xc = xc * _sigmoid(xc)
        ab = gd_ref[t, :, w3 + wz:w3 + wz + LANES]
        g_t = -jnp.exp(alog_ref[...]) * _softplus(ab + dtb_ref[...])
        gc.append(g_t if t == 0 else gc[-1] + g_t)
        beta.append(_sigmoid(ab))
        qn = _l2n_heads(xc[:, 0:wz], nh, hd, scale)
        kn = _l2n_heads(xc[:, wz:2 * wz], nh, hd, 1.0)
        q.append([qn[:, h * hd:(h + 1) * hd] for h in range(nh)])
        k.append([kn[:, h * hd:(h + 1) * hd] for h in range(nh)])
        v.append([xc[:, 2 * wz + h * hd:2 * wz + (h + 1) * hd] for h in range(nh)])

    def dot(a, b):
        return jnp.sum(a * b, axis=-1, keepdims=True)

    wk_rows = [[None] * nh for _ in range(nt)]
    qd_rows = [[None] * nh for _ in range(nt)]
    kd_rows = [[None] * nh for _ in range(nt)]
    wv_rows = [[None] * nh for _ in range(nt)]
    sd_rows = [None] * nh
    attn = [None] * nh
    for h in range(nh):
        gch = [gc[t][:, h:h + 1] for t in range(nt)]
        bh = [beta[t][:, nh + h:nh + h + 1] for t in range(nt)]
        e = [jnp.exp(x) for x in gch]
        kb = [k[t][h] * bh[t] for t in range(nt)]
        vb = [v[t][h] * bh[t] for t in range(nt)]
        kbe = [kb[t] * e[t] for t in range(nt)]
        dec = [[jnp.exp(gch[i] - gch[j]) for j in range(i)] for i in range(nt)]
        m = [[dot(kb[i], k[j][h]) * dec[i][j] for j in range(i)] for i in range(nt)]
        attn[h] = [[dot(q[i][h], k[j][h]) * (dec[i][j] if j < i else 1.0) for j in range(i + 1)]
                   for i in range(nt)]
        tinv = [[None] * nt for _ in range(nt)]
        for j in range(nt):
            for i in range(j + 1, nt):
                acc = m[i][j]
                for l in range(j + 1, i):
                    acc = acc + m[i][l] * tinv[l][j]
                tinv[i][j] = -acc
        for i in range(nt):
            wv_i, wk_i = vb[i], kbe[i]
            for j in range(i):
                wv_i = wv_i + tinv[i][j] * vb[j]
                wk_i = wk_i + tinv[i][j] * kbe[j]
            wv_rows[i][h] = wv_i
            wk_rows[i][h] = wk_i
            qd_rows[i][h] = q[i][h] * e[i]
            kd_rows[i][h] = k[i][h] * jnp.exp(gch[nt - 1] - gch[i])
        sd_rows[h] = jnp.broadcast_to(e[nt - 1], (nbk, hd))

    cat = lambda parts: jnp.concatenate(parts, axis=-1)
    to_seq = lambda slabs: jnp.swapaxes(jnp.stack(slabs, axis=0), 0, 1)
    zero = jnp.zeros((nbk, wz), F32)
    wk_s[...] = to_seq([cat(wk_rows[i]) for i in range(nt)] + [cat(qd_rows[i]) for i in range(nt)])
    kd_s[...] = to_seq([cat(kd_rows[i]) for i in range(nt)] + [zero] * nt)
    wv_s[...] = to_seq([cat(wv_rows[i]) for i in range(nt)] + [cat(sd_rows)] * nt)

    def per_seq(b, _):
        twk_all = wk_s[b]
        tkd_all = kd_s[b]
        twv_all = wv_s[b]
        for h in range(nh):
            cols = slice(h * hd, (h + 1) * hd)
            twk, tkd, twv = twk_all[:, cols], tkd_all[:, cols], twv_all[:, cols]
            a = _mm_tn(tkd, jnp.concatenate([twv, twk], axis=1))
            s = s0_ref[b, h]
            r = _mm(jnp.concatenate([twk, a[:, hd:]], axis=0), s)
            s_out[b, h] = s * twv[nt:nt + 1, :] + a[:, 0:hd] - r[2 * nt:, :]
            ws_s[b, :, cols] = r[0:2 * nt, :]
        return 0

    lax.fori_loop(0, nbk, per_seq, 0, unroll=4)

    ws_tm = jnp.swapaxes(ws_s[...], 0, 1)
    for h in range(nh):
        cols = slice(h * hd, (h + 1) * hd)
        v_new = [wv_rows[j][h] - ws_tm[j][:, cols] for j in range(nt)]
        for i in range(nt):
            o = ws_tm[nt + i][:, cols]
            for j in range(i + 1):
                o = o + attn[h][i][j] * v_new[j]
            z = gd_ref[i, :, w3 + h * hd:w3 + (h + 1) * hd]
            y_ref[i, :, cols] = _rms(o, nw_ref[...]) * (z * _sigmoid(z))


def _gdn_sample(gd3, cv0, s_all, s_acc, layer, p, *, nbk):
    nt, nb, n_gd = gd3.shape
    nh = p["gdn_nh"]
    hd = p["gdn_nw"].shape[1]
    w3 = 3 * nh * hd
    wz = nh * hd
    s_spec = pl.BlockSpec((None, nbk, nh, hd, hd), lambda i: (layer, i, 0, 0, 0))
    args = [gd3, cv0, s_all, p["gdn_cw"], p["gdn_alog"], p["gdn_dtb"], p["gdn_nw"]]
    in_specs = [pl.BlockSpec((nt, nbk, n_gd), lambda i: (0, i, 0)),
                pl.BlockSpec((3, nbk, w3), lambda i: (0, i, 0)),
                s_spec,
                pl.BlockSpec((4, w3), lambda i: (0, 0)),
                pl.BlockSpec((1, LANES), lambda i: (0, 0)),
                pl.BlockSpec((1, LANES), lambda i: (0, 0)),
                pl.BlockSpec((1, hd), lambda i: (0, 0))]
    aliases = {}
    s_out_spec = s_spec
    if s_acc is not None:
        aliases = {len(args): 1}
        args.append(s_acc)
        in_specs.append(pl.BlockSpec(memory_space=pl.ANY))
    else:
        s_out_spec = pl.BlockSpec((s_all.shape[0], nbk, nh, hd, hd), lambda i: (0, i, 0, 0, 0))
    return pl.pallas_call(
        functools.partial(_gdn_sample_kernel, nt=nt, nbk=nbk, nh=nh, hd=hd, layer=layer,
                          has_acc=s_acc is not None),
        grid=(nb // nbk,),
        in_specs=in_specs,
        out_specs=[pl.BlockSpec((nt, nbk, wz), lambda i: (0, i, 0)),
                   s_out_spec,
                   pl.BlockSpec((3, nbk, w3), lambda i: (0, i, 0))],
        out_shape=[jax.ShapeDtypeStruct((nt, nb, wz), F32),
                   jax.ShapeDtypeStruct(s_all.shape, F32),
                   jax.ShapeDtypeStruct((3, nb, w3), F32)],
        scratch_shapes=[pltpu.VMEM((nbk, 2 * nt, wz), F32) for _ in range(4)],
        input_output_aliases=aliases,
        compiler_params=pltpu.CompilerParams(dimension_semantics=("arbitrary",),
                                             vmem_limit_bytes=VMEM_LIMIT),
        name="gdn_sample",
    )(*args)


def _ffn_kernel(x_ref, ysl_ref, yg_ref, wo_ref, nmp_ref, nfp_ref, nfq_ref, wgu_ref, wd_ref, o_ref,
                *, from_time_major):
    nb, tc, d = x_ref.shape
    rows = nb * tc
    x = x_ref[...].reshape(rows, d)
    ysl = ysl_ref[...]
    if from_time_major:
        ysl = jnp.swapaxes(ysl, 0, 1)
    ysl = ysl.reshape(rows, ysl.shape[-1])
    yg = yg_ref[...].reshape(rows, yg_ref.shape[-1])
    mix = jnp.concatenate([ysl, yg], axis=-1)
    x = x + _rms(_mm(mix, wo_ref[...]), nmp_ref[...])
    h = _rms(x, nfp_ref[...])
    gu = _mm(h, wgu_ref[...])
    hid = gu.shape[1] // 2
    gt = gu[:, :hid]
    f = _mm(gt * _sigmoid(gt) * gu[:, hid:], wd_ref[...])
    x = x + _rms(f, nfq_ref[...])
    o_ref[...] = x.reshape(nb, tc, d)


def _ffn(x3, ysl3, yg3, p, *, tc, from_time_major):
    nb, length, d = x3.shape
    n1 = ysl3.shape[-1]
    n2 = yg3.shape[-1]
    if from_time_major:
        ysl_spec = pl.BlockSpec((tc, nb, n1), lambda i: (i, 0, 0))
    else:
        ysl_spec = pl.BlockSpec((nb, tc, n1), lambda i: (0, i, 0))
    return pl.pallas_call(
        functools.partial(_ffn_kernel, from_time_major=from_time_major),
        grid=(length // tc,),
        in_specs=[pl.BlockSpec((nb, tc, d), lambda i: (0, i, 0)), ysl_spec,
                  pl.BlockSpec((nb, tc, n2), lambda i: (0, i, 0)),
                  _const_spec(p["w_out"].shape), _const_spec((1, d)), _const_spec((1, d)),
                  _const_spec((1, d)), _const_spec(p["w_gu"].shape), _const_spec(p["w_down"].shape)],
        out_specs=pl.BlockSpec((nb, tc, d), lambda i: (0, i, 0)),
        out_shape=jax.ShapeDtypeStruct((nb, length, d), F32),
        compiler_params=pltpu.CompilerParams(dimension_semantics=("arbitrary",),
                                             vmem_limit_bytes=VMEM_LIMIT),
        name="ffn",
    )(x3, ysl3, yg3, p["w_out"], p["n_mix_post"], p["n_ffn_pre"], p["n_ffn_post"], p["w_gu"], p["w_down"])


def _block_diag(blocks):
    n, r, c = blocks.shape
    eye = jnp.eye(n, dtype=blocks.dtype)
    return (eye[:, None, :, None] * blocks[:, :, None, :]).reshape(n * r, n * c)


def _pad_lanes(v, offset):
    return jnp.zeros((1, LANES), F32).at[0, offset:offset + v.shape[0]].set(v)


def _layer_params(l, norm_mix_pre, norm_mix_post, norm_ffn_pre, norm_ffn_post, w_in, s5_lambda_re,
                  s5_lambda_im, s5_log_dt, s5_b_re, s5_b_im, s5_c_re, s5_c_im, s5_d, s5_w_glu, s5_b_glu,
                  lru_conv_w, lru_conv_b, lru_w_a, lru_b_a, lru_w_x, lru_b_x, lru_lambda, gdn_conv_w,
                  gdn_a_log, gdn_dt_bias, gdn_norm_w, w_out, ffn_w_gate_up, ffn_w_down):
    g, pst, hgrp = s5_b_re.shape[1:]
    s5w = g * hgrp
    lw = lru_lambda.shape[1]
    nh = gdn_a_log.shape[1]
    hd = gdn_norm_w.shape[1]
    gw = nh * hd
    row = lambda v: v[l][None, :]
    w = w_in[l]
    n_sl = s5w + 2 * lw
    c_qkv, c_a, c_b, c_z = n_sl, n_sl + 3 * gw, n_sl + 3 * gw + nh, n_sl + 3 * gw + 2 * nh
    w_r = jnp.concatenate([w[:, :c_a], w[:, c_z:c_z + gw], w[:, c_a:c_z],
                           jnp.zeros((w.shape[0], LANES - 2 * nh), w.dtype)], axis=1).astype(BF16)
    return {
        "n_sl": n_sl,
        "n_mix_pre": row(norm_mix_pre), "n_mix_post": row(norm_mix_post),
        "n_ffn_pre": row(norm_ffn_pre), "n_ffn_post": row(norm_ffn_post),
        "w_in": w_r,
        "s5_lre": s5_lambda_re[l].reshape(1, g * pst), "s5_lim": s5_lambda_im[l].reshape(1, g * pst),
        "s5_ldt": jnp.repeat(s5_log_dt[l], pst)[None, :],
        "s5_bre": _block_diag(jnp.swapaxes(s5_b_re[l], 1, 2)),
        "s5_bim": _block_diag(jnp.swapaxes(s5_b_im[l], 1, 2)),
        "s5_cre": _block_diag(jnp.swapaxes(s5_c_re[l], 1, 2)),
        "s5_cim": _block_diag(jnp.swapaxes(s5_c_im[l], 1, 2)),
        "s5_d": row(s5_d), "s5_wglu": s5_w_glu[l].astype(BF16), "s5_bglu": row(s5_b_glu),
        "lru_cw": lru_conv_w[l], "lru_cb": row(lru_conv_b),
        "lru_wa": _block_diag(lru_w_a[l]).astype(BF16), "lru_ba": row(lru_b_a),
        "lru_wx": _block_diag(lru_w_x[l]).astype(BF16), "lru_bx": row(lru_b_x),
        "lru_lambda": row(lru_lambda),
        "gdn_cw": gdn_conv_w[l], "gdn_alog": _pad_lanes(gdn_a_log[l], 0),
        "gdn_dtb": _pad_lanes(gdn_dt_bias[l], 0), "gdn_nw": row(gdn_norm_w), "gdn_nh": nh,
        "w_out": w_out[l].astype(BF16), "w_gu": ffn_w_gate_up[l].astype(BF16),
        "w_down": ffn_w_down[l].astype(BF16),
    }


def _prompt_layer(x, p, *, tc, rg):
    nb, length, _ = x.shape
    sl_tm, gd = _in_proj(x, p["n_mix_pre"], p["w_in"], n_sl=p["n_sl"], tc=tc, to_time_major=True)
    n_state = p["s5_lre"].shape[1]
    lw = p["lru_lambda"].shape[1]
    z = lambda *s: jnp.zeros(s, F32)
    ysl, xr, xi, hl, cv = _scan(sl_tm.reshape(length * nb, -1), p, z(nb, n_state), z(nb, n_state),
                                z(nb, lw), z(3 * nb, lw), nb=nb, tc=tc, first_pos_is_zero=True)
    yg, s_new, gcv = _gdn_prompt(gd, p, rg=rg, ns=2 if nb % 2 == 0 else 1)
    x = _ffn(x, ysl.reshape(length, nb, -1), yg, p, tc=tc, from_time_major=True)
    lcv = jnp.swapaxes(cv.reshape(3, nb, lw), 0, 1)
    return x, (xr, xi, hl, lcv, s_new, gcv)


def _sample_layer(x_tm, st, gdn_s_all, gdn_s_acc, layer, p, *, nbk):
    s5_re, s5_im, lru_h, lru_conv, gdn_conv = st
    nb = s5_re.shape[0]
    rows = x_tm.shape[1]
    nt = rows // nb
    sl, gd = _in_proj(x_tm, p["n_mix_pre"], p["w_in"], n_sl=p["n_sl"], tc=rows, to_time_major=False)
    lw = p["lru_lambda"].shape[1]
    cv0 = jnp.swapaxes(lru_conv, 0, 1).reshape(3 * nb, lw)
    ysl, xr, xi, hl, cv = _scan(sl.reshape(rows, -1), p, s5_re.reshape(nb, -1), s5_im.reshape(nb, -1),
                                lru_h, cv0, nb=nb, tc=nt, first_pos_is_zero=(PAST_LEN == 0))
    yg, s_acc, gcv = _gdn_sample(gd.reshape(nt, nb, -1), jnp.swapaxes(gdn_conv, 0, 1), gdn_s_all, gdn_s_acc,
                                 layer, p, nbk=nbk)
    x_tm = _ffn(x_tm, ysl.reshape(1, rows, -1), yg.reshape(1, rows, -1), p, tc=rows, from_time_major=False)
    lcv = jnp.swapaxes(cv.reshape(3, nb, lw), 0, 1)
    return x_tm, (xr, xi, hl, lcv, jnp.swapaxes(gcv, 0, 1)), s_acc


def kernel(x_prompt, x_sample, state_s5_re, state_s5_im, state_lru_h, state_lru_conv, state_gdn_S, state_gdn_conv, norm_mix_pre, norm_mix_post, norm_ffn_pre, norm_ffn_post, w_in, s5_lambda_re, s5_lambda_im, s5_log_dt, s5_b_re, s5_b_im, s5_c_re, s5_c_im, s5_d, s5_w_glu, s5_b_glu, lru_conv_w, lru_conv_b, lru_w_a, lru_b_a, lru_w_x, lru_b_x, lru_lambda, gdn_conv_w, gdn_a_log, gdn_dt_bias, gdn_norm_w, w_out, ffn_w_gate_up, ffn_w_down):
    depth = w_in.shape[0]
    bp, lp, d = x_prompt.shape
    bs, ls, _ = x_sample.shape
    g, pst = s5_lambda_re.shape[1:]
    tc = min(64, lp)
    rg = min(4 * GDN_CHUNK, lp)
    y_p = x_prompt
    y_s = jnp.swapaxes(x_sample, 0, 1).reshape(1, ls * bs, d)
    new_p = [[] for _ in range(6)]
    new_s = [[] for _ in range(5)]
    s_gdn_s = None
    for l in range(depth):
        p = _layer_params(l, norm_mix_pre, norm_mix_post, norm_ffn_pre, norm_ffn_post, w_in, s5_lambda_re,
                          s5_lambda_im, s5_log_dt, s5_b_re, s5_b_im, s5_c_re, s5_c_im, s5_d, s5_w_glu,
                          s5_b_glu, lru_conv_w, lru_conv_b, lru_w_a, lru_b_a, lru_w_x, lru_b_x, lru_lambda,
                          gdn_conv_w, gdn_a_log, gdn_dt_bias, gdn_norm_w, w_out, ffn_w_gate_up, ffn_w_down)
        y_p, sp = _prompt_layer(y_p, p, tc=tc, rg=rg)
        st_s = (state_s5_re[l], state_s5_im[l], state_lru_h[l], state_lru_conv[l], state_gdn_conv[l])
        y_s, ss, s_gdn_s = _sample_layer(y_s, st_s, state_gdn_S, s_gdn_s, l, p, nbk=min(16, bs))
        for j in range(6):
            new_p[j].append(sp[j])
        for j in range(5):
            new_s[j].append(ss[j])
    y_s = jnp.swapaxes(y_s.reshape(ls, bs, d), 0, 1)
    outs_p = [jnp.stack(t, axis=0) for t in new_p]
    outs_s = [jnp.stack(t, axis=0) for t in new_s]
    outs_s.insert(4, s_gdn_s)
    outs_p[0] = outs_p[0].reshape(depth, bp, g, pst)
    outs_p[1] = outs_p[1].reshape(depth, bp, g, pst)
    outs_s[0] = outs_s[0].reshape(depth, bs, g, pst)
    outs_s[1] = outs_s[1].reshape(depth, bs, g, pst)
    return (y_p, y_s, *outs_p, *outs_s)
```

```python
import functools

import jax
import jax.numpy as jnp
from jax import lax
from jax.experimental import pallas as pl
from jax.experimental.pallas import tpu as pltpu

F32 = jnp.float32
BF16 = jnp.bfloat16
HIGHEST = lax.Precision.HIGHEST

NORM_EPS = 1e-6
LRU_C = 8.0
GDN_CHUNK = 64
PAST_LEN = 16384
LANES = 128
SUBLANES = 8
VMEM_LIMIT = 56 * 1024 * 1024


def _rms(x, w):
    return x * lax.rsqrt(jnp.mean(x * x, axis=-1, keepdims=True) + NORM_EPS) * w


def _sigmoid(x):
    return jax.nn.sigmoid(x)


def _softplus(x):
    return jnp.maximum(x, 0.0) + jnp.log1p(jnp.exp(-jnp.abs(x)))


def _mm(a, b):
    return jnp.dot(a.astype(BF16), b.astype(BF16), preferred_element_type=F32)


def _mm_nt(a, b):
    return lax.dot_general(a.astype(BF16), b.astype(BF16), (((1,), (1,)), ((), ())),
                           preferred_element_type=F32)


def _mm_tn(a, b):
    return lax.dot_general(a.astype(BF16), b.astype(BF16), (((0,), (0,)), ((), ())),
                           preferred_element_type=F32)


def _const_spec(shape):
    n = len(shape)
    return pl.BlockSpec(shape, lambda *_: (0,) * n, pipeline_mode=pl.Buffered(1))


def _layer_spec(arr, layer):
    n = arr.ndim - 1
    return pl.BlockSpec((None,) + arr.shape[1:], lambda *_: (layer,) + (0,) * n,
                        pipeline_mode=pl.Buffered(1))


def _l2n_heads(x, nh, hd, mul):
    def one(h):
        xh = x[:, h * hd:(h + 1) * hd]
        return xh * (lax.rsqrt(jnp.sum(xh * xh, axis=-1, keepdims=True) + NORM_EPS) * mul)
    return jnp.concatenate([one(h) for h in range(nh)], axis=-1)


def _in_proj_kernel(x_ref, nw_ref, w_ref, sl_ref, gd_ref, *, n_sl, to_time_major):
    nb, tc, d = x_ref.shape
    x = x_ref[...].reshape(nb * tc, d)
    h = _rms(x, nw_ref[...])
    p = _mm(h, w_ref[...])
    sl = p[:, :n_sl].reshape(nb, tc, n_sl)
    if to_time_major:
        sl = jnp.swapaxes(sl, 0, 1)
    sl_ref[...] = sl
    gd_ref[...] = p[:, n_sl:].reshape(nb, tc, p.shape[1] - n_sl)


def _in_proj(x3, p, layer, *, tc, to_time_major):
    nb, length, d = x3.shape
    nw, w, n_sl = p["n_mix_pre"], p["w_in"], p["n_sl"]
    n_tot = w.shape[-1]
    n_gd = n_tot - n_sl
    if to_time_major:
        sl_shape, sl_spec = (length, nb, n_sl), pl.BlockSpec((tc, nb, n_sl), lambda i: (i, 0, 0))
    else:
        sl_shape, sl_spec = (nb, length, n_sl), pl.BlockSpec((nb, tc, n_sl), lambda i: (0, i, 0))
    return pl.pallas_call(
        functools.partial(_in_proj_kernel, n_sl=n_sl, to_time_major=to_time_major),
        grid=(length // tc,),
        in_specs=[pl.BlockSpec((nb, tc, d), lambda i: (0, i, 0)),
                  _layer_spec(nw, layer), _layer_spec(w, layer)],
        out_specs=[sl_spec, pl.BlockSpec((nb, tc, n_gd), lambda i: (0, i, 0))],
        out_shape=[jax.ShapeDtypeStruct(sl_shape, F32), jax.ShapeDtypeStruct((nb, length, n_gd), F32)],
        compiler_params=pltpu.CompilerParams(dimension_semantics=("arbitrary",),
                                             vmem_limit_bytes=VMEM_LIMIT),
        name="in_proj",
    )(x3, nw, w)


def _scan_kernel(*refs, nb, tc, s5w, lw, first_pos_is_zero, zero_state):
    (sl_ref, lre_ref, lim_ref, ldt_ref, bre_ref, bim_ref, cre_ref, cim_ref, d_ref,
     wglu_ref, bglu_ref, cw_ref, cb_ref, wa_ref, ba_ref, wx_ref, bx_ref, lam_ref) = refs[:18]
    n_in = 18 if zero_state else 22
    (y_ref, xr_out, xi_out, h_out, cv_out,
     ar_s, ai_s, bbr_s, bbi_s, crb_s, cib_s, xr_c, xi_c, h_c, xr_s, xi_s, xpad_s, a_s, b_s) = refs[n_in:]
    i = pl.program_id(0)
    last = pl.num_programs(0) - 1
    rows = nb * tc

    @pl.when(i == 0)
    def _init():
        lr = jnp.minimum(lre_ref[...], -1e-4)
        li = lim_ref[...]
        dt = jnp.exp(ldt_ref[...])
        mag = jnp.exp(lr * dt)
        ar = mag * jnp.cos(li * dt)
        ai = mag * jnp.sin(li * dt)
        den = lr * lr + li * li
        fr = ((ar - 1.0) * lr + ai * li) / den
        fi = (ai * lr - (ar - 1.0) * li) / den
        ar_s[...] = ar
        ai_s[...] = ai
        bre = bre_ref[...]
        bim = bim_ref[...]
        bbr_s[...] = (fr * bre - fi * bim).astype(BF16)
        bbi_s[...] = (fr * bim + fi * bre).astype(BF16)
        crb_s[...] = cre_ref[...].astype(BF16)
        cib_s[...] = cim_ref[...].astype(BF16)
        if zero_state:
            xr_c[...] = jnp.zeros_like(xr_c)
            xi_c[...] = jnp.zeros_like(xi_c)
            h_c[...] = jnp.zeros_like(h_c)
            xpad_s[0:3 * nb, :] = jnp.zeros((3 * nb, lw), F32)
        else:
            xr0_ref, xi0_ref, h0_ref, cv0_ref = refs[18:22]
            xr_c[...] = xr0_ref[...]
            xi_c[...] = xi0_ref[...]
            h_c[...] = h0_ref[...]
            xpad_s[0:3 * nb, :] = cv0_ref[...]

    u = sl_ref[:, 0:s5w]
    xb = sl_ref[:, s5w:s5w + lw]
    gate = sl_ref[:, s5w + lw:s5w + 2 * lw]

    ub = u.astype(BF16)
    xr_s[...] = jnp.dot(ub, bbr_s[...], preferred_element_type=F32)
    xi_s[...] = jnp.dot(ub, bbi_s[...], preferred_element_type=F32)

    xpad_s[3 * nb:3 * nb + rows, :] = xb
    cw = cw_ref[...]
    xc = (xpad_s[0:rows, :] * cw[0:1] + xpad_s[nb:nb + rows, :] * cw[1:2]
          + xpad_s[2 * nb:2 * nb + rows, :] * cw[2:3] + xb * cw[3:4]) + cb_ref[...]
    new_cv = xpad_s[rows:rows + 3 * nb, :]
    xpad_s[0:3 * nb, :] = new_cv
    r = _sigmoid(_mm(xc, wa_ref[...]) + ba_ref[...])
    gi = _sigmoid(_mm(xc, wx_ref[...]) + bx_ref[...])
    log_a = -LRU_C * r * _softplus(-lam_ref[...])
    a = jnp.exp(log_a)
    mult = jnp.sqrt(-jnp.tanh(log_a) * (a * a + 1.0))
    if first_pos_is_zero:
        rid = lax.broadcasted_iota(jnp.int32, (rows, 1), 0)
        mult = jnp.where(jnp.logical_and(rid < nb, i == 0), 1.0, mult)
    a_s[...] = a
    b_s[...] = mult * gi * xc

    ar = jnp.broadcast_to(ar_s[...], (nb, ar_s.shape[1]))
    ai = jnp.broadcast_to(ai_s[...], (nb, ai_s.shape[1]))

    def step(t, carry):
        xr, xi, h = carry
        sl_t = pl.ds(pl.multiple_of(t * nb, nb), nb)
        nxr = ar * xr - ai * xi + xr_s[sl_t, :]
        nxi = ar * xi + ai * xr + xi_s[sl_t, :]
        nh = a_s[sl_t, :] * h + b_s[sl_t, :]
        xr_s[sl_t, :] = nxr
        xi_s[sl_t, :] = nxi
        b_s[sl_t, :] = nh
        return nxr, nxi, nh

    xr, xi, h = lax.fori_loop(0, tc, step, (xr_c[...], xi_c[...], h_c[...]),
                              unroll=(True if tc <= 8 else 8))
    xr_c[...] = xr
    xi_c[...] = xi
    h_c[...] = h

    y = (jnp.dot(xr_s[...].astype(BF16), crb_s[...], preferred_element_type=F32)
         - jnp.dot(xi_s[...].astype(BF16), cib_s[...], preferred_element_type=F32))
    y = jax.nn.gelu(y + d_ref[...] * u)
    y = y * _sigmoid(_mm(y, wglu_ref[...]) + bglu_ref[...])
    y_ref[:, 0:s5w] = y
    y_ref[:, s5w:s5w + lw] = b_s[...] * jax.nn.gelu(gate)

    @pl.when(i == last)
    def _fin():
        xr_out[...] = xr
        xi_out[...] = xi
        h_out[...] = h
        cv_out[...] = new_cv


def _scan(sl2, p, layer, states, *, nb, tc, first_pos_is_zero):
    rows_total, n_sl = sl2.shape
    s5w = p["s5_d"].shape[-1]
    lw = p["lru_lambda"].shape[-1]
    n_state = p["s5_lre"].shape[-1]
    rows = nb * tc
    params = [p["s5_lre"], p["s5_lim"], p["s5_ldt"], p["s5_bre"], p["s5_bim"], p["s5_cre"], p["s5_cim"],
              p["s5_d"], p["s5_wglu"], p["s5_bglu"], p["lru_cw"], p["lru_cb"], p["lru_wa"], p["lru_ba"],
              p["lru_wx"], p["lru_bx"], p["lru_lambda"]]
    zero_state = states is None
    states = [] if zero_state else list(states)
    return pl.pallas_call(
        functools.partial(_scan_kernel, nb=nb, tc=tc, s5w=s5w, lw=lw, first_pos_is_zero=first_pos_is_zero,
                          zero_state=zero_state),
        grid=(rows_total // rows,),
        in_specs=[pl.BlockSpec((rows, n_sl), lambda i: (i, 0))]
        + [_layer_spec(a, layer) for a in params] + [_const_spec(a.shape) for a in states],
        out_specs=[pl.BlockSpec((rows, s5w + lw), lambda i: (i, 0)),
                   pl.BlockSpec((nb, n_state), lambda i: (0, 0)),
                   pl.BlockSpec((nb, n_state), lambda i: (0, 0)),
                   pl.BlockSpec((nb, lw), lambda i: (0, 0)),
                   pl.BlockSpec((3 * nb, lw), lambda i: (0, 0))],
        out_shape=[jax.ShapeDtypeStruct((rows_total, s5w + lw), F32),
                   jax.ShapeDtypeStruct((nb, n_state), F32),
                   jax.ShapeDtypeStruct((nb, n_state), F32),
                   jax.ShapeDtypeStruct((nb, lw), F32),
                   jax.ShapeDtypeStruct((3 * nb, lw), F32)],
        scratch_shapes=[pltpu.VMEM((1, n_state), F32), pltpu.VMEM((1, n_state), F32),
                        pltpu.VMEM((s5w, n_state), BF16), pltpu.VMEM((s5w, n_state), BF16),
                        pltpu.VMEM((n_state, s5w), BF16), pltpu.VMEM((n_state, s5w), BF16),
                        pltpu.VMEM((nb, n_state), F32), pltpu.VMEM((nb, n_state), F32),
                        pltpu.VMEM((nb, lw), F32),
                        pltpu.VMEM((rows, n_state), F32), pltpu.VMEM((rows, n_state), F32),
                        pltpu.VMEM((rows + 3 * nb, lw), F32),
                        pltpu.VMEM((rows, lw), F32), pltpu.VMEM((rows, lw), F32)],
        compiler_params=pltpu.CompilerParams(dimension_semantics=("arbitrary",),
                                             vmem_limit_bytes=VMEM_LIMIT),
        name="scan",
    )(sl2, *params, *states)


def _gdn_prompt_kernel(gd_ref, cw_ref, alog_ref, dtb_ref, nw_ref, y_ref, s_out, cv_out,
                       s_s, xp_s, *, ns, rg, nh, hd):
    i = pl.program_id(1)
    last = pl.num_programs(1) - 1
    w3 = 3 * nh * hd
    wz = nh * hd
    c = GDN_CHUNK
    nc = rg // c
    wc = nh * c
    rt = ns * rg

    @pl.when(i == 0)
    def _init():
        s_s[...] = jnp.zeros_like(s_s)
        xp_s[...] = jnp.zeros_like(xp_s)

    cw = cw_ref[...]
    xcs = []
    for s in range(ns):
        qkv = gd_ref[s, :, 0:w3]
        xe = jnp.concatenate([xp_s[s], qkv], axis=0)
        acc = qkv * cw[3:4]
        for j in range(1, 4):
            acc = acc + pltpu.roll(xe, j, 0)[SUBLANES:, :] * cw[3 - j:4 - j]
        xp_s[s] = qkv[rg - SUBLANES:, :]
        xcs.append(acc)
    xc = jnp.concatenate(xcs, axis=0)
    xc = xc * _sigmoid(xc)

    ab = jnp.concatenate([gd_ref[s, :, w3 + wz:w3 + wz + LANES] for s in range(ns)], axis=0)
    g_all = -jnp.exp(alog_ref[...]) * _softplus(ab + dtb_ref[...])
    beta_all = _sigmoid(ab)

    def per_head(fn):
        return jnp.concatenate([fn(h) for h in range(nh)], axis=-1)

    q_all = _l2n_heads(xc[:, 0:wz], nh, hd, hd ** -0.5)
    k_all = _l2n_heads(xc[:, wz:2 * wz], nh, hd, 1.0)
    v_all = xc[:, 2 * wz:3 * wz]
    beta_w = per_head(lambda h: jnp.broadcast_to(beta_all[:, nh + h:nh + h + 1], (rt, hd)))
    kb_all = k_all * beta_w
    vb_all = v_all * beta_w

    ri = lax.broadcasted_iota(jnp.int32, (c, wc), 0)
    cj = lax.broadcasted_iota(jnp.int32, (c, wc), 1) % c
    incl = ri >= cj
    strict = ri > cj
    eye = (ri == cj).astype(F32)
    r1 = lax.broadcasted_iota(jnp.int32, (c, c), 0)
    c1 = lax.broadcasted_iota(jnp.int32, (c, c), 1)
    ltri = (r1 >= c1).astype(F32)
    bd_cc = (lax.broadcasted_iota(jnp.int32, (wc, wc), 0) // c
             == lax.broadcasted_iota(jnp.int32, (wc, wc), 1) // c)
    bd_cd = (lax.broadcasted_iota(jnp.int32, (wc, wz), 0) // c
             == lax.broadcasted_iota(jnp.int32, (wc, wz), 1) // hd)

    def bdiag(x, mask):
        return jnp.where(mask, jnp.concatenate([x] * nh, axis=0), 0.0).astype(BF16)

    nc_all = ns * nc
    rows = [slice(cc * c, (cc + 1) * c) for cc in range(nc_all)]
    gcum = [jnp.dot(ltri, g_all[r], precision=HIGHEST, preferred_element_type=F32) for r in rows]
    cgc = [per_head(lambda h, x=x: jnp.broadcast_to(x[:, h:h + 1], (c, hd))) for x in gcum]
    gcol = [per_head(lambda h, x=x: jnp.broadcast_to(x[:, h:h + 1], (c, c))) for x in gcum]
    grow = [jnp.concatenate([xt[h:h + 1, :] for h in range(nh)], axis=1)
            for xt in [x.T for x in gcum]]
    every = range(nc_all)
    decay = [jnp.exp(jnp.where(incl, gcol[n] - grow[n], -jnp.inf)) for n in every]
    eg = [jnp.exp(x) for x in cgc]
    g_last = [x[c - 1:c, :] for x in cgc]
    kk = [_mm_nt(jnp.concatenate([kb_all[r], q_all[r]], axis=0), bdiag(k_all[r], bd_cd)) for r in rows]
    attn = [kk[n][c:2 * c] * decay[n] for n in every]
    rj = [-jnp.where(strict, kk[n][0:c] * decay[n], 0.0) for n in every]
    sj = [eye + rj[n] for n in every]
    rj = [_mm(rj[n], bdiag(rj[n], bd_cc)) for n in every]
    for _ in range(4):
        rs = [_mm(jnp.concatenate([rj[n], sj[n]], axis=0), bdiag(rj[n], bd_cc)) for n in every]
        rj = [x[0:c] for x in rs]
        sj = [sj[n] + rs[n][c:2 * c] for n in every]
    sj = [sj[n] + _mm(sj[n], bdiag(rj[n], bd_cc)) for n in every]
    w = [_mm(sj[n], jnp.concatenate([bdiag(vb_all[rows[n]], bd_cd),
                                     bdiag(kb_all[rows[n]] * eg[n], bd_cd)], axis=1)) for n in every]
    q_dec = [q_all[rows[n]] * eg[n] for n in every]
    k_dec = [k_all[rows[n]] * jnp.exp(g_last[n] - cgc[n]) for n in every]
    s_dec = [jnp.exp(x) for x in g_last]

    s_cur = [[s_s[s, h] for h in range(nh)] for s in range(ns)]
    hs = [slice(h * hd, (h + 1) * hd) for h in range(nh)]
    for m in range(nc):
        for s in range(ns):
            n = s * nc + m
            st = s_cur[s]
            ws = [_mm(jnp.concatenate([w[n][:, wz + h * hd:wz + (h + 1) * hd], q_dec[n][:, hs[h]]], axis=0),
                      st[h]) for h in range(nh)]
            v_new = jnp.concatenate([w[n][:, hs[h]] - ws[h][0:c] for h in range(nh)], axis=-1)
            o = jnp.concatenate([x[c:2 * c] for x in ws], axis=-1) + _mm(attn[n], bdiag(v_new, bd_cd))
            s_cur[s] = [st[h] * s_dec[n][:, hs[h]] + _mm_tn(k_dec[n][:, hs[h]], v_new[:, hs[h]])
                        for h in range(nh)]
            for h in range(nh):
                z = gd_ref[s, m * c:(m + 1) * c, w3 + h * hd:w3 + (h + 1) * hd]
                y_ref[s, m * c:(m + 1) * c, hs[h]] = _rms(o[:, hs[h]], nw_ref[...]) * (z * _sigmoid(z))
    for s in range(ns):
        for h in range(nh):
            s_s[s, h] = s_cur[s][h]

    @pl.when(i == last)
    def _fin():
        for s in range(ns):
            for h in range(nh):
                s_out[s, h] = s_cur[s][h]
            cv_out[s] = xp_s[s, SUBLANES - 3:SUBLANES, :]


def _gdn_prompt(gd3, p, layer, *, rg, ns):
    nb, length, n_gd = gd3.shape
    nh = p["gdn_nh"]
    hd = p["gdn_nw"].shape[-1]
    w3 = 3 * nh * hd
    consts = [p["gdn_cw"], p["gdn_alog"], p["gdn_dtb"], p["gdn_nw"]]
    return pl.pallas_call(
        functools.partial(_gdn_prompt_kernel, ns=ns, rg=rg, nh=nh, hd=hd),
        grid=(nb // ns, length // rg),
        in_specs=[pl.BlockSpec((ns, rg, n_gd), lambda b, i: (b, i, 0))]
        + [_layer_spec(a, layer) for a in consts],
        out_specs=[pl.BlockSpec((ns, rg, nh * hd), lambda b, i: (b, i, 0)),
                   pl.BlockSpec((ns, nh, hd, hd), lambda b, i: (b, 0, 0, 0)),
                   pl.BlockSpec((ns, 3, w3), lambda b, i: (b, 0, 0))],
        out_shape=[jax.ShapeDtypeStruct((nb, length, nh * hd), F32),
                   jax.ShapeDtypeStruct((nb, nh, hd, hd), F32),
                   jax.ShapeDtypeStruct((nb, 3, w3), F32)],
        scratch_shapes=[pltpu.VMEM((ns, nh, hd, hd), F32), pltpu.VMEM((ns, SUBLANES, w3), F32)],
        compiler_params=pltpu.CompilerParams(dimension_semantics=("arbitrary", "arbitrary"),
                                             vmem_limit_bytes=VMEM_LIMIT),
        name="gdn_prompt",
    )(gd3, *consts)


def _gdn_sample_kernel(*refs, nt, nbk, nh, hd, layer, has_acc):
    gd_ref, cv0_ref, s0_ref, cw_ref, alog_ref, dtb_ref, nw_ref = refs[:7]
    y_ref, s_out, cv_out, wk_s, kd_s, wv_s, ws_s = refs[7 + int(has_acc):]
    if not has_acc:
        for other in range(s_out.shape[0]):
            if other != layer:
                s_out[other] = jnp.zeros(s_out.shape[1:], F32)
        s_out = s_out.at[layer]
    w3 = 3 * nh * hd
    wz = nh * hd
    cw = cw_ref[...]
    scale = hd ** -0.5
    xs = [cv0_ref[j] for j in range(3)] + [gd_ref[t, :, 0:w3] for t in range(nt)]
    for j in range(3):
        cv_out[j] = xs[nt + j]
    q, k, v, gc, beta = [], [], [], [], []
    for t in range(nt):
        xc = xs[t] * cw[0:1] + xs[t + 1] * cw[1:2] + xs[t + 2] * cw[2:3] + xs[t + 3] * cw[3:4]
        xc = xc * _sigmoid(xc)
        ab = gd_ref[t, :, w3 + wz:w3 + wz + LANES]
        g_t = -jnp.exp(alog_ref[...]) * _softplus(ab + dtb_ref[...])
        gc.append(g_t if t == 0 else gc[-1] + g_t)
        beta.append(_sigmoid(ab))
        qn = _l2n_heads(xc[:, 0:wz], nh, hd, scale)
        kn = _l2n_heads(xc[:, wz:2 * wz], nh, hd, 1.0)
        q.append([qn[:, h * hd:(h + 1) * hd] for h in range(nh)])
        k.append([kn[:, h * hd:(h + 1) * hd] for h in range(nh)])
        v.append([xc[:, 2 * wz + h * hd:2 * wz + (h + 1) * hd] for h in range(nh)])

    def dot(a, b):
        return jnp.sum(a * b, axis=-1, keepdims=True)

    wk_rows = [[None] * nh for _ in range(nt)]
    qd_rows = [[None] * nh for _ in range(nt)]
    kd_rows = [[None] * nh for _ in range(nt)]
    wv_rows = [[None] * nh for _ in range(nt)]
    sd_rows = [None] * nh
    attn = [None] * nh
    for h in range(nh):
        gch = [gc[t][:, h:h + 1] for t in range(nt)]
        bh = [beta[t][:, nh + h:nh + h + 1] for t in range(nt)]
        e = [jnp.exp(x) for x in gch]
        kb = [k[t][h] * bh[t] for t in range(nt)]
        vb = [v[t][h] * bh[t] for t in range(nt)]
        kbe = [kb[t] * e[t] for t in range(nt)]
        dec = [[jnp.exp(gch[i] - gch[j]) for j in range(i)] for i in range(nt)]
        m = [[dot(kb[i], k[j][h]) * dec[i][j] for j in range(i)] for i in range(nt)]
        attn[h] = [[dot(q[i][h], k[j][h]) * (dec[i][j] if j < i else 1.0) for j in range(i + 1)]
                   for i in range(nt)]
        tinv = [[None] * nt for _ in range(nt)]
        for j in range(nt):
            for i in range(j + 1, nt):
                acc = m[i][j]
                for l in range(j + 1, i):
                    acc = acc + m[i][l] * tinv[l][j]
                tinv[i][j] = -acc
        for i in range(nt):
            wv_i, wk_i = vb[i], kbe[i]
            for j in range(i):
                wv_i = wv_i + tinv[i][j] * vb[j]
                wk_i = wk_i + tinv[i][j] * kbe[j]
            wv_rows[i][h] = wv_i
            wk_rows[i][h] = wk_i
            qd_rows[i][h] = q[i][h] * e[i]
            kd_rows[i][h] = k[i][h] * jnp.exp(gch[nt - 1] - gch[i])
        sd_rows[h] = jnp.broadcast_to(e[nt - 1], (nbk, hd))

    cat = lambda parts: jnp.concatenate(parts, axis=-1)
    to_seq = lambda slabs: jnp.swapaxes(jnp.stack(slabs, axis=0), 0, 1)
    zero = jnp.zeros((nbk, wz), F32)
    wk_s[...] = to_seq([cat(wk_rows[i]) for i in range(nt)] + [cat(qd_rows[i]) for i in range(nt)])
    kd_s[...] = to_seq([cat(kd_rows[i]) for i in range(nt)] + [zero] * nt)
    wv_s[...] = to_seq([cat(wv_rows[i]) for i in range(nt)] + [cat(sd_rows)] * nt)

    def per_seq(b, _):
        twk_all = wk_s[b]
        tkd_all = kd_s[b]
        twv_all = wv_s[b]
        for h in range(nh):
            cols = slice(h * hd, (h + 1) * hd)
            twk, tkd, twv = twk_all[:, cols], tkd_all[:, cols], twv_all[:, cols]
            a = _mm_tn(tkd, jnp.concatenate([twv, twk], axis=1))
            s = s0_ref[b, h]
            r = _mm(jnp.concatenate([twk, a[:, hd:]], axis=0), s)
            s_out[b, h] = s * twv[nt:nt + 1, :] + a[:, 0:hd] - r[2 * nt:, :]
            ws_s[b, :, cols] = r[0:2 * nt, :]
        return 0

    lax.fori_loop(0, nbk, per_seq, 0, unroll=4)

    ws_tm = jnp.swapaxes(ws_s[...], 0, 1)
    for h in range(nh):
        cols = slice(h * hd, (h + 1) * hd)
        v_new = [wv_rows[j][h] - ws_tm[j][:, cols] for j in range(nt)]
        for i in range(nt):
            o = ws_tm[nt + i][:, cols]
            for j in range(i + 1):
                o = o + attn[h][i][j] * v_new[j]
            z = gd_ref[i, :, w3 + h * hd:w3 + (h + 1) * hd]
            y_ref[i, :, cols] = _rms(o, nw_ref[...]) * (z * _sigmoid(z))


def _gdn_sample(gd3, cv0, s_all, s_acc, layer, p, *, nbk):
    nt, nb, n_gd = gd3.shape
    nh = p["gdn_nh"]
    hd = p["gdn_nw"].shape[-1]
    w3 = 3 * nh * hd
    wz = nh * hd
    s_spec = pl.BlockSpec((None, nbk, nh, hd, hd), lambda i: (layer, i, 0, 0, 0))
    consts = [p["gdn_cw"], p["gdn_alog"], p["gdn_dtb"], p["gdn_nw"]]
    args = [gd3, cv0, s_all] + consts
    in_specs = [pl.BlockSpec((nt, nbk, n_gd), lambda i: (0, i, 0)),
                pl.BlockSpec((3, nbk, w3), lambda i: (0, i, 0)),
                s_spec] + [_layer_spec(a, layer) for a in consts]
    aliases = {}
    s_out_spec = s_spec
    if s_acc is not None:
        aliases = {len(args): 1}
        args.append(s_acc)
        in_specs.append(pl.BlockSpec(memory_space=pl.ANY))
    else:
        s_out_spec = pl.BlockSpec((s_all.shape[0], nbk, nh, hd, hd), lambda i: (0, i, 0, 0, 0))
    return pl.pallas_call(
        functools.partial(_gdn_sample_kernel, nt=nt, nbk=nbk, nh=nh, hd=hd, layer=layer,
                          has_acc=s_acc is not None),
        grid=(nb // nbk,),
        in_specs=in_specs,
        out_specs=[pl.BlockSpec((nt, nbk, wz), lambda i: (0, i, 0)),
                   s_out_spec,
                   pl.BlockSpec((3, nbk, w3), lambda i: (0, i, 0))],
        out_shape=[jax.ShapeDtypeStruct((nt, nb, wz), F32),
                   jax.ShapeDtypeStruct(s_all.shape, F32),
                   jax.ShapeDtypeStruct((3, nb, w3), F32)],
        scratch_shapes=[pltpu.VMEM((nbk, 2 * nt, wz), F32) for _ in range(4)],
        input_output_aliases=aliases,
        compiler_params=pltpu.CompilerParams(dimension_semantics=("arbitrary",),
                                             vmem_limit_bytes=VMEM_LIMIT),
        name="gdn_sample",
    )(*args)


def _ffn_kernel(x_ref, ysl_ref, yg_ref, wo_ref, nmp_ref, nfp_ref, nfq_ref, wgu_ref, wd_ref, o_ref,
                *, from_time_major):
    nb, tc, d = x_ref.shape
    rows = nb * tc
    x = x_ref[...].reshape(rows, d)
    ysl = ysl_ref[...]
    if from_time_major:
        ysl = jnp.swapaxes(ysl, 0, 1)
    ysl = ysl.reshape(rows, ysl.shape[-1])
    yg = yg_ref[...].reshape(rows, yg_ref.shape[-1])
    mix = jnp.concatenate([ysl, yg], axis=-1)
    x = x + _rms(_mm(mix, wo_ref[...]), nmp_ref[...])
    h = _rms(x, nfp_ref[...])
    gu = _mm(h, wgu_ref[...])
    hid = gu.shape[1] // 2
    gt = gu[:, :hid]
    f = _mm(gt * _sigmoid(gt) * gu[:, hid:], wd_ref[...])
    x = x + _rms(f, nfq_ref[...])
    o_ref[...] = x.reshape(nb, tc, d)


def _ffn(x3, ysl3, yg3, p, layer, *, tc, from_time_major):
    nb, length, d = x3.shape
    n1 = ysl3.shape[-1]
    n2 = yg3.shape[-1]
    consts = [p["w_out"], p["n_mix_post"], p["n_ffn_pre"], p["n_ffn_post"], p["w_gu"], p["w_down"]]
    if from_time_major:
        ysl_spec = pl.BlockSpec((tc, nb, n1), lambda i: (i, 0, 0))
    else:
        ysl_spec = pl.BlockSpec((nb, tc, n1), lambda i: (0, i, 0))
    return pl.pallas_call(
        functools.partial(_ffn_kernel, from_time_major=from_time_major),
        grid=(length // tc,),
        in_specs=[pl.BlockSpec((nb, tc, d), lambda i: (0, i, 0)), ysl_spec,
                  pl.BlockSpec((nb, tc, n2), lambda i: (0, i, 0))]
        + [_layer_spec(a, layer) for a in consts],
        out_specs=pl.BlockSpec((nb, tc, d), lambda i: (0, i, 0)),
        out_shape=jax.ShapeDtypeStruct((nb, length, d), F32),
        compiler_params=pltpu.CompilerParams(dimension_semantics=("arbitrary",),
                                             vmem_limit_bytes=VMEM_LIMIT),
        name="ffn",
    )(x3, ysl3, yg3, *consts)


def _block_diag(blocks):
    dep, n, r, c = blocks.shape
    eye = jnp.eye(n, dtype=blocks.dtype)
    return (eye[None, :, None, :, None] * blocks[:, :, :, None, :]).reshape(dep, n * r, n * c)


def _params(norm_mix_pre, norm_mix_post, norm_ffn_pre, norm_ffn_post, w_in, s5_lambda_re,
            s5_lambda_im, s5_log_dt, s5_b_re, s5_b_im, s5_c_re, s5_c_im, s5_d, s5_w_glu, s5_b_glu,
            lru_conv_w, lru_conv_b, lru_w_a, lru_b_a, lru_w_x, lru_b_x, lru_lambda, gdn_conv_w,
            gdn_a_log, gdn_dt_bias, gdn_norm_w, w_out, ffn_w_gate_up, ffn_w_down):
    dep, g, pst, hgrp = s5_b_re.shape
    s5w = g * hgrp
    lw = lru_lambda.shape[1]
    nh = gdn_a_log.shape[1]
    hd = gdn_norm_w.shape[1]
    gw = nh * hd
    row = lambda v: v[:, None, :]
    lanes = lambda v: jnp.pad(v, ((0, 0), (0, LANES - v.shape[1])))[:, None, :]
    n_sl = s5w + 2 * lw
    c_a, c_z = n_sl + 3 * gw, n_sl + 3 * gw + 2 * nh
    wb = w_in.astype(BF16)
    w_r = jnp.concatenate([wb[..., :c_a], wb[..., c_z:c_z + gw], wb[..., c_a:c_z],
                           jnp.zeros(wb.shape[:2] + (LANES - 2 * nh,), BF16)], axis=-1)
    return {
        "n_sl": n_sl,
        "n_mix_pre": row(norm_mix_pre), "n_mix_post": row(norm_mix_post),
        "n_ffn_pre": row(norm_ffn_pre), "n_ffn_post": row(norm_ffn_post),
        "w_in": w_r,
        "s5_lre": s5_lambda_re.reshape(dep, 1, g * pst), "s5_lim": s5_lambda_im.reshape(dep, 1, g * pst),
        "s5_ldt": jnp.repeat(s5_log_dt, pst, axis=1)[:, None, :],
        "s5_bre": _block_diag(jnp.swapaxes(s5_b_re, 2, 3)),
        "s5_bim": _block_diag(jnp.swapaxes(s5_b_im, 2, 3)),
        "s5_cre": _block_diag(jnp.swapaxes(s5_c_re, 2, 3)),
        "s5_cim": _block_diag(jnp.swapaxes(s5_c_im, 2, 3)),
        "s5_d": row(s5_d), "s5_wglu": s5_w_glu.astype(BF16), "s5_bglu": row(s5_b_glu),
        "lru_cw": lru_conv_w, "lru_cb": row(lru_conv_b),
        "lru_wa": _block_diag(lru_w_a).astype(BF16), "lru_ba": row(lru_b_a),
        "lru_wx": _block_diag(lru_w_x).astype(BF16), "lru_bx": row(lru_b_x),
        "lru_lambda": row(lru_lambda),
        "gdn_cw": gdn_conv_w, "gdn_alog": lanes(gdn_a_log),
        "gdn_dtb": lanes(gdn_dt_bias), "gdn_nw": row(gdn_norm_w), "gdn_nh": nh,
        "w_out": w_out.astype(BF16), "w_gu": ffn_w_gate_up.astype(BF16),
        "w_down": ffn_w_down.astype(BF16),
    }


def _prompt_layer(x, p, layer, *, tc, rg):
    nb, length, _ = x.shape
    sl_tm, gd = _in_proj(x, p, layer, tc=tc, to_time_major=True)
    lw = p["lru_lambda"].shape[-1]
    ysl, xr, xi, hl, cv = _scan(sl_tm.reshape(length * nb, -1), p, layer, None, nb=nb, tc=tc,
                                first_pos_is_zero=True)
    yg, s_new, gcv = _gdn_prompt(gd, p, layer, rg=rg, ns=2 if nb % 2 == 0 else 1)
    x = _ffn(x, ysl.reshape(length, nb, -1), yg, p, layer, tc=tc, from_time_major=True)
    lcv = jnp.swapaxes(cv.reshape(3, nb, lw), 0, 1)
    return x, (xr, xi, hl, lcv, s_new, gcv)


def _sample_layer(x_tm, st, gdn_s_all, gdn_s_acc, layer, p, *, nbk):
    s5_re, s5_im, lru_h, lru_conv, gdn_conv = st
    nb = s5_re.shape[0]
    rows = x_tm.shape[1]
    nt = rows // nb
    sl, gd = _in_proj(x_tm, p, layer, tc=rows, to_time_major=False)
    lw = p["lru_lambda"].shape[-1]
    cv0 = jnp.swapaxes(lru_conv, 0, 1).reshape(3 * nb, lw)
    states = [s5_re.reshape(nb, -1), s5_im.reshape(nb, -1), lru_h, cv0]
    ysl, xr, xi, hl, cv = _scan(sl.reshape(rows, -1), p, layer, states, nb=nb, tc=nt,
                                first_pos_is_zero=(PAST_LEN == 0))
    yg, s_acc, gcv = _gdn_sample(gd.reshape(nt, nb, -1), jnp.swapaxes(gdn_conv, 0, 1), gdn_s_all, gdn_s_acc,
                                 layer, p, nbk=nbk)
    x_tm = _ffn(x_tm, ysl.reshape(1, rows, -1), yg.reshape(1, rows, -1), p, layer, tc=rows,
                from_time_major=False)
    lcv = jnp.swapaxes(cv.reshape(3, nb, lw), 0, 1)
    return x_tm, (xr, xi, hl, lcv, jnp.swapaxes(gcv, 0, 1)), s_acc


def kernel(x_prompt, x_sample, state_s5_re, state_s5_im, state_lru_h, state_lru_conv, state_gdn_S, state_gdn_conv, norm_mix_pre, norm_mix_post, norm_ffn_pre, norm_ffn_post, w_in, s5_lambda_re, s5_lambda_im, s5_log_dt, s5_b_re, s5_b_im, s5_c_re, s5_c_im, s5_d, s5_w_glu, s5_b_glu, lru_conv_w, lru_conv_b, lru_w_a, lru_b_a, lru_w_x, lru_b_x, lru_lambda, gdn_conv_w, gdn_a_log, gdn_dt_bias, gdn_norm_w, w_out, ffn_w_gate_up, ffn_w_down):
    depth = w_in.shape[0]
    bp, lp, d = x_prompt.shape
    bs, ls, _ = x_sample.shape
    g, pst = s5_lambda_re.shape[1:]
    tc = min(64, lp)
    rg = min(4 * GDN_CHUNK, lp)
    y_p = x_prompt
    y_s = jnp.swapaxes(x_sample, 0, 1).reshape(1, ls * bs, d)
    new_p = [[] for _ in range(6)]
    new_s = [[] for _ in range(5)]
    s_gdn_s = None
    p = _params(norm_mix_pre, norm_mix_post, norm_ffn_pre, norm_ffn_post, w_in, s5_lambda_re,
                s5_lambda_im, s5_log_dt, s5_b_re, s5_b_im, s5_c_re, s5_c_im, s5_d, s5_w_glu,
                s5_b_glu, lru_conv_w, lru_conv_b, lru_w_a, lru_b_a, lru_w_x, lru_b_x, lru_lambda,
                gdn_conv_w, gdn_a_log, gdn_dt_bias, gdn_norm_w, w_out, ffn_w_gate_up, ffn_w_down)
    for l in range(depth):
        y_p, sp = _prompt_layer(y_p, p, l, tc=tc, rg=rg)
        st_s = (state_s5_re[l], state_s5_im[l], state_lru_h[l], state_lru_conv[l], state_gdn_conv[l])
        y_s, ss, s_gdn_s = _sample_layer(y_s, st_s, state_gdn_S, s_gdn_s, l, p, nbk=min(16, bs))
        for j in range(6):
            new_p[j].append(sp[j])
        for j in range(5):
            new_s[j].append(ss[j])
    y_s = jnp.swapaxes(y_s.reshape(ls, bs, d), 0, 1)
    outs_p = [jnp.stack(t, axis=0) for t in new_p]
    outs_s = [jnp.stack(t, axis=0) for t in new_s]
    outs_s.insert(4, s_gdn_s)
    outs_p[0] = outs_p[0].reshape(depth, bp, g, pst)
    outs_p[1] = outs_p[1].reshape(depth, bp, g, pst)
    outs_s[0] = outs_s[0].reshape(depth, bs, g, pst)
    outs_s[1] = outs_s[1].reshape(depth, bs, g, pst)
    return (y_p, y_s, *outs_p, *outs_s)
```

```python
import functools

import jax
import jax.numpy as jnp
from jax import lax
from jax.experimental import pallas as pl
from jax.experimental.pallas import tpu as pltpu

F32 = jnp.float32
BF16 = jnp.bfloat16
HIGHEST = lax.Precision.HIGHEST

NORM_EPS = 1e-6
LRU_C = 8.0
GDN_CHUNK = 64
PAST_LEN = 16384
LANES = 128
SUBLANES = 8
VMEM_LIMIT = 56 * 1024 * 1024


def _rms(x, w):
    return x * lax.rsqrt(jnp.mean(x * x, axis=-1, keepdims=True) + NORM_EPS) * w


def _sigmoid(x):
    return jax.nn.sigmoid(x)


def _softplus(x):
    return jnp.maximum(x, 0.0) + jnp.log1p(jnp.exp(-jnp.abs(x)))


def _mm(a, b):
    return jnp.dot(a.astype(BF16), b.astype(BF16), preferred_element_type=F32)


def _mm_nt(a, b):
    return lax.dot_general(a.astype(BF16), b.astype(BF16), (((1,), (1,)), ((), ())),
                           preferred_element_type=F32)


def _mm_tn(a, b):
    return lax.dot_general(a.astype(BF16), b.astype(BF16), (((0,), (0,)), ((), ())),
                           preferred_element_type=F32)


def _const_spec(shape):
    n = len(shape)
    return pl.BlockSpec(shape, lambda *_: (0,) * n, pipeline_mode=pl.Buffered(1))


def _layer_spec(arr, layer):
    n = arr.ndim - 1
    return pl.BlockSpec((None,) + arr.shape[1:], lambda *_: (layer,) + (0,) * n,
                        pipeline_mode=pl.Buffered(1))


def _l2n_heads(x, nh, hd, mul):
    def one(h):
        xh = x[:, h * hd:(h + 1) * hd]
        return xh * (lax.rsqrt(jnp.sum(xh * xh, axis=-1, keepdims=True) + NORM_EPS) * mul)
    return jnp.concatenate([one(h) for h in range(nh)], axis=-1)


def _in_proj_kernel(x_ref, nw_ref, w_ref, sl_ref, gd_ref, *, n_sl, w3, n_ab, to_time_major):
    nb, tc, d = x_ref.shape
    rows = nb * tc
    x = x_ref[...].reshape(rows, d)
    h = _rms(x, nw_ref[...])
    p = _mm(h, w_ref[...])
    sl = p[:, :n_sl].reshape(nb, tc, n_sl)
    if to_time_major:
        sl = jnp.swapaxes(sl, 0, 1)
    sl_ref[...] = sl
    c_ab = n_sl + w3
    c_z = c_ab + n_ab
    wz = p.shape[1] - c_z
    gd_ref[:, :, 0:w3] = p[:, n_sl:c_ab].reshape(nb, tc, w3)
    gd_ref[:, :, w3:w3 + wz] = p[:, c_z:].reshape(nb, tc, wz)
    ab = jnp.concatenate([p[:, c_ab:c_z], jnp.zeros((rows, LANES - n_ab), F32)], axis=1)
    gd_ref[:, :, w3 + wz:] = ab.reshape(nb, tc, LANES)


def _in_proj(x3, p, layer, *, tc, to_time_major):
    nb, length, d = x3.shape
    nw, w, n_sl = p["n_mix_pre"], p["w_in"], p["n_sl"]
    n_tot = w.shape[-1]
    nh = p["gdn_nh"]
    w3 = 3 * nh * p["gdn_nw"].shape[-1]
    n_gd = n_tot - n_sl - 2 * nh + LANES
    if to_time_major:
        sl_shape, sl_spec = (length, nb, n_sl), pl.BlockSpec((tc, nb, n_sl), lambda i: (i, 0, 0))
    else:
        sl_shape, sl_spec = (nb, length, n_sl), pl.BlockSpec((nb, tc, n_sl), lambda i: (0, i, 0))
    return pl.pallas_call(
        functools.partial(_in_proj_kernel, n_sl=n_sl, w3=w3, n_ab=2 * nh, to_time_major=to_time_major),
        grid=(length // tc,),
        in_specs=[pl.BlockSpec((nb, tc, d), lambda i: (0, i, 0)),
                  _layer_spec(nw, layer), _layer_spec(w, layer)],
        out_specs=[sl_spec, pl.BlockSpec((nb, tc, n_gd), lambda i: (0, i, 0))],
        out_shape=[jax.ShapeDtypeStruct(sl_shape, F32), jax.ShapeDtypeStruct((nb, length, n_gd), F32)],
        compiler_params=pltpu.CompilerParams(dimension_semantics=("arbitrary",),
                                             vmem_limit_bytes=VMEM_LIMIT),
        name="in_proj",
    )(x3, nw, w)


def _scan_kernel(*refs, nb, tc, s5w, lw, first_pos_is_zero, zero_state):
    (sl_ref, lre_ref, lim_ref, ldt_ref, bre_ref, bim_ref, cre_ref, cim_ref, d_ref,
     wglu_ref, bglu_ref, cw_ref, cb_ref, wa_ref, ba_ref, wx_ref, bx_ref, lam_ref) = refs[:18]
    n_in = 18 if zero_state else 22
    (y_ref, xr_out, xi_out, h_out, cv_out,
     ar_s, ai_s, bbr_s, bbi_s, crb_s, cib_s, xr_c, xi_c, h_c, xr_s, xi_s, xpad_s, a_s, b_s) = refs[n_in:]
    i = pl.program_id(0)
    last = pl.num_programs(0) - 1
    rows = nb * tc

    @pl.when(i == 0)
    def _init():
        lr = jnp.minimum(lre_ref[...], -1e-4)
        li = lim_ref[...]
        dt = jnp.exp(ldt_ref[...])
        mag = jnp.exp(lr * dt)
        ar = mag * jnp.cos(li * dt)
        ai = mag * jnp.sin(li * dt)
        den = lr * lr + li * li
        fr = ((ar - 1.0) * lr + ai * li) / den
        fi = (ai * lr - (ar - 1.0) * li) / den
        ar_s[...] = ar
        ai_s[...] = ai
        bre = bre_ref[...]
        bim = bim_ref[...]
        bbr_s[...] = (fr * bre - fi * bim).astype(BF16)
        bbi_s[...] = (fr * bim + fi * bre).astype(BF16)
        crb_s[...] = cre_ref[...].astype(BF16)
        cib_s[...] = cim_ref[...].astype(BF16)
        if zero_state:
            xr_c[...] = jnp.zeros_like(xr_c)
            xi_c[...] = jnp.zeros_like(xi_c)
            h_c[...] = jnp.zeros_like(h_c)
            xpad_s[0:3 * nb, :] = jnp.zeros((3 * nb, lw), F32)
        else:
            xr0_ref, xi0_ref, h0_ref, cv0_ref = refs[18:22]
            xr_c[...] = xr0_ref[...]
            xi_c[...] = xi0_ref[...]
            h_c[...] = h0_ref[...]
            xpad_s[0:3 * nb, :] = cv0_ref[...]

    u = sl_ref[:, 0:s5w]
    xb = sl_ref[:, s5w:s5w + lw]
    gate = sl_ref[:, s5w + lw:s5w + 2 * lw]

    ub = u.astype(BF16)
    xr_s[...] = jnp.dot(ub, bbr_s[...], preferred_element_type=F32)
    xi_s[...] = jnp.dot(ub, bbi_s[...], preferred_element_type=F32)

    xpad_s[3 * nb:3 * nb + rows, :] = xb
    cw = cw_ref[...]
    xc = (xpad_s[0:rows, :] * cw[0:1] + xpad_s[nb:nb + rows, :] * cw[1:2]
          + xpad_s[2 * nb:2 * nb + rows, :] * cw[2:3] + xb * cw[3:4]) + cb_ref[...]
    new_cv = xpad_s[rows:rows + 3 * nb, :]
    xpad_s[0:3 * nb, :] = new_cv
    r = _sigmoid(_mm(xc, wa_ref[...]) + ba_ref[...])
    gi = _sigmoid(_mm(xc, wx_ref[...]) + bx_ref[...])
    log_a = -LRU_C * r * _softplus(-lam_ref[...])
    a = jnp.exp(log_a)
    m2 = -jnp.tanh(log_a) * (a * a + 1.0)
    mult = jnp.where(m2 > 0.0, m2 * lax.rsqrt(m2), 0.0)
    if first_pos_is_zero:
        rid = lax.broadcasted_iota(jnp.int32, (rows, 1), 0)
        mult = jnp.where(jnp.logical_and(rid < nb, i == 0), 1.0, mult)
    a_s[...] = a
    b_s[...] = mult * gi * xc

    ar = jnp.broadcast_to(ar_s[...], (nb, ar_s.shape[1]))
    ai = jnp.broadcast_to(ai_s[...], (nb, ai_s.shape[1]))

    def step(t, carry):
        xr, xi, h = carry
        sl_t = pl.ds(pl.multiple_of(t * nb, nb), nb)
        nxr = ar * xr - ai * xi + xr_s[sl_t, :]
        nxi = ar * xi + ai * xr + xi_s[sl_t, :]
        nh = a_s[sl_t, :] * h + b_s[sl_t, :]
        xr_s[sl_t, :] = nxr
        xi_s[sl_t, :] = nxi
        b_s[sl_t, :] = nh
        return nxr, nxi, nh

    xr, xi, h = lax.fori_loop(0, tc, step, (xr_c[...], xi_c[...], h_c[...]),
                              unroll=(True if tc <= 8 else 8))
    xr_c[...] = xr
    xi_c[...] = xi
    h_c[...] = h

    y = (jnp.dot(xr_s[...].astype(BF16), crb_s[...], preferred_element_type=F32)
         - jnp.dot(xi_s[...].astype(BF16), cib_s[...], preferred_element_type=F32))
    y = jax.nn.gelu(y + d_ref[...] * u)
    y = y * _sigmoid(_mm(y, wglu_ref[...]) + bglu_ref[...])
    y_ref[:, 0:s5w] = y
    y_ref[:, s5w:s5w + lw] = b_s[...] * jax.nn.gelu(gate)

    @pl.when(i == last)
    def _fin():
        xr_out[...] = xr
        xi_out[...] = xi
        h_out[...] = h
        cv_out[...] = new_cv


def _scan(sl2, p, layer, states, *, nb, tc, first_pos_is_zero):
    rows_total, n_sl = sl2.shape
    s5w = p["s5_d"].shape[-1]
    lw = p["lru_lambda"].shape[-1]
    n_state = p["s5_lre"].shape[-1]
    rows = nb * tc
    params = [p["s5_lre"], p["s5_lim"], p["s5_ldt"], p["s5_bre"], p["s5_bim"], p["s5_cre"], p["s5_cim"],
              p["s5_d"], p["s5_wglu"], p["s5_bglu"], p["lru_cw"], p["lru_cb"], p["lru_wa"], p["lru_ba"],
              p["lru_wx"], p["lru_bx"], p["lru_lambda"]]
    zero_state = states is None
    states = [] if zero_state else list(states)
    return pl.pallas_call(
        functools.partial(_scan_kernel, nb=nb, tc=tc, s5w=s5w, lw=lw, first_pos_is_zero=first_pos_is_zero,
                          zero_state=zero_state),
        grid=(rows_total // rows,),
        in_specs=[pl.BlockSpec((rows, n_sl), lambda i: (i, 0))]
        + [_layer_spec(a, layer) for a in params] + [_const_spec(a.shape) for a in states],
        out_specs=[pl.BlockSpec((rows, s5w + lw), lambda i: (i, 0)),
                   pl.BlockSpec((nb, n_state), lambda i: (0, 0)),
                   pl.BlockSpec((nb, n_state), lambda i: (0, 0)),
                   pl.BlockSpec((nb, lw), lambda i: (0, 0)),
                   pl.BlockSpec((3 * nb, lw), lambda i: (0, 0))],
        out_shape=[jax.ShapeDtypeStruct((rows_total, s5w + lw), F32),
                   jax.ShapeDtypeStruct((nb, n_state), F32),
                   jax.ShapeDtypeStruct((nb, n_state), F32),
                   jax.ShapeDtypeStruct((nb, lw), F32),
                   jax.ShapeDtypeStruct((3 * nb, lw), F32)],
        scratch_shapes=[pltpu.VMEM((1, n_state), F32), pltpu.VMEM((1, n_state), F32),
                        pltpu.VMEM((s5w, n_state), BF16), pltpu.VMEM((s5w, n_state), BF16),
                        pltpu.VMEM((n_state, s5w), BF16), pltpu.VMEM((n_state, s5w), BF16),
                        pltpu.VMEM((nb, n_state), F32), pltpu.VMEM((nb, n_state), F32),
                        pltpu.VMEM((nb, lw), F32),
                        pltpu.VMEM((rows, n_state), F32), pltpu.VMEM((rows, n_state), F32),
                        pltpu.VMEM((rows + 3 * nb, lw), F32),
                        pltpu.VMEM((rows, lw), F32), pltpu.VMEM((rows, lw), F32)],
        compiler_params=pltpu.CompilerParams(dimension_semantics=("arbitrary",),
                                             vmem_limit_bytes=VMEM_LIMIT),
        name="scan",
    )(sl2, *params, *states)


def _gdn_prompt_kernel(gd_ref, cw_ref, alog_ref, dtb_ref, nw_ref, y_ref, s_out, cv_out,
                       s_s, xp_s, *, ns, rg, nh, hd):
    i = pl.program_id(1)
    last = pl.num_programs(1) - 1
    w3 = 3 * nh * hd
    wz = nh * hd
    c = GDN_CHUNK
    nc = rg // c
    wc = nh * c
    rt = ns * rg

    @pl.when(i == 0)
    def _init():
        s_s[...] = jnp.zeros_like(s_s)
        xp_s[...] = jnp.zeros_like(xp_s)

    cw = cw_ref[...]
    xcs = []
    for s in range(ns):
        qkv = gd_ref[s, :, 0:w3]
        xe = jnp.concatenate([xp_s[s], qkv], axis=0)
        acc = qkv * cw[3:4]
        for j in range(1, 4):
            acc = acc + pltpu.roll(xe, j, 0)[SUBLANES:, :] * cw[3 - j:4 - j]
        xp_s[s] = qkv[rg - SUBLANES:, :]
        xcs.append(acc)
    xc = jnp.concatenate(xcs, axis=0)
    xc = xc * _sigmoid(xc)

    ab = jnp.concatenate([gd_ref[s, :, w3 + wz:w3 + wz + LANES] for s in range(ns)], axis=0)
    g_all = -jnp.exp(alog_ref[...]) * _softplus(ab + dtb_ref[...])
    beta_all = _sigmoid(ab)

    def per_head(fn):
        return jnp.concatenate([fn(h) for h in range(nh)], axis=-1)

    q_all = _l2n_heads(xc[:, 0:wz], nh, hd, hd ** -0.5)
    k_all = _l2n_heads(xc[:, wz:2 * wz], nh, hd, 1.0)
    v_all = xc[:, 2 * wz:3 * wz]
    beta_w = per_head(lambda h: jnp.broadcast_to(beta_all[:, nh + h:nh + h + 1], (rt, hd)))
    kb_all = k_all * beta_w
    vb_all = v_all * beta_w

    ri = lax.broadcasted_iota(jnp.int32, (c, wc), 0)
    cj = lax.broadcasted_iota(jnp.int32, (c, wc), 1) % c
    incl = ri >= cj
    strict = ri > cj
    eye = (ri == cj).astype(F32)
    r1 = lax.broadcasted_iota(jnp.int32, (c, c), 0)
    c1 = lax.broadcasted_iota(jnp.int32, (c, c), 1)
    ltri = (r1 >= c1).astype(F32)
    bd_cc = (lax.broadcasted_iota(jnp.int32, (wc, wc), 0) // c
             == lax.broadcasted_iota(jnp.int32, (wc, wc), 1) // c)
    bd_cd = (lax.broadcasted_iota(jnp.int32, (wc, wz), 0) // c
             == lax.broadcasted_iota(jnp.int32, (wc, wz), 1) // hd)

    def bdiag(x, mask):
        return jnp.where(mask, jnp.concatenate([x] * nh, axis=0), 0.0).astype(BF16)

    nc_all = ns * nc
    rows = [slice(cc * c, (cc + 1) * c) for cc in range(nc_all)]
    gcum = [jnp.dot(ltri, g_all[r], precision=HIGHEST, preferred_element_type=F32) for r in rows]
    cgc = [per_head(lambda h, x=x: jnp.broadcast_to(x[:, h:h + 1], (c, hd))) for x in gcum]
    gcol = [per_head(lambda h, x=x: jnp.broadcast_to(x[:, h:h + 1], (c, c))) for x in gcum]
    grow = [jnp.concatenate([xt[h:h + 1, :] for h in range(nh)], axis=1)
            for xt in [x.T for x in gcum]]
    every = range(nc_all)
    decay = [jnp.exp(jnp.where(incl, gcol[n] - grow[n], -jnp.inf)) for n in every]
    eg = [jnp.exp(x) for x in cgc]
    g_last = [x[c - 1:c, :] for x in cgc]
    kk = [_mm_nt(jnp.concatenate([kb_all[r], q_all[r]], axis=0), bdiag(k_all[r], bd_cd)) for r in rows]
    attn = [kk[n][c:2 * c] * decay[n] for n in every]
    rj = [-jnp.where(strict, kk[n][0:c] * decay[n], 0.0) for n in every]
    sj = [eye + rj[n] for n in every]
    rj = [_mm(rj[n], bdiag(rj[n], bd_cc)) for n in every]
    for _ in range(4):
        rs = [_mm(jnp.concatenate([rj[n], sj[n]], axis=0), bdiag(rj[n], bd_cc)) for n in every]
        rj = [x[0:c] for x in rs]
        sj = [sj[n] + rs[n][c:2 * c] for n in every]
    sj = [sj[n] + _mm(sj[n], bdiag(rj[n], bd_cc)) for n in every]
    w = [_mm(sj[n], jnp.concatenate([bdiag(vb_all[rows[n]], bd_cd),
                                     bdiag(kb_all[rows[n]] * eg[n], bd_cd)], axis=1)) for n in every]
    q_dec = [q_all[rows[n]] * eg[n] for n in every]
    k_dec = [k_all[rows[n]] * jnp.exp(g_last[n] - cgc[n]) for n in every]
    s_dec = [jnp.exp(x) for x in g_last]

    s_cur = [[s_s[s, h] for h in range(nh)] for s in range(ns)]
    hs = [slice(h * hd, (h + 1) * hd) for h in range(nh)]
    for m in range(nc):
        for s in range(ns):
            n = s * nc + m
            st = s_cur[s]
            ws = [_mm(jnp.concatenate([w[n][:, wz + h * hd:wz + (h + 1) * hd], q_dec[n][:, hs[h]]], axis=0),
                      st[h]) for h in range(nh)]
            v_new = jnp.concatenate([w[n][:, hs[h]] - ws[h][0:c] for h in range(nh)], axis=-1)
            o = jnp.concatenate([x[c:2 * c] for x in ws], axis=-1) + _mm(attn[n], bdiag(v_new, bd_cd))
            s_cur[s] = [st[h] * s_dec[n][:, hs[h]] + _mm_tn(k_dec[n][:, hs[h]], v_new[:, hs[h]])
                        for h in range(nh)]
            for h in range(nh):
                z = gd_ref[s, m * c:(m + 1) * c, w3 + h * hd:w3 + (h + 1) * hd]
                y_ref[s, m * c:(m + 1) * c, hs[h]] = _rms(o[:, hs[h]], nw_ref[...]) * (z * _sigmoid(z))
    for s in range(ns):
        for h in range(nh):
            s_s[s, h] = s_cur[s][h]

    @pl.when(i == last)
    def _fin():
        for s in range(ns):
            for h in range(nh):
                s_out[s, h] = s_cur[s][h]
            cv_out[s] = xp_s[s, SUBLANES - 3:SUBLANES, :]


def _gdn_prompt(gd3, p, layer, *, rg, ns):
    nb, length, n_gd = gd3.shape
    nh = p["gdn_nh"]
    hd = p["gdn_nw"].shape[-1]
    w3 = 3 * nh * hd
    consts = [p["gdn_cw"], p["gdn_alog"], p["gdn_dtb"], p["gdn_nw"]]
    return pl.pallas_call(
        functools.partial(_gdn_prompt_kernel, ns=ns, rg=rg, nh=nh, hd=hd),
        grid=(nb // ns, length // rg),
        in_specs=[pl.BlockSpec((ns, rg, n_gd), lambda b, i: (b, i, 0))]
        + [_layer_spec(a, layer) for a in consts],
        out_specs=[pl.BlockSpec((ns, rg, nh * hd), lambda b, i: (b, i, 0)),
                   pl.BlockSpec((ns, nh, hd, hd), lambda b, i: (b, 0, 0, 0)),
                   pl.BlockSpec((ns, 3, w3), lambda b, i: (b, 0, 0))],
        out_shape=[jax.ShapeDtypeStruct((nb, length, nh * hd), F32),
                   jax.ShapeDtypeStruct((nb, nh, hd, hd), F32),
                   jax.ShapeDtypeStruct((nb, 3, w3), F32)],
        scratch_shapes=[pltpu.VMEM((ns, nh, hd, hd), F32), pltpu.VMEM((ns, SUBLANES, w3), F32)],
        compiler_params=pltpu.CompilerParams(dimension_semantics=("arbitrary", "arbitrary"),
                                             vmem_limit_bytes=VMEM_LIMIT),
        name="gdn_prompt",
    )(gd3, *consts)


def _gdn_sample_kernel(*refs, nt, nbk, nh, hd, layer, has_acc):
    gd_ref, cv0_ref, s0_ref, cw_ref, alog_ref, dtb_ref, nw_ref = refs[:7]
    y_ref, s_out, cv_out, wk_s, kd_s, wv_s, ws_s = refs[7 + int(has_acc):]
    if not has_acc:
        for other in range(s_out.shape[0]):
            if other != layer:
                s_out[other] = jnp.zeros(s_out.shape[1:], F32)
        s_out = s_out.at[layer]
    w3 = 3 * nh * hd
    wz = nh * hd
    cw = cw_ref[...]
    scale = hd ** -0.5
    xs = [cv0_ref[j] for j in range(3)] + [gd_ref[t, :, 0:w3] for t in range(nt)]
    for j in range(3):
        cv_out[j] = xs[nt + j]
    q, k, v, gc, beta = [], [], [], [], []
    for t in range(nt):
        xc = xs[t] * cw[0:1] + xs[t + 1] * cw[1:2] + xs[t + 2] * cw[2:3] + xs[t + 3] * cw[3:4]
        xc = xc * _sigmoid(xc)
        ab = gd_ref[t, :, w3 + wz:w3 + wz + LANES]
        g_t = -jnp.exp(alog_ref[...]) * _softplus(ab + dtb_ref[...])
        gc.append(g_t if t == 0 else gc[-1] + g_t)
        beta.append(_sigmoid(ab))
        qn = _l2n_heads(xc[:, 0:wz], nh, hd, scale)
        kn = _l2n_heads(xc[:, wz:2 * wz], nh, hd, 1.0)
        q.append([qn[:, h * hd:(h + 1) * hd] for h in range(nh)])
        k.append([kn[:, h * hd:(h + 1) * hd] for h in range(nh)])
        v.append([xc[:, 2 * wz + h * hd:2 * wz + (h + 1) * hd] for h in range(nh)])

    def dot(a, b):
        return jnp.sum(a * b, axis=-1, keepdims=True)

    wk_rows = [[None] * nh for _ in range(nt)]
    qd_rows = [[None] * nh for _ in range(nt)]
    kd_rows = [[None] * nh for _ in range(nt)]
    wv_rows = [[None] * nh for _ in range(nt)]
    sd_rows = [None] * nh
    attn = [None] * nh
    for h in range(nh):
        gch = [gc[t][:, h:h + 1] for t in range(nt)]
        bh = [beta[t][:, nh + h:nh + h + 1] for t in range(nt)]
        e = [jnp.exp(x) for x in gch]
        kb = [k[t][h] * bh[t] for t in range(nt)]
        vb = [v[t][h] * bh[t] for t in range(nt)]
        kbe = [kb[t] * e[t] for t in range(nt)]
        dec = [[jnp.exp(gch[i] - gch[j]) for j in range(i)] for i in range(nt)]
        m = [[dot(kb[i], k[j][h]) * dec[i][j] for j in range(i)] for i in range(nt)]
        attn[h] = [[dot(q[i][h], k[j][h]) * (dec[i][j] if j < i else 1.0) for j in range(i + 1)]
                   for i in range(nt)]
        tinv = [[None] * nt for _ in range(nt)]
        for j in range(nt):
            for i in range(j + 1, nt):
                acc = m[i][j]
                for l in range(j + 1, i):
                    acc = acc + m[i][l] * tinv[l][j]
                tinv[i][j] = -acc
        for i in range(nt):
            wv_i, wk_i = vb[i], kbe[i]
            for j in range(i):
                wv_i = wv_i + tinv[i][j] * vb[j]
                wk_i = wk_i + tinv[i][j] * kbe[j]
            wv_rows[i][h] = wv_i
            wk_rows[i][h] = wk_i
            qd_rows[i][h] = q[i][h] * e[i]
            kd_rows[i][h] = k[i][h] * jnp.exp(gch[nt - 1] - gch[i])
        sd_rows[h] = jnp.broadcast_to(e[nt - 1], (nbk, hd))

    cat = lambda parts: jnp.concatenate(parts, axis=-1)
    to_seq = lambda slabs: jnp.swapaxes(jnp.stack(slabs, axis=0), 0, 1)
    zero = jnp.zeros((nbk, wz), F32)
    wk_s[...] = to_seq([cat(wk_rows[i]) for i in range(nt)] + [cat(qd_rows[i]) for i in range(nt)])
    kd_s[...] = to_seq([cat(kd_rows[i]) for i in range(nt)] + [zero] * nt)
    wv_s[...] = to_seq([cat(wv_rows[i]) for i in range(nt)] + [cat(sd_rows)] * nt)

    def per_seq(b, _):
        twk_all = wk_s[b]
        tkd_all = kd_s[b]
        twv_all = wv_s[b]
        for h in range(nh):
            cols = slice(h * hd, (h + 1) * hd)
            twk, tkd, twv = twk_all[:, cols], tkd_all[:, cols], twv_all[:, cols]
            a = _mm_tn(tkd, jnp.concatenate([twv, twk], axis=1))
            s = s0_ref[b, h]
            r = _mm(jnp.concatenate([twk, a[:, hd:]], axis=0), s)
            s_out[b, h] = s * twv[nt:nt + 1, :] + a[:, 0:hd] - r[2 * nt:, :]
            ws_s[b, :, cols] = r[0:2 * nt, :]
        return 0

    lax.fori_loop(0, nbk, per_seq, 0, unroll=4)

    ws_tm = jnp.swapaxes(ws_s[...], 0, 1)
    for h in range(nh):
        cols = slice(h * hd, (h + 1) * hd)
        v_new = [wv_rows[j][h] - ws_tm[j][:, cols] for j in range(nt)]
        for i in range(nt):
            o = ws_tm[nt + i][:, cols]
            for j in range(i + 1):
                o = o + attn[h][i][j] * v_new[j]
            z = gd_ref[i, :, w3 + h * hd:w3 + (h + 1) * hd]
            y_ref[i, :, cols] = _rms(o, nw_ref[...]) * (z * _sigmoid(z))


def _gdn_sample(gd3, cv0, s_all, s_acc, layer, p, *, nbk):
    nt, nb, n_gd = gd3.shape
    nh = p["gdn_nh"]
    hd = p["gdn_nw"].shape[-1]
    w3 = 3 * nh * hd
    wz = nh * hd
    s_spec = pl.BlockSpec((None, nbk, nh, hd, hd), lambda i: (layer, i, 0, 0, 0))
    consts = [p["gdn_cw"], p["gdn_alog"], p["gdn_dtb"], p["gdn_nw"]]
    args = [gd3, cv0, s_all] + consts
    in_specs = [pl.BlockSpec((nt, nbk, n_gd), lambda i: (0, i, 0)),
                pl.BlockSpec((3, nbk, w3), lambda i: (0, i, 0)),
                s_spec] + [_layer_spec(a, layer) for a in consts]
    aliases = {}
    s_out_spec = s_spec
    if s_acc is not None:
        aliases = {len(args): 1}
        args.append(s_acc)
        in_specs.append(pl.BlockSpec(memory_space=pl.ANY))
    else:
        s_out_spec = pl.BlockSpec((s_all.shape[0], nbk, nh, hd, hd), lambda i: (0, i, 0, 0, 0))
    return pl.pallas_call(
        functools.partial(_gdn_sample_kernel, nt=nt, nbk=nbk, nh=nh, hd=hd, layer=layer,
                          has_acc=s_acc is not None),
        grid=(nb // nbk,),
        in_specs=in_specs,
        out_specs=[pl.BlockSpec((nt, nbk, wz), lambda i: (0, i, 0)),
                   s_out_spec,
                   pl.BlockSpec((3, nbk, w3), lambda i: (0, i, 0))],
        out_shape=[jax.ShapeDtypeStruct((nt, nb, wz), F32),
                   jax.ShapeDtypeStruct(s_all.shape, F32),
                   jax.ShapeDtypeStruct((3, nb, w3), F32)],
        scratch_shapes=[pltpu.VMEM((nbk, 2 * nt, wz), F32) for _ in range(4)],
        input_output_aliases=aliases,
        compiler_params=pltpu.CompilerParams(dimension_semantics=("arbitrary",),
                                             vmem_limit_bytes=VMEM_LIMIT),
        name="gdn_sample",
    )(*args)


def _ffn_kernel(x_ref, ysl_ref, yg_ref, wo_ref, nmp_ref, nfp_ref, nfq_ref, wgu_ref, wd_ref, o_ref,
                *, from_time_major):
    nb, tc, d = x_ref.shape
    rows = nb * tc
    x = x_ref[...].reshape(rows, d)
    ysl = ysl_ref[...]
    if from_time_major:
        ysl = jnp.swapaxes(ysl, 0, 1)
    ysl = ysl.reshape(rows, ysl.shape[-1])
    yg = yg_ref[...].reshape(rows, yg_ref.shape[-1])
    mix = jnp.concatenate([ysl, yg], axis=-1)
    x = x + _rms(_mm(mix, wo_ref[...]), nmp_ref[...])
    h = _rms(x, nfp_ref[...])
    gu = _mm(h, wgu_ref[...])
    hid = gu.shape[1] // 2
    gt = gu[:, :hid]
    f = _mm(gt * _sigmoid(gt) * gu[:, hid:], wd_ref[...])
    x = x + _rms(f, nfq_ref[...])
    o_ref[...] = x.reshape(nb, tc, d)


def _ffn(x3, ysl3, yg3, p, layer, *, tc, from_time_major):
    nb, length, d = x3.shape
    n1 = ysl3.shape[-1]
    n2 = yg3.shape[-1]
    consts = [p["w_out"], p["n_mix_post"], p["n_ffn_pre"], p["n_ffn_post"], p["w_gu"], p["w_down"]]
    if from_time_major:
        ysl_spec = pl.BlockSpec((tc, nb, n1), lambda i: (i, 0, 0))
    else:
        ysl_spec = pl.BlockSpec((nb, tc, n1), lambda i: (0, i, 0))
    return pl.pallas_call(
        functools.partial(_ffn_kernel, from_time_major=from_time_major),
        grid=(length // tc,),
        in_specs=[pl.BlockSpec((nb, tc, d), lambda i: (0, i, 0)), ysl_spec,
                  pl.BlockSpec((nb, tc, n2), lambda i: (0, i, 0))]
        + [_layer_spec(a, layer) for a in consts],
        out_specs=pl.BlockSpec((nb, tc, d), lambda i: (0, i, 0)),
        out_shape=jax.ShapeDtypeStruct((nb, length, d), F32),
        compiler_params=pltpu.CompilerParams(dimension_semantics=("arbitrary",),
                                             vmem_limit_bytes=VMEM_LIMIT),
        name="ffn",
    )(x3, ysl3, yg3, *consts)


def _block_diag(blocks):
    dep, n, r, c = blocks.shape
    eye = jnp.eye(n, dtype=blocks.dtype)
    return (eye[None, :, None, :, None] * blocks[:, :, :, None, :]).reshape(dep, n * r, n * c)


def _params(norm_mix_pre, norm_mix_post, norm_ffn_pre, norm_ffn_post, w_in, s5_lambda_re,
            s5_lambda_im, s5_log_dt, s5_b_re, s5_b_im, s5_c_re, s5_c_im, s5_d, s5_w_glu, s5_b_glu,
            lru_conv_w, lru_conv_b, lru_w_a, lru_b_a, lru_w_x, lru_b_x, lru_lambda, gdn_conv_w,
            gdn_a_log, gdn_dt_bias, gdn_norm_w, w_out, ffn_w_gate_up, ffn_w_down):
    dep, g, pst, hgrp = s5_b_re.shape
    s5w = g * hgrp
    lw = lru_lambda.shape[1]
    nh = gdn_a_log.shape[1]
    hd = gdn_norm_w.shape[1]
    gw = nh * hd
    row = lambda v: v[:, None, :]
    lanes = lambda v: jnp.pad(v, ((0, 0), (0, LANES - v.shape[1])))[:, None, :]
    n_sl = s5w + 2 * lw
    w_r = w_in.astype(BF16)
    return {
        "n_sl": n_sl,
        "n_mix_pre": row(norm_mix_pre), "n_mix_post": row(norm_mix_post),
        "n_ffn_pre": row(norm_ffn_pre), "n_ffn_post": row(norm_ffn_post),
        "w_in": w_r,
        "s5_lre": s5_lambda_re.reshape(dep, 1, g * pst), "s5_lim": s5_lambda_im.reshape(dep, 1, g * pst),
        "s5_ldt": jnp.repeat(s5_log_dt, pst, axis=1)[:, None, :],
        "s5_bre": _block_diag(jnp.swapaxes(s5_b_re, 2, 3)),
        "s5_bim": _block_diag(jnp.swapaxes(s5_b_im, 2, 3)),
        "s5_cre": _block_diag(jnp.swapaxes(s5_c_re, 2, 3)),
        "s5_cim": _block_diag(jnp.swapaxes(s5_c_im, 2, 3)),
        "s5_d": row(s5_d), "s5_wglu": s5_w_glu.astype(BF16), "s5_bglu": row(s5_b_glu),
        "lru_cw": lru_conv_w, "lru_cb": row(lru_conv_b),
        "lru_wa": _block_diag(lru_w_a).astype(BF16), "lru_ba": row(lru_b_a),
        "lru_wx": _block_diag(lru_w_x).astype(BF16), "lru_bx": row(lru_b_x),
        "lru_lambda": row(lru_lambda),
        "gdn_cw": gdn_conv_w, "gdn_alog": lanes(gdn_a_log),
        "gdn_dtb": lanes(gdn_dt_bias), "gdn_nw": row(gdn_norm_w), "gdn_nh": nh,
        "w_out": w_out.astype(BF16), "w_gu": ffn_w_gate_up.astype(BF16),
        "w_down": ffn_w_down.astype(BF16),
    }


def _prompt_layer(x, p, layer, *, tc, rg):
    nb, length, _ = x.shape
    sl_tm, gd = _in_proj(x, p, layer, tc=tc, to_time_major=True)
    lw = p["lru_lambda"].shape[-1]
    ysl, xr, xi, hl, cv = _scan(sl_tm.reshape(length * nb, -1), p, layer, None, nb=nb, tc=tc,
                                first_pos_is_zero=True)
    yg, s_new, gcv = _gdn_prompt(gd, p, layer, rg=rg, ns=2 if nb % 2 == 0 else 1)
    x = _ffn(x, ysl.reshape(length, nb, -1), yg, p, layer, tc=tc, from_time_major=True)
    lcv = jnp.swapaxes(cv.reshape(3, nb, lw), 0, 1)
    return x, (xr, xi, hl, lcv, s_new, gcv)


def _sample_layer(x_tm, st, gdn_s_all, gdn_s_acc, layer, p, *, nbk):
    s5_re, s5_im, lru_h, lru_conv, gdn_conv = st
    nb = s5_re.shape[0]
    rows = x_tm.shape[1]
    nt = rows // nb
    sl, gd = _in_proj(x_tm, p, layer, tc=rows, to_time_major=False)
    lw = p["lru_lambda"].shape[-1]
    cv0 = jnp.swapaxes(lru_conv, 0, 1).reshape(3 * nb, lw)
    states = [s5_re.reshape(nb, -1), s5_im.reshape(nb, -1), lru_h, cv0]
    ysl, xr, xi, hl, cv = _scan(sl.reshape(rows, -1), p, layer, states, nb=nb, tc=nt,
                                first_pos_is_zero=(PAST_LEN == 0))
    yg, s_acc, gcv = _gdn_sample(gd.reshape(nt, nb, -1), jnp.swapaxes(gdn_conv, 0, 1), gdn_s_all, gdn_s_acc,
                                 layer, p, nbk=nbk)
    x_tm = _ffn(x_tm, ysl.reshape(1, rows, -1), yg.reshape(1, rows, -1), p, layer, tc=rows,
                from_time_major=False)
    lcv = jnp.swapaxes(cv.reshape(3, nb, lw), 0, 1)
    return x_tm, (xr, xi, hl, lcv, jnp.swapaxes(gcv, 0, 1)), s_acc


def kernel(x_prompt, x_sample, state_s5_re, state_s5_im, state_lru_h, state_lru_conv, state_gdn_S, state_gdn_conv, norm_mix_pre, norm_mix_post, norm_ffn_pre, norm_ffn_post, w_in, s5_lambda_re, s5_lambda_im, s5_log_dt, s5_b_re, s5_b_im, s5_c_re, s5_c_im, s5_d, s5_w_glu, s5_b_glu, lru_conv_w, lru_conv_b, lru_w_a, lru_b_a, lru_w_x, lru_b_x, lru_lambda, gdn_conv_w, gdn_a_log, gdn_dt_bias, gdn_norm_w, w_out, ffn_w_gate_up, ffn_w_down):
    depth = w_in.shape[0]
    bp, lp, d = x_prompt.shape
    bs, ls, _ = x_sample.shape
    g, pst = s5_lambda_re.shape[1:]
    tc = min(64, lp)
    rg = min(4 * GDN_CHUNK, lp)
    y_p = x_prompt
    y_s = jnp.swapaxes(x_sample, 0, 1).reshape(1, ls * bs, d)
    new_p = [[] for _ in range(6)]
    new_s = [[] for _ in range(5)]
    s_gdn_s = None
    p = _params(norm_mix_pre, norm_mix_post, norm_ffn_pre, norm_ffn_post, w_in, s5_lambda_re,
                s5_lambda_im, s5_log_dt, s5_b_re, s5_b_im, s5_c_re, s5_c_im, s5_d, s5_w_glu,
                s5_b_glu, lru_conv_w, lru_conv_b, lru_w_a, lru_b_a, lru_w_x, lru_b_x, lru_lambda,
                gdn_conv_w, gdn_a_log, gdn_dt_bias, gdn_norm_w, w_out, ffn_w_gate_up, ffn_w_down)
    for l in range(depth):
        y_p, sp = _prompt_layer(y_p, p, l, tc=tc, rg=rg)
        st_s = (state_s5_re[l], state_s5_im[l], state_lru_h[l], state_lru_conv[l], state_gdn_conv[l])
        y_s, ss, s_gdn_s = _sample_layer(y_s, st_s, state_gdn_S, s_gdn_s, l, p, nbk=min(16, bs))
        for j in range(6):
            new_p[j].append(sp[j])
        for j in range(5):
            new_s[j].append(ss[j])
    y_s = jnp.swapaxes(y_s.reshape(ls, bs, d), 0, 1)
    outs_p = [jnp.stack(t, axis=0) for t in new_p]
    outs_s = [jnp.stack(t, axis=0) for t in new_s]
    outs_s.insert(4, s_gdn_s)
    outs_p[0] = outs_p[0].reshape(depth, bp, g, pst)
    outs_p[1] = outs_p[1].reshape(depth, bp, g, pst)
    outs_s[0] = outs_s[0].reshape(depth, bs, g, pst)
    outs_s[1] = outs_s[1].reshape(depth, bs, g, pst)
    return (y_p, y_s, *outs_p, *outs_s)
```

```python
import functools

import jax
import jax.numpy as jnp
from jax import lax
from jax.experimental import pallas as pl
from jax.experimental.pallas import tpu as pltpu

F32 = jnp.float32
BF16 = jnp.bfloat16
HIGHEST = lax.Precision.HIGHEST

NORM_EPS = 1e-6
LRU_C = 8.0
GDN_CHUNK = 64
PAST_LEN = 16384
LANES = 128
SUBLANES = 8
VMEM_LIMIT = 56 * 1024 * 1024

TOKENS_PER_SEQ_STEP = 64
GDN_ROWS_PER_STEP = 4 * GDN_CHUNK
GDN_SEQS_PER_STEP = 2
DECODE_SEQS_PER_STEP = 16
DECODE_SEQ_UNROLL = 4


def _rms(x, w):
    return x * lax.rsqrt(jnp.mean(x * x, axis=-1, keepdims=True) + NORM_EPS) * w


def _sigmoid(x):
    return jax.nn.sigmoid(x)


def _softplus(x):
    return jnp.maximum(x, 0.0) + jnp.log1p(jnp.exp(-jnp.abs(x)))


def _mm(a, b):
    return jnp.dot(a.astype(BF16), b.astype(BF16), preferred_element_type=F32)


def _mm_nt(a, b):
    return lax.dot_general(a.astype(BF16), b.astype(BF16), (((1,), (1,)), ((), ())),
                           preferred_element_type=F32)


def _mm_tn(a, b):
    return lax.dot_general(a.astype(BF16), b.astype(BF16), (((0,), (0,)), ((), ())),
                           preferred_element_type=F32)


def _const_spec(shape):
    n = len(shape)
    return pl.BlockSpec(shape, lambda *_: (0,) * n, pipeline_mode=pl.Buffered(1))


def _layer_spec(arr, layer):
    n = arr.ndim - 1
    return pl.BlockSpec((None,) + arr.shape[1:], lambda *_: (layer,) + (0,) * n,
                        pipeline_mode=pl.Buffered(1))


def _l2n_heads(x, nh, hd, mul):
    def one(h):
        xh = x[:, h * hd:(h + 1) * hd]
        return xh * (lax.rsqrt(jnp.sum(xh * xh, axis=-1, keepdims=True) + NORM_EPS) * mul)
    return jnp.concatenate([one(h) for h in range(nh)], axis=-1)


def _in_proj_kernel(x_ref, nw_ref, w_ref, sl_ref, gd_ref, *, n_sl, w3, n_ab, to_time_major):
    nb, tc, d = x_ref.shape
    rows = nb * tc
    x = x_ref[...].reshape(rows, d)
    h = _rms(x, nw_ref[...])
    p = _mm(h, w_ref[...])
    sl = p[:, :n_sl].reshape(nb, tc, n_sl)
    if to_time_major:
        sl = jnp.swapaxes(sl, 0, 1)
    sl_ref[...] = sl
    c_ab = n_sl + w3
    c_z = c_ab + n_ab
    wz = p.shape[1] - c_z
    gd_ref[:, :, 0:w3] = p[:, n_sl:c_ab].reshape(nb, tc, w3)
    gd_ref[:, :, w3:w3 + wz] = p[:, c_z:].reshape(nb, tc, wz)
    ab = jnp.concatenate([p[:, c_ab:c_z], jnp.zeros((rows, LANES - n_ab), F32)], axis=1)
    gd_ref[:, :, w3 + wz:] = ab.reshape(nb, tc, LANES)


def _in_proj(x3, p, layer, *, tc, to_time_major):
    nb, length, d = x3.shape
    nw, w, n_sl = p["n_mix_pre"], p["w_in"], p["n_sl"]
    n_tot = w.shape[-1]
    nh = p["gdn_nh"]
    w3 = 3 * nh * p["gdn_nw"].shape[-1]
    n_gd = n_tot - n_sl - 2 * nh + LANES
    if to_time_major:
        sl_shape, sl_spec = (length, nb, n_sl), pl.BlockSpec((tc, nb, n_sl), lambda i: (i, 0, 0))
    else:
        sl_shape, sl_spec = (nb, length, n_sl), pl.BlockSpec((nb, tc, n_sl), lambda i: (0, i, 0))
    return pl.pallas_call(
        functools.partial(_in_proj_kernel, n_sl=n_sl, w3=w3, n_ab=2 * nh, to_time_major=to_time_major),
        grid=(length // tc,),
        in_specs=[pl.BlockSpec((nb, tc, d), lambda i: (0, i, 0)),
                  _layer_spec(nw, layer), _layer_spec(w, layer)],
        out_specs=[sl_spec, pl.BlockSpec((nb, tc, n_gd), lambda i: (0, i, 0))],
        out_shape=[jax.ShapeDtypeStruct(sl_shape, F32), jax.ShapeDtypeStruct((nb, length, n_gd), F32)],
        compiler_params=pltpu.CompilerParams(dimension_semantics=("arbitrary",),
                                             vmem_limit_bytes=VMEM_LIMIT),
        name="in_proj",
    )(x3, nw, w)


def _scan_kernel(*refs, nb, tc, s5w, lw, first_pos_is_zero, zero_state):
    (sl_ref, lre_ref, lim_ref, ldt_ref, bre_ref, bim_ref, cre_ref, cim_ref, d_ref,
     wglu_ref, bglu_ref, cw_ref, cb_ref, wa_ref, ba_ref, wx_ref, bx_ref, lam_ref) = refs[:18]
    n_in = 18 if zero_state else 22
    (y_ref, xr_out, xi_out, h_out, cv_out,
     ar_s, ai_s, bbr_s, bbi_s, crb_s, cib_s, xr_c, xi_c, h_c, xr_s, xi_s, xpad_s, a_s, b_s) = refs[n_in:]
    i = pl.program_id(0)
    last = pl.num_programs(0) - 1
    rows = nb * tc

    @pl.when(i == 0)
    def _init():
        lr = jnp.minimum(lre_ref[...], -1e-4)
        li = lim_ref[...]
        dt = jnp.exp(ldt_ref[...])
        mag = jnp.exp(lr * dt)
        ar = mag * jnp.cos(li * dt)
        ai = mag * jnp.sin(li * dt)
        den = lr * lr + li * li
        fr = ((ar - 1.0) * lr + ai * li) / den
        fi = (ai * lr - (ar - 1.0) * li) / den
        ar_s[...] = ar
        ai_s[...] = ai
        bre = bre_ref[...]
        bim = bim_ref[...]
        bbr_s[...] = (fr * bre - fi * bim).astype(BF16)
        bbi_s[...] = (fr * bim + fi * bre).astype(BF16)
        crb_s[...] = cre_ref[...].astype(BF16)
        cib_s[...] = cim_ref[...].astype(BF16)
        if zero_state:
            xr_c[...] = jnp.zeros_like(xr_c)
            xi_c[...] = jnp.zeros_like(xi_c)
            h_c[...] = jnp.zeros_like(h_c)
            xpad_s[0:3 * nb, :] = jnp.zeros((3 * nb, lw), F32)
        else:
            xr0_ref, xi0_ref, h0_ref, cv0_ref = refs[18:22]
            xr_c[...] = xr0_ref[...]
            xi_c[...] = xi0_ref[...]
            h_c[...] = h0_ref[...]
            xpad_s[0:3 * nb, :] = cv0_ref[...]

    u = sl_ref[:, 0:s5w]
    xb = sl_ref[:, s5w:s5w + lw]
    gate = sl_ref[:, s5w + lw:s5w + 2 * lw]

    ub = u.astype(BF16)
    xr_s[...] = jnp.dot(ub, bbr_s[...], preferred_element_type=F32)
    xi_s[...] = jnp.dot(ub, bbi_s[...], preferred_element_type=F32)

    xpad_s[3 * nb:3 * nb + rows, :] = xb
    cw = cw_ref[...]
    xc = (xpad_s[0:rows, :] * cw[0:1] + xpad_s[nb:nb + rows, :] * cw[1:2]
          + xpad_s[2 * nb:2 * nb + rows, :] * cw[2:3] + xb * cw[3:4]) + cb_ref[...]
    new_cv = xpad_s[rows:rows + 3 * nb, :]
    xpad_s[0:3 * nb, :] = new_cv
    r = _sigmoid(_mm(xc, wa_ref[...]) + ba_ref[...])
    gi = _sigmoid(_mm(xc, wx_ref[...]) + bx_ref[...])
    log_a = -LRU_C * r * _softplus(-lam_ref[...])
    a = jnp.exp(log_a)
    m2 = -jnp.tanh(log_a) * (a * a + 1.0)
    mult = jnp.where(m2 > 0.0, m2 * lax.rsqrt(m2), 0.0)
    if first_pos_is_zero:
        rid = lax.broadcasted_iota(jnp.int32, (rows, 1), 0)
        mult = jnp.where(jnp.logical_and(rid < nb, i == 0), 1.0, mult)
    a_s[...] = a
    b_s[...] = mult * gi * xc

    ar = jnp.broadcast_to(ar_s[...], (nb, ar_s.shape[1]))
    ai = jnp.broadcast_to(ai_s[...], (nb, ai_s.shape[1]))

    def step(t, carry):
        xr, xi, h = carry
        sl_t = pl.ds(pl.multiple_of(t * nb, nb), nb)
        nxr = ar * xr - ai * xi + xr_s[sl_t, :]
        nxi = ar * xi + ai * xr + xi_s[sl_t, :]
        nh = a_s[sl_t, :] * h + b_s[sl_t, :]
        xr_s[sl_t, :] = nxr
        xi_s[sl_t, :] = nxi
        b_s[sl_t, :] = nh
        return nxr, nxi, nh

    xr, xi, h = lax.fori_loop(0, tc, step, (xr_c[...], xi_c[...], h_c[...]),
                              unroll=(True if tc <= 8 else 8))
    xr_c[...] = xr
    xi_c[...] = xi
    h_c[...] = h

    y = (jnp.dot(xr_s[...].astype(BF16), crb_s[...], preferred_element_type=F32)
         - jnp.dot(xi_s[...].astype(BF16), cib_s[...], preferred_element_type=F32))
    y = jax.nn.gelu(y + d_ref[...] * u)
    y = y * _sigmoid(_mm(y, wglu_ref[...]) + bglu_ref[...])
    y_ref[:, 0:s5w] = y
    y_ref[:, s5w:s5w + lw] = b_s[...] * jax.nn.gelu(gate)

    @pl.when(i == last)
    def _fin():
        xr_out[...] = xr
        xi_out[...] = xi
        h_out[...] = h
        cv_out[...] = new_cv


def _scan(sl2, p, layer, states, *, nb, tc, first_pos_is_zero):
    rows_total, n_sl = sl2.shape
    s5w = p["s5_d"].shape[-1]
    lw = p["lru_lambda"].shape[-1]
    n_state = p["s5_lre"].shape[-1]
    rows = nb * tc
    params = [p["s5_lre"], p["s5_lim"], p["s5_ldt"], p["s5_bre"], p["s5_bim"], p["s5_cre"], p["s5_cim"],
              p["s5_d"], p["s5_wglu"], p["s5_bglu"], p["lru_cw"], p["lru_cb"], p["lru_wa"], p["lru_ba"],
              p["lru_wx"], p["lru_bx"], p["lru_lambda"]]
    zero_state = states is None
    states = [] if zero_state else list(states)
    return pl.pallas_call(
        functools.partial(_scan_kernel, nb=nb, tc=tc, s5w=s5w, lw=lw, first_pos_is_zero=first_pos_is_zero,
                          zero_state=zero_state),
        grid=(rows_total // rows,),
        in_specs=[pl.BlockSpec((rows, n_sl), lambda i: (i, 0))]
        + [_layer_spec(a, layer) for a in params] + [_const_spec(a.shape) for a in states],
        out_specs=[pl.BlockSpec((rows, s5w + lw), lambda i: (i, 0)),
                   pl.BlockSpec((nb, n_state), lambda i: (0, 0)),
                   pl.BlockSpec((nb, n_state), lambda i: (0, 0)),
                   pl.BlockSpec((nb, lw), lambda i: (0, 0)),
                   pl.BlockSpec((3 * nb, lw), lambda i: (0, 0))],
        out_shape=[jax.ShapeDtypeStruct((rows_total, s5w + lw), F32),
                   jax.ShapeDtypeStruct((nb, n_state), F32),
                   jax.ShapeDtypeStruct((nb, n_state), F32),
                   jax.ShapeDtypeStruct((nb, lw), F32),
                   jax.ShapeDtypeStruct((3 * nb, lw), F32)],
        scratch_shapes=[pltpu.VMEM((1, n_state), F32), pltpu.VMEM((1, n_state), F32),
                        pltpu.VMEM((s5w, n_state), BF16), pltpu.VMEM((s5w, n_state), BF16),
                        pltpu.VMEM((n_state, s5w), BF16), pltpu.VMEM((n_state, s5w), BF16),
                        pltpu.VMEM((nb, n_state), F32), pltpu.VMEM((nb, n_state), F32),
                        pltpu.VMEM((nb, lw), F32),
                        pltpu.VMEM((rows, n_state), F32), pltpu.VMEM((rows, n_state), F32),
                        pltpu.VMEM((rows + 3 * nb, lw), F32),
                        pltpu.VMEM((rows, lw), F32), pltpu.VMEM((rows, lw), F32)],
        compiler_params=pltpu.CompilerParams(dimension_semantics=("arbitrary",),
                                             vmem_limit_bytes=VMEM_LIMIT),
        name="scan",
    )(sl2, *params, *states)


def _gdn_prompt_kernel(gd_ref, cw_ref, alog_ref, dtb_ref, nw_ref, y_ref, s_out, cv_out,
                       s_s, xp_s, *, ns, rg, nh, hd):
    i = pl.program_id(1)
    last = pl.num_programs(1) - 1
    w3 = 3 * nh * hd
    wz = nh * hd
    c = GDN_CHUNK
    nc = rg // c
    wc = nh * c
    rt = ns * rg

    @pl.when(i == 0)
    def _init():
        s_s[...] = jnp.zeros_like(s_s)
        xp_s[...] = jnp.zeros_like(xp_s)

    cw = cw_ref[...]
    xcs = []
    for s in range(ns):
        qkv = gd_ref[s, :, 0:w3]
        xe = jnp.concatenate([xp_s[s], qkv], axis=0)
        acc = qkv * cw[3:4]
        for j in range(1, 4):
            acc = acc + pltpu.roll(xe, j, 0)[SUBLANES:, :] * cw[3 - j:4 - j]
        xp_s[s] = qkv[rg - SUBLANES:, :]
        xcs.append(acc)
    xc = jnp.concatenate(xcs, axis=0)
    xc = xc * _sigmoid(xc)

    ab = jnp.concatenate([gd_ref[s, :, w3 + wz:w3 + wz + LANES] for s in range(ns)], axis=0)
    g_all = -jnp.exp(alog_ref[...]) * _softplus(ab + dtb_ref[...])
    beta_all = _sigmoid(ab)

    def per_head(fn):
        return jnp.concatenate([fn(h) for h in range(nh)], axis=-1)

    q_all = _l2n_heads(xc[:, 0:wz], nh, hd, hd ** -0.5)
    k_all = _l2n_heads(xc[:, wz:2 * wz], nh, hd, 1.0)
    v_all = xc[:, 2 * wz:3 * wz]
    beta_w = per_head(lambda h: jnp.broadcast_to(beta_all[:, nh + h:nh + h + 1], (rt, hd)))
    kb_all = k_all * beta_w
    vb_all = v_all * beta_w

    ri = lax.broadcasted_iota(jnp.int32, (c, wc), 0)
    cj = lax.broadcasted_iota(jnp.int32, (c, wc), 1) % c
    incl = ri >= cj
    strict = ri > cj
    eye = (ri == cj).astype(F32)
    r1 = lax.broadcasted_iota(jnp.int32, (c, c), 0)
    c1 = lax.broadcasted_iota(jnp.int32, (c, c), 1)
    ltri = (r1 >= c1).astype(F32)
    bd_cc = (lax.broadcasted_iota(jnp.int32, (wc, wc), 0) // c
             == lax.broadcasted_iota(jnp.int32, (wc, wc), 1) // c)
    bd_cd = (lax.broadcasted_iota(jnp.int32, (wc, wz), 0) // c
             == lax.broadcasted_iota(jnp.int32, (wc, wz), 1) // hd)

    def bdiag(x, mask):
        return jnp.where(mask, jnp.concatenate([x] * nh, axis=0), 0.0).astype(BF16)

    nc_all = ns * nc
    rows = [slice(cc * c, (cc + 1) * c) for cc in range(nc_all)]
    gcum = [jnp.dot(ltri, g_all[r], precision=HIGHEST, preferred_element_type=F32) for r in rows]
    cgc = [per_head(lambda h, x=x: jnp.broadcast_to(x[:, h:h + 1], (c, hd))) for x in gcum]
    gcol = [per_head(lambda h, x=x: jnp.broadcast_to(x[:, h:h + 1], (c, c))) for x in gcum]
    grow = [jnp.concatenate([xt[h:h + 1, :] for h in range(nh)], axis=1)
            for xt in [x.T for x in gcum]]
    every = range(nc_all)
    decay = [jnp.exp(jnp.where(incl, gcol[n] - grow[n], -jnp.inf)) for n in every]
    eg = [jnp.exp(x) for x in cgc]
    g_last = [x[c - 1:c, :] for x in cgc]
    kk = [_mm_nt(jnp.concatenate([kb_all[r], q_all[r]], axis=0), bdiag(k_all[r], bd_cd)) for r in rows]
    attn = [kk[n][c:2 * c] * decay[n] for n in every]
    rj = [-jnp.where(strict, kk[n][0:c] * decay[n], 0.0) for n in every]
    sj = [eye + rj[n] for n in every]
    rj = [_mm(rj[n], bdiag(rj[n], bd_cc)) for n in every]
    for _ in range(4):
        rs = [_mm(jnp.concatenate([rj[n], sj[n]], axis=0), bdiag(rj[n], bd_cc)) for n in every]
        rj = [x[0:c] for x in rs]
        sj = [sj[n] + rs[n][c:2 * c] for n in every]
    sj = [sj[n] + _mm(sj[n], bdiag(rj[n], bd_cc)) for n in every]
    w = [_mm(sj[n], jnp.concatenate([bdiag(vb_all[rows[n]], bd_cd),
                                     bdiag(kb_all[rows[n]] * eg[n], bd_cd)], axis=1)) for n in every]
    q_dec = [q_all[rows[n]] * eg[n] for n in every]
    k_dec = [k_all[rows[n]] * jnp.exp(g_last[n] - cgc[n]) for n in every]
    s_dec = [jnp.exp(x) for x in g_last]

    s_cur = [[s_s[s, h] for h in range(nh)] for s in range(ns)]
    hs = [slice(h * hd, (h + 1) * hd) for h in range(nh)]
    for m in range(nc):
        for s in range(ns):
            n = s * nc + m
            st = s_cur[s]
            ws = [_mm(jnp.concatenate([w[n][:, wz + h * hd:wz + (h + 1) * hd], q_dec[n][:, hs[h]]], axis=0),
                      st[h]) for h in range(nh)]
            v_new = jnp.concatenate([w[n][:, hs[h]] - ws[h][0:c] for h in range(nh)], axis=-1)
            o = jnp.concatenate([x[c:2 * c] for x in ws], axis=-1) + _mm(attn[n], bdiag(v_new, bd_cd))
            s_cur[s] = [st[h] * s_dec[n][:, hs[h]] + _mm_tn(k_dec[n][:, hs[h]], v_new[:, hs[h]])
                        for h in range(nh)]
            for h in range(nh):
                z = gd_ref[s, m * c:(m + 1) * c, w3 + h * hd:w3 + (h + 1) * hd]
                y_ref[s, m * c:(m + 1) * c, hs[h]] = _rms(o[:, hs[h]], nw_ref[...]) * (z * _sigmoid(z))
    for s in range(ns):
        for h in range(nh):
            s_s[s, h] = s_cur[s][h]

    @pl.when(i == last)
    def _fin():
        for s in range(ns):
            for h in range(nh):
                s_out[s, h] = s_cur[s][h]
            cv_out[s] = xp_s[s, SUBLANES - 3:SUBLANES, :]


def _gdn_prompt(gd3, p, layer, *, rg, ns):
    nb, length, n_gd = gd3.shape
    nh = p["gdn_nh"]
    hd = p["gdn_nw"].shape[-1]
    w3 = 3 * nh * hd
    consts = [p["gdn_cw"], p["gdn_alog"], p["gdn_dtb"], p["gdn_nw"]]
    return pl.pallas_call(
        functools.partial(_gdn_prompt_kernel, ns=ns, rg=rg, nh=nh, hd=hd),
        grid=(nb // ns, length // rg),
        in_specs=[pl.BlockSpec((ns, rg, n_gd), lambda b, i: (b, i, 0))]
        + [_layer_spec(a, layer) for a in consts],
        out_specs=[pl.BlockSpec((ns, rg, nh * hd), lambda b, i: (b, i, 0)),
                   pl.BlockSpec((ns, nh, hd, hd), lambda b, i: (b, 0, 0, 0)),
                   pl.BlockSpec((ns, 3, w3), lambda b, i: (b, 0, 0))],
        out_shape=[jax.ShapeDtypeStruct((nb, length, nh * hd), F32),
                   jax.ShapeDtypeStruct((nb, nh, hd, hd), F32),
                   jax.ShapeDtypeStruct((nb, 3, w3), F32)],
        scratch_shapes=[pltpu.VMEM((ns, nh, hd, hd), F32), pltpu.VMEM((ns, SUBLANES, w3), F32)],
        compiler_params=pltpu.CompilerParams(dimension_semantics=("arbitrary", "arbitrary"),
                                             vmem_limit_bytes=VMEM_LIMIT),
        name="gdn_prompt",
    )(gd3, *consts)


def _gdn_sample_kernel(*refs, nt, nbk, nh, hd, layer, has_acc):
    gd_ref, cv0_ref, s0_ref, cw_ref, alog_ref, dtb_ref, nw_ref = refs[:7]
    y_ref, s_out, cv_out, wk_s, kd_s, wv_s, ws_s = refs[7 + int(has_acc):]
    if not has_acc:
        for other in range(s_out.shape[0]):
            if other != layer:
                s_out[other] = jnp.zeros(s_out.shape[1:], F32)
        s_out = s_out.at[layer]
    w3 = 3 * nh * hd
    wz = nh * hd
    cw = cw_ref[...]
    scale = hd ** -0.5
    xs = [cv0_ref[j] for j in range(3)] + [gd_ref[t, :, 0:w3] for t in range(nt)]
    for j in range(3):
        cv_out[j] = xs[nt + j]
    q, k, v, gc, beta = [], [], [], [], []
    for t in range(nt):
        xc = xs[t] * cw[0:1] + xs[t + 1] * cw[1:2] + xs[t + 2] * cw[2:3] + xs[t + 3] * cw[3:4]
        xc = xc * _sigmoid(xc)
        ab = gd_ref[t, :, w3 + wz:w3 + wz + LANES]
        g_t = -jnp.exp(alog_ref[...]) * _softplus(ab + dtb_ref[...])
        gc.append(g_t if t == 0 else gc[-1] + g_t)
        beta.append(_sigmoid(ab))
        qn = _l2n_heads(xc[:, 0:wz], nh, hd, scale)
        kn = _l2n_heads(xc[:, wz:2 * wz], nh, hd, 1.0)
        q.append([qn[:, h * hd:(h + 1) * hd] for h in range(nh)])
        k.append([kn[:, h * hd:(h + 1) * hd] for h in range(nh)])
        v.append([xc[:, 2 * wz + h * hd:2 * wz + (h + 1) * hd] for h in range(nh)])

    def dot(a, b):
        return jnp.sum(a * b, axis=-1, keepdims=True)

    wk_rows = [[None] * nh for _ in range(nt)]
    qd_rows = [[None] * nh for _ in range(nt)]
    kd_rows = [[None] * nh for _ in range(nt)]
    wv_rows = [[None] * nh for _ in range(nt)]
    sd_rows = [None] * nh
    attn = [None] * nh
    for h in range(nh):
        gch = [gc[t][:, h:h + 1] for t in range(nt)]
        bh = [beta[t][:, nh + h:nh + h + 1] for t in range(nt)]
        e = [jnp.exp(x) for x in gch]
        kb = [k[t][h] * bh[t] for t in range(nt)]
        vb = [v[t][h] * bh[t] for t in range(nt)]
        kbe = [kb[t] * e[t] for t in range(nt)]
        dec = [[jnp.exp(gch[i] - gch[j]) for j in range(i)] for i in range(nt)]
        m = [[dot(kb[i], k[j][h]) * dec[i][j] for j in range(i)] for i in range(nt)]
        attn[h] = [[dot(q[i][h], k[j][h]) * (dec[i][j] if j < i else 1.0) for j in range(i + 1)]
                   for i in range(nt)]
        tinv = [[None] * nt for _ in range(nt)]
        for j in range(nt):
            for i in range(j + 1, nt):
                acc = m[i][j]
                for l in range(j + 1, i):
                    acc = acc + m[i][l] * tinv[l][j]
                tinv[i][j] = -acc
        for i in range(nt):
            wv_i, wk_i = vb[i], kbe[i]
            for j in range(i):
                wv_i = wv_i + tinv[i][j] * vb[j]
                wk_i = wk_i + tinv[i][j] * kbe[j]
            wv_rows[i][h] = wv_i
            wk_rows[i][h] = wk_i
            qd_rows[i][h] = q[i][h] * e[i]
            kd_rows[i][h] = k[i][h] * jnp.exp(gch[nt - 1] - gch[i])
        sd_rows[h] = jnp.broadcast_to(e[nt - 1], (nbk, hd))

    cat = lambda parts: jnp.concatenate(parts, axis=-1)
    to_seq = lambda slabs: jnp.swapaxes(jnp.stack(slabs, axis=0), 0, 1)
    zero = jnp.zeros((nbk, wz), F32)
    wk_s[...] = to_seq([cat(wk_rows[i]) for i in range(nt)] + [cat(qd_rows[i]) for i in range(nt)])
    kd_s[...] = to_seq([cat(kd_rows[i]) for i in range(nt)] + [zero] * nt)
    wv_s[...] = to_seq([cat(wv_rows[i]) for i in range(nt)] + [cat(sd_rows)] * nt)

    def per_seq(b, _):
        twk_all = wk_s[b]
        tkd_all = kd_s[b]
        twv_all = wv_s[b]
        for h in range(nh):
            cols = slice(h * hd, (h + 1) * hd)
            twk, tkd, twv = twk_all[:, cols], tkd_all[:, cols], twv_all[:, cols]
            a = _mm_tn(tkd, jnp.concatenate([twv, twk], axis=1))
            s = s0_ref[b, h]
            r = _mm(jnp.concatenate([twk, a[:, hd:]], axis=0), s)
            s_out[b, h] = s * twv[nt:nt + 1, :] + a[:, 0:hd] - r[2 * nt:, :]
            ws_s[b, :, cols] = r[0:2 * nt, :]
        return 0

    lax.fori_loop(0, nbk, per_seq, 0, unroll=min(DECODE_SEQ_UNROLL, nbk))

    ws_tm = jnp.swapaxes(ws_s[...], 0, 1)
    for h in range(nh):
        cols = slice(h * hd, (h + 1) * hd)
        v_new = [wv_rows[j][h] - ws_tm[j][:, cols] for j in range(nt)]
        for i in range(nt):
            o = ws_tm[nt + i][:, cols]
            for j in range(i + 1):
                o = o + attn[h][i][j] * v_new[j]
            z = gd_ref[i, :, w3 + h * hd:w3 + (h + 1) * hd]
            y_ref[i, :, cols] = _rms(o, nw_ref[...]) * (z * _sigmoid(z))


def _gdn_sample(gd3, cv0, s_all, s_acc, layer, p, *, nbk):
    nt, nb, n_gd = gd3.shape
    nh = p["gdn_nh"]
    hd = p["gdn_nw"].shape[-1]
    w3 = 3 * nh * hd
    wz = nh * hd
    s_spec = pl.BlockSpec((None, nbk, nh, hd, hd), lambda i: (layer, i, 0, 0, 0))
    consts = [p["gdn_cw"], p["gdn_alog"], p["gdn_dtb"], p["gdn_nw"]]
    args = [gd3, cv0, s_all] + consts
    in_specs = [pl.BlockSpec((nt, nbk, n_gd), lambda i: (0, i, 0)),
                pl.BlockSpec((3, nbk, w3), lambda i: (0, i, 0)),
                s_spec] + [_layer_spec(a, layer) for a in consts]
    aliases = {}
    s_out_spec = s_spec
    if s_acc is not None:
        aliases = {len(args): 1}
        args.append(s_acc)
        in_specs.append(pl.BlockSpec(memory_space=pl.ANY))
    else:
        s_out_spec = pl.BlockSpec((s_all.shape[0], nbk, nh, hd, hd), lambda i: (0, i, 0, 0, 0))
    return pl.pallas_call(
        functools.partial(_gdn_sample_kernel, nt=nt, nbk=nbk, nh=nh, hd=hd, layer=layer,
                          has_acc=s_acc is not None),
        grid=(nb // nbk,),
        in_specs=in_specs,
        out_specs=[pl.BlockSpec((nt, nbk, wz), lambda i: (0, i, 0)),
                   s_out_spec,
                   pl.BlockSpec((3, nbk, w3), lambda i: (0, i, 0))],
        out_shape=[jax.ShapeDtypeStruct((nt, nb, wz), F32),
                   jax.ShapeDtypeStruct(s_all.shape, F32),
                   jax.ShapeDtypeStruct((3, nb, w3), F32)],
        scratch_shapes=[pltpu.VMEM((nbk, 2 * nt, wz), F32) for _ in range(4)],
        input_output_aliases=aliases,
        compiler_params=pltpu.CompilerParams(dimension_semantics=("arbitrary",),
                                             vmem_limit_bytes=VMEM_LIMIT),
        name="gdn_sample",
    )(*args)


def _ffn_kernel(x_ref, ysl_ref, yg_ref, wo_ref, nmp_ref, nfp_ref, nfq_ref, wgu_ref, wd_ref, o_ref,
                *, from_time_major):
    nb, tc, d = x_ref.shape
    rows = nb * tc
    x = x_ref[...].reshape(rows, d)
    ysl = ysl_ref[...]
    if from_time_major:
        ysl = jnp.swapaxes(ysl, 0, 1)
    ysl = ysl.reshape(rows, ysl.shape[-1])
    yg = yg_ref[...].reshape(rows, yg_ref.shape[-1])
    mix = jnp.concatenate([ysl, yg], axis=-1)
    x = x + _rms(_mm(mix, wo_ref[...]), nmp_ref[...])
    h = _rms(x, nfp_ref[...])
    gu = _mm(h, wgu_ref[...])
    hid = gu.shape[1] // 2
    gt = gu[:, :hid]
    f = _mm(gt * _sigmoid(gt) * gu[:, hid:], wd_ref[...])
    x = x + _rms(f, nfq_ref[...])
    o_ref[...] = x.reshape(nb, tc, d)


def _ffn(x3, ysl3, yg3, p, layer, *, tc, from_time_major):
    nb, length, d = x3.shape
    n1 = ysl3.shape[-1]
    n2 = yg3.shape[-1]
    consts = [p["w_out"], p["n_mix_post"], p["n_ffn_pre"], p["n_ffn_post"], p["w_gu"], p["w_down"]]
    if from_time_major:
        ysl_spec = pl.BlockSpec((tc, nb, n1), lambda i: (i, 0, 0))
    else:
        ysl_spec = pl.BlockSpec((nb, tc, n1), lambda i: (0, i, 0))
    return pl.pallas_call(
        functools.partial(_ffn_kernel, from_time_major=from_time_major),
        grid=(length // tc,),
        in_specs=[pl.BlockSpec((nb, tc, d), lambda i: (0, i, 0)), ysl_spec,
                  pl.BlockSpec((nb, tc, n2), lambda i: (0, i, 0))]
        + [_layer_spec(a, layer) for a in consts],
        out_specs=pl.BlockSpec((nb, tc, d), lambda i: (0, i, 0)),
        out_shape=jax.ShapeDtypeStruct((nb, length, d), F32),
        compiler_params=pltpu.CompilerParams(dimension_semantics=("arbitrary",),
                                             vmem_limit_bytes=VMEM_LIMIT),
        name="ffn",
    )(x3, ysl3, yg3, *consts)


def _block_diag(blocks):
    dep, n, r, c = blocks.shape
    eye = jnp.eye(n, dtype=blocks.dtype)
    return (eye[None, :, None, :, None] * blocks[:, :, :, None, :]).reshape(dep, n * r, n * c)


def _params(norm_mix_pre, norm_mix_post, norm_ffn_pre, norm_ffn_post, w_in, s5_lambda_re,
            s5_lambda_im, s5_log_dt, s5_b_re, s5_b_im, s5_c_re, s5_c_im, s5_d, s5_w_glu, s5_b_glu,
            lru_conv_w, lru_conv_b, lru_w_a, lru_b_a, lru_w_x, lru_b_x, lru_lambda, gdn_conv_w,
            gdn_a_log, gdn_dt_bias, gdn_norm_w, w_out, ffn_w_gate_up, ffn_w_down):
    dep, g, pst, hgrp = s5_b_re.shape
    s5w = g * hgrp
    lw = lru_lambda.shape[1]
    nh = gdn_a_log.shape[1]
    row = lambda v: v[:, None, :]
    lanes = lambda v: jnp.pad(v, ((0, 0), (0, LANES - v.shape[1])))[:, None, :]
    return {
        "n_sl": s5w + 2 * lw,
        "n_mix_pre": row(norm_mix_pre), "n_mix_post": row(norm_mix_post),
        "n_ffn_pre": row(norm_ffn_pre), "n_ffn_post": row(norm_ffn_post),
        "w_in": w_in.astype(BF16),
        "s5_lre": s5_lambda_re.reshape(dep, 1, g * pst), "s5_lim": s5_lambda_im.reshape(dep, 1, g * pst),
        "s5_ldt": jnp.repeat(s5_log_dt, pst, axis=1)[:, None, :],
        "s5_bre": _block_diag(jnp.swapaxes(s5_b_re, 2, 3)),
        "s5_bim": _block_diag(jnp.swapaxes(s5_b_im, 2, 3)),
        "s5_cre": _block_diag(jnp.swapaxes(s5_c_re, 2, 3)),
        "s5_cim": _block_diag(jnp.swapaxes(s5_c_im, 2, 3)),
        "s5_d": row(s5_d), "s5_wglu": s5_w_glu.astype(BF16), "s5_bglu": row(s5_b_glu),
        "lru_cw": lru_conv_w, "lru_cb": row(lru_conv_b),
        "lru_wa": _block_diag(lru_w_a).astype(BF16), "lru_ba": row(lru_b_a),
        "lru_wx": _block_diag(lru_w_x).astype(BF16), "lru_bx": row(lru_b_x),
        "lru_lambda": row(lru_lambda),
        "gdn_cw": gdn_conv_w, "gdn_alog": lanes(gdn_a_log),
        "gdn_dtb": lanes(gdn_dt_bias), "gdn_nw": row(gdn_norm_w), "gdn_nh": nh,
        "w_out": w_out.astype(BF16), "w_gu": ffn_w_gate_up.astype(BF16),
        "w_down": ffn_w_down.astype(BF16),
    }


def _prompt_layer(x, p, layer, *, tc, rg):
    nb, length, _ = x.shape
    sl_tm, gd = _in_proj(x, p, layer, tc=tc, to_time_major=True)
    lw = p["lru_lambda"].shape[-1]
    ysl, xr, xi, hl, cv = _scan(sl_tm.reshape(length * nb, -1), p, layer, None, nb=nb, tc=tc,
                                first_pos_is_zero=True)
    yg, s_new, gcv = _gdn_prompt(gd, p, layer, rg=rg, ns=GDN_SEQS_PER_STEP if nb % GDN_SEQS_PER_STEP == 0 else 1)
    x = _ffn(x, ysl.reshape(length, nb, -1), yg, p, layer, tc=tc, from_time_major=True)
    lcv = jnp.swapaxes(cv.reshape(3, nb, lw), 0, 1)
    return x, (xr, xi, hl, lcv, s_new, gcv)


def _sample_layer(x_tm, st, gdn_s_all, gdn_s_acc, layer, p, *, nbk):
    s5_re, s5_im, lru_h, lru_conv, gdn_conv = st
    nb = s5_re.shape[0]
    rows = x_tm.shape[1]
    nt = rows // nb
    sl, gd = _in_proj(x_tm, p, layer, tc=rows, to_time_major=False)
    lw = p["lru_lambda"].shape[-1]
    cv0 = jnp.swapaxes(lru_conv, 0, 1).reshape(3 * nb, lw)
    states = [s5_re.reshape(nb, -1), s5_im.reshape(nb, -1), lru_h, cv0]
    ysl, xr, xi, hl, cv = _scan(sl.reshape(rows, -1), p, layer, states, nb=nb, tc=nt,
                                first_pos_is_zero=(PAST_LEN == 0))
    yg, s_acc, gcv = _gdn_sample(gd.reshape(nt, nb, -1), jnp.swapaxes(gdn_conv, 0, 1), gdn_s_all, gdn_s_acc,
                                 layer, p, nbk=nbk)
    x_tm = _ffn(x_tm, ysl.reshape(1, rows, -1), yg.reshape(1, rows, -1), p, layer, tc=rows,
                from_time_major=False)
    lcv = jnp.swapaxes(cv.reshape(3, nb, lw), 0, 1)
    return x_tm, (xr, xi, hl, lcv, jnp.swapaxes(gcv, 0, 1)), s_acc


def kernel(x_prompt, x_sample, state_s5_re, state_s5_im, state_lru_h, state_lru_conv, state_gdn_S, state_gdn_conv, norm_mix_pre, norm_mix_post, norm_ffn_pre, norm_ffn_post, w_in, s5_lambda_re, s5_lambda_im, s5_log_dt, s5_b_re, s5_b_im, s5_c_re, s5_c_im, s5_d, s5_w_glu, s5_b_glu, lru_conv_w, lru_conv_b, lru_w_a, lru_b_a, lru_w_x, lru_b_x, lru_lambda, gdn_conv_w, gdn_a_log, gdn_dt_bias, gdn_norm_w, w_out, ffn_w_gate_up, ffn_w_down):
    depth = w_in.shape[0]
    bp, lp, d = x_prompt.shape
    bs, ls, _ = x_sample.shape
    g, pst = s5_lambda_re.shape[1:]
    tc = min(TOKENS_PER_SEQ_STEP, lp)
    rg = min(GDN_ROWS_PER_STEP, lp)
    y_p = x_prompt
    y_s = jnp.swapaxes(x_sample, 0, 1).reshape(1, ls * bs, d)
    new_p = [[] for _ in range(6)]
    new_s = [[] for _ in range(5)]
    s_gdn_s = None
    p = _params(norm_mix_pre, norm_mix_post, norm_ffn_pre, norm_ffn_post, w_in, s5_lambda_re,
                s5_lambda_im, s5_log_dt, s5_b_re, s5_b_im, s5_c_re, s5_c_im, s5_d, s5_w_glu,
                s5_b_glu, lru_conv_w, lru_conv_b, lru_w_a, lru_b_a, lru_w_x, lru_b_x, lru_lambda,
                gdn_conv_w, gdn_a_log, gdn_dt_bias, gdn_norm_w, w_out, ffn_w_gate_up, ffn_w_down)
    for l in range(depth):
        y_p, sp = _prompt_layer(y_p, p, l, tc=tc, rg=rg)
        st_s = (state_s5_re[l], state_s5_im[l], state_lru_h[l], state_lru_conv[l], state_gdn_conv[l])
        y_s, ss, s_gdn_s = _sample_layer(y_s, st_s, state_gdn_S, s_gdn_s, l, p,
                                         nbk=min(DECODE_SEQS_PER_STEP, bs))
        for j in range(6):
            new_p[j].append(sp[j])
        for j in range(5):
            new_s[j].append(ss[j])
    y_s = jnp.swapaxes(y_s.reshape(ls, bs, d), 0, 1)
    outs_p = [jnp.stack(t, axis=0) for t in new_p]
    outs_s = [jnp.stack(t, axis=0) for t in new_s]
    outs_s.insert(4, s_gdn_s)
    outs_p[0] = outs_p[0].reshape(depth, bp, g, pst)
    outs_p[1] = outs_p[1].reshape(depth, bp, g, pst)
    outs_s[0] = outs_s[0].reshape(depth, bs, g, pst)
    outs_s[1] = outs_s[1].reshape(depth, bs, g, pst)
    return (y_p, y_s, *outs_p, *outs_s)
```

```python
import functools

import jax
import jax.numpy as jnp
from jax import lax
from jax.experimental import pallas as pl
from jax.experimental.pallas import tpu as pltpu

F32 = jnp.float32
BF16 = jnp.bfloat16
HIGHEST = lax.Precision.HIGHEST

NORM_EPS = 1e-6
LRU_C = 8.0
GDN_CHUNK = 64
PAST_LEN = 16384
LANES = 128
SUBLANES = 8
VMEM_LIMIT = 56 * 1024 * 1024

TOKENS_PER_SEQ_STEP = 64
IN_PROJ_TOKENS_PER_SEQ_STEP = 128
GDN_ROWS_PER_STEP = 4 * GDN_CHUNK
GDN_SEQS_PER_STEP = 4
DECODE_SEQS_PER_STEP = 16
DECODE_SEQ_UNROLL = 4


def _rms(x, w):
    return x * lax.rsqrt(jnp.mean(x * x, axis=-1, keepdims=True) + NORM_EPS) * w


def _sigmoid(x):
    return jax.nn.sigmoid(x)


def _softplus(x):
    return jnp.maximum(x, 0.0) + jnp.log1p(jnp.exp(-jnp.abs(x)))


def _mm(a, b):
    return jnp.dot(a.astype(BF16), b.astype(BF16), preferred_element_type=F32)


def _mm_nt(a, b):
    return lax.dot_general(a.astype(BF16), b.astype(BF16), (((1,), (1,)), ((), ())),
                           preferred_element_type=F32)


def _mm_tn(a, b):
    return lax.dot_general(a.astype(BF16), b.astype(BF16), (((0,), (0,)), ((), ())),
                           preferred_element_type=F32)


def _const_spec(shape):
    n = len(shape)
    return pl.BlockSpec(shape, lambda *_: (0,) * n, pipeline_mode=pl.Buffered(1))


def _layer_spec(arr, layer):
    n = arr.ndim - 1
    return pl.BlockSpec((None,) + arr.shape[1:], lambda *_: (layer,) + (0,) * n,
                        pipeline_mode=pl.Buffered(1))


def _l2n_heads(x, nh, hd, mul):
    def one(h):
        xh = x[:, h * hd:(h + 1) * hd]
        return xh * (lax.rsqrt(jnp.sum(xh * xh, axis=-1, keepdims=True) + NORM_EPS) * mul)
    return jnp.concatenate([one(h) for h in range(nh)], axis=-1)


def _in_proj_kernel(x_ref, nw_ref, w_ref, sl_ref, gd_ref, *, n_sl, w3, n_ab, to_time_major):
    nb, tc, d = x_ref.shape
    rows = nb * tc
    x = x_ref[...].reshape(rows, d)
    h = _rms(x, nw_ref[...])
    p = _mm(h, w_ref[...])
    sl = p[:, :n_sl].reshape(nb, tc, n_sl)
    if to_time_major:
        sl = jnp.swapaxes(sl, 0, 1)
    sl_ref[...] = sl
    c_ab = n_sl + w3
    c_z = c_ab + n_ab
    wz = p.shape[1] - c_z
    gd_ref[:, :, 0:w3] = p[:, n_sl:c_ab].reshape(nb, tc, w3)
    gd_ref[:, :, w3:w3 + wz] = p[:, c_z:].reshape(nb, tc, wz)
    ab = jnp.concatenate([p[:, c_ab:c_z], jnp.zeros((rows, LANES - n_ab), F32)], axis=1)
    gd_ref[:, :, w3 + wz:] = ab.reshape(nb, tc, LANES)


def _in_proj(x3, p, layer, *, tc, to_time_major):
    nb, length, d = x3.shape
    nw, w, n_sl = p["n_mix_pre"], p["w_in"], p["n_sl"]
    n_tot = w.shape[-1]
    nh = p["gdn_nh"]
    w3 = 3 * nh * p["gdn_nw"].shape[-1]
    n_gd = n_tot - n_sl - 2 * nh + LANES
    if to_time_major:
        sl_shape, sl_spec = (length, nb, n_sl), pl.BlockSpec((tc, nb, n_sl), lambda i: (i, 0, 0))
    else:
        sl_shape, sl_spec = (nb, length, n_sl), pl.BlockSpec((nb, tc, n_sl), lambda i: (0, i, 0))
    return pl.pallas_call(
        functools.partial(_in_proj_kernel, n_sl=n_sl, w3=w3, n_ab=2 * nh, to_time_major=to_time_major),
        grid=(length // tc,),
        in_specs=[pl.BlockSpec((nb, tc, d), lambda i: (0, i, 0)),
                  _layer_spec(nw, layer), _layer_spec(w, layer)],
        out_specs=[sl_spec, pl.BlockSpec((nb, tc, n_gd), lambda i: (0, i, 0))],
        out_shape=[jax.ShapeDtypeStruct(sl_shape, F32), jax.ShapeDtypeStruct((nb, length, n_gd), F32)],
        compiler_params=pltpu.CompilerParams(dimension_semantics=("arbitrary",),
                                             vmem_limit_bytes=VMEM_LIMIT),
        name="in_proj",
    )(x3, nw, w)


def _scan_kernel(*refs, nb, tc, s5w, lw, first_pos_is_zero, zero_state):
    (sl_ref, lre_ref, lim_ref, ldt_ref, bre_ref, bim_ref, cre_ref, cim_ref, d_ref,
     wglu_ref, bglu_ref, cw_ref, cb_ref, wa_ref, ba_ref, wx_ref, bx_ref, lam_ref) = refs[:18]
    n_in = 18 if zero_state else 22
    (y_ref, xr_out, xi_out, h_out, cv_out,
     ar_s, ai_s, bbr_s, bbi_s, crb_s, cib_s, xr_c, xi_c, h_c, xr_s, xi_s, xpad_s, a_s, b_s) = refs[n_in:]
    i = pl.program_id(0)
    last = pl.num_programs(0) - 1
    rows = nb * tc

    @pl.when(i == 0)
    def _init():
        lr = jnp.minimum(lre_ref[...], -1e-4)
        li = lim_ref[...]
        dt = jnp.exp(ldt_ref[...])
        mag = jnp.exp(lr * dt)
        ar = mag * jnp.cos(li * dt)
        ai = mag * jnp.sin(li * dt)
        den = lr * lr + li * li
        fr = ((ar - 1.0) * lr + ai * li) / den
        fi = (ai * lr - (ar - 1.0) * li) / den
        ar_s[...] = ar
        ai_s[...] = ai
        bre = bre_ref[...]
        bim = bim_ref[...]
        bbr_s[...] = (fr * bre - fi * bim).astype(BF16)
        bbi_s[...] = (fr * bim + fi * bre).astype(BF16)
        crb_s[...] = cre_ref[...].astype(BF16)
        cib_s[...] = cim_ref[...].astype(BF16)
        if zero_state:
            xr_c[...] = jnp.zeros_like(xr_c)
            xi_c[...] = jnp.zeros_like(xi_c)
            h_c[...] = jnp.zeros_like(h_c)
            xpad_s[0:3 * nb, :] = jnp.zeros((3 * nb, lw), F32)
        else:
            xr0_ref, xi0_ref, h0_ref, cv0_ref = refs[18:22]
            xr_c[...] = xr0_ref[...]
            xi_c[...] = xi0_ref[...]
            h_c[...] = h0_ref[...]
            xpad_s[0:3 * nb, :] = cv0_ref[...]

    u = sl_ref[:, 0:s5w]
    xb = sl_ref[:, s5w:s5w + lw]
    gate = sl_ref[:, s5w + lw:s5w + 2 * lw]

    ub = u.astype(BF16)
    xr_s[...] = jnp.dot(ub, bbr_s[...], preferred_element_type=F32)
    xi_s[...] = jnp.dot(ub, bbi_s[...], preferred_element_type=F32)

    xpad_s[3 * nb:3 * nb + rows, :] = xb
    cw = cw_ref[...]
    xc = (xpad_s[0:rows, :] * cw[0:1] + xpad_s[nb:nb + rows, :] * cw[1:2]
          + xpad_s[2 * nb:2 * nb + rows, :] * cw[2:3] + xb * cw[3:4]) + cb_ref[...]
    new_cv = xpad_s[rows:rows + 3 * nb, :]
    xpad_s[0:3 * nb, :] = new_cv
    r = _sigmoid(_mm(xc, wa_ref[...]) + ba_ref[...])
    gi = _sigmoid(_mm(xc, wx_ref[...]) + bx_ref[...])
    log_a = -LRU_C * r * _softplus(-lam_ref[...])
    a = jnp.exp(log_a)
    m2 = -jnp.tanh(log_a) * (a * a + 1.0)
    mult = jnp.where(m2 > 0.0, m2 * lax.rsqrt(m2), 0.0)
    if first_pos_is_zero:
        rid = lax.broadcasted_iota(jnp.int32, (rows, 1), 0)
        mult = jnp.where(jnp.logical_and(rid < nb, i == 0), 1.0, mult)
    a_s[...] = a
    b_s[...] = mult * gi * xc

    ar = jnp.broadcast_to(ar_s[...], (nb, ar_s.shape[1]))
    ai = jnp.broadcast_to(ai_s[...], (nb, ai_s.shape[1]))

    def step(t, carry):
        xr, xi, h = carry
        sl_t = pl.ds(pl.multiple_of(t * nb, nb), nb)
        nxr = ar * xr - ai * xi + xr_s[sl_t, :]
        nxi = ar * xi + ai * xr + xi_s[sl_t, :]
        nh = a_s[sl_t, :] * h + b_s[sl_t, :]
        xr_s[sl_t, :] = nxr
        xi_s[sl_t, :] = nxi
        b_s[sl_t, :] = nh
        return nxr, nxi, nh

    xr, xi, h = lax.fori_loop(0, tc, step, (xr_c[...], xi_c[...], h_c[...]),
                              unroll=(True if tc <= 8 else 8))
    xr_c[...] = xr
    xi_c[...] = xi
    h_c[...] = h

    y = (jnp.dot(xr_s[...].astype(BF16), crb_s[...], preferred_element_type=F32)
         - jnp.dot(xi_s[...].astype(BF16), cib_s[...], preferred_element_type=F32))
    y = jax.nn.gelu(y + d_ref[...] * u)
    y = y * _sigmoid(_mm(y, wglu_ref[...]) + bglu_ref[...])
    y_ref[:, 0:s5w] = y
    y_ref[:, s5w:s5w + lw] = b_s[...] * jax.nn.gelu(gate)

    @pl.when(i == last)
    def _fin():
        xr_out[...] = xr
        xi_out[...] = xi
        h_out[...] = h
        cv_out[...] = new_cv


def _scan(sl2, p, layer, states, *, nb, tc, first_pos_is_zero):
    rows_total, n_sl = sl2.shape
    s5w = p["s5_d"].shape[-1]
    lw = p["lru_lambda"].shape[-1]
    n_state = p["s5_lre"].shape[-1]
    rows = nb * tc
    params = [p["s5_lre"], p["s5_lim"], p["s5_ldt"], p["s5_bre"], p["s5_bim"], p["s5_cre"], p["s5_cim"],
              p["s5_d"], p["s5_wglu"], p["s5_bglu"], p["lru_cw"], p["lru_cb"], p["lru_wa"], p["lru_ba"],
              p["lru_wx"], p["lru_bx"], p["lru_lambda"]]
    zero_state = states is None
    states = [] if zero_state else list(states)
    return pl.pallas_call(
        functools.partial(_scan_kernel, nb=nb, tc=tc, s5w=s5w, lw=lw, first_pos_is_zero=first_pos_is_zero,
                          zero_state=zero_state),
        grid=(rows_total // rows,),
        in_specs=[pl.BlockSpec((rows, n_sl), lambda i: (i, 0))]
        + [_layer_spec(a, layer) for a in params] + [_const_spec(a.shape) for a in states],
        out_specs=[pl.BlockSpec((rows, s5w + lw), lambda i: (i, 0)),
                   pl.BlockSpec((nb, n_state), lambda i: (0, 0)),
                   pl.BlockSpec((nb, n_state), lambda i: (0, 0)),
                   pl.BlockSpec((nb, lw), lambda i: (0, 0)),
                   pl.BlockSpec((3 * nb, lw), lambda i: (0, 0))],
        out_shape=[jax.ShapeDtypeStruct((rows_total, s5w + lw), F32),
                   jax.ShapeDtypeStruct((nb, n_state), F32),
                   jax.ShapeDtypeStruct((nb, n_state), F32),
                   jax.ShapeDtypeStruct((nb, lw), F32),
                   jax.ShapeDtypeStruct((3 * nb, lw), F32)],
        scratch_shapes=[pltpu.VMEM((1, n_state), F32), pltpu.VMEM((1, n_state), F32),
                        pltpu.VMEM((s5w, n_state), BF16), pltpu.VMEM((s5w, n_state), BF16),
                        pltpu.VMEM((n_state, s5w), BF16), pltpu.VMEM((n_state, s5w), BF16),
                        pltpu.VMEM((nb, n_state), F32), pltpu.VMEM((nb, n_state), F32),
                        pltpu.VMEM((nb, lw), F32),
                        pltpu.VMEM((rows, n_state), F32), pltpu.VMEM((rows, n_state), F32),
                        pltpu.VMEM((rows + 3 * nb, lw), F32),
                        pltpu.VMEM((rows, lw), F32), pltpu.VMEM((rows, lw), F32)],
        compiler_params=pltpu.CompilerParams(dimension_semantics=("arbitrary",),
                                             vmem_limit_bytes=VMEM_LIMIT),
        name="scan",
    )(sl2, *params, *states)


def _gdn_prompt_kernel(gd_ref, cw_ref, alog_ref, dtb_ref, nw_ref, y_ref, s_out, cv_out,
                       s_s, xp_s, *, ns, rg, nh, hd):
    i = pl.program_id(1)
    last = pl.num_programs(1) - 1
    w3 = 3 * nh * hd
    wz = nh * hd
    c = GDN_CHUNK
    nc = rg // c
    wc = nh * c
    rt = ns * rg

    @pl.when(i == 0)
    def _init():
        s_s[...] = jnp.zeros_like(s_s)
        xp_s[...] = jnp.zeros_like(xp_s)

    cw = cw_ref[...]
    xcs = []
    for s in range(ns):
        qkv = gd_ref[s, :, 0:w3]
        xe = jnp.concatenate([xp_s[s], qkv], axis=0)
        acc = qkv * cw[3:4]
        for j in range(1, 4):
            acc = acc + pltpu.roll(xe, j, 0)[SUBLANES:, :] * cw[3 - j:4 - j]
        xp_s[s] = qkv[rg - SUBLANES:, :]
        xcs.append(acc)
    xc = jnp.concatenate(xcs, axis=0)
    xc = xc * _sigmoid(xc)

    ab = jnp.concatenate([gd_ref[s, :, w3 + wz:w3 + wz + LANES] for s in range(ns)], axis=0)
    g_all = -jnp.exp(alog_ref[...]) * _softplus(ab + dtb_ref[...])
    beta_all = _sigmoid(ab)

    def per_head(fn):
        return jnp.concatenate([fn(h) for h in range(nh)], axis=-1)

    q_all = _l2n_heads(xc[:, 0:wz], nh, hd, hd ** -0.5)
    k_all = _l2n_heads(xc[:, wz:2 * wz], nh, hd, 1.0)
    v_all = xc[:, 2 * wz:3 * wz]
    beta_w = per_head(lambda h: jnp.broadcast_to(beta_all[:, nh + h:nh + h + 1], (rt, hd)))
    kb_all = k_all * beta_w
    vb_all = v_all * beta_w

    ri = lax.broadcasted_iota(jnp.int32, (c, wc), 0)
    cj = lax.broadcasted_iota(jnp.int32, (c, wc), 1) % c
    incl = ri >= cj
    strict = ri > cj
    eye = (ri == cj).astype(F32)
    r1 = lax.broadcasted_iota(jnp.int32, (c, c), 0)
    c1 = lax.broadcasted_iota(jnp.int32, (c, c), 1)
    ltri = (r1 >= c1).astype(F32)
    bd_cc = (lax.broadcasted_iota(jnp.int32, (wc, wc), 0) // c
             == lax.broadcasted_iota(jnp.int32, (wc, wc), 1) // c)
    bd_cd = (lax.broadcasted_iota(jnp.int32, (wc, wz), 0) // c
             == lax.broadcasted_iota(jnp.int32, (wc, wz), 1) // hd)

    def bdiag(x, mask):
        return jnp.where(mask, jnp.concatenate([x] * nh, axis=0), 0.0).astype(BF16)

    nc_all = ns * nc
    rows = [slice(cc * c, (cc + 1) * c) for cc in range(nc_all)]
    gcum = [jnp.dot(ltri, g_all[r], precision=HIGHEST, preferred_element_type=F32) for r in rows]
    cgc = [per_head(lambda h, x=x: jnp.broadcast_to(x[:, h:h + 1], (c, hd))) for x in gcum]
    gcol = [per_head(lambda h, x=x: jnp.broadcast_to(x[:, h:h + 1], (c, c))) for x in gcum]
    grow = [jnp.concatenate([xt[h:h + 1, :] for h in range(nh)], axis=1)
            for xt in [x.T for x in gcum]]
    every = range(nc_all)
    decay = [jnp.exp(jnp.where(incl, gcol[n] - grow[n], -jnp.inf)) for n in every]
    eg = [jnp.exp(x) for x in cgc]
    g_last = [x[c - 1:c, :] for x in cgc]
    kk = [_mm_nt(jnp.concatenate([kb_all[r], q_all[r]], axis=0), bdiag(k_all[r], bd_cd)) for r in rows]
    attn = [kk[n][c:2 * c] * decay[n] for n in every]
    rj = [-jnp.where(strict, kk[n][0:c] * decay[n], 0.0) for n in every]
    sj = [eye + rj[n] for n in every]
    rj = [_mm(rj[n], bdiag(rj[n], bd_cc)) for n in every]
    for _ in range(4):
        rs = [_mm(jnp.concatenate([rj[n], sj[n]], axis=0), bdiag(rj[n], bd_cc)) for n in every]
        rj = [x[0:c] for x in rs]
        sj = [sj[n] + rs[n][c:2 * c] for n in every]
    sj = [sj[n] + _mm(sj[n], bdiag(rj[n], bd_cc)) for n in every]
    w = [_mm(sj[n], jnp.concatenate([bdiag(vb_all[rows[n]], bd_cd),
                                     bdiag(kb_all[rows[n]] * eg[n], bd_cd)], axis=1)) for n in every]
    q_dec = [q_all[rows[n]] * eg[n] for n in every]
    k_dec = [k_all[rows[n]] * jnp.exp(g_last[n] - cgc[n]) for n in every]
    s_dec = [jnp.exp(x) for x in g_last]

    s_cur = [[s_s[s, h] for h in range(nh)] for s in range(ns)]
    hs = [slice(h * hd, (h + 1) * hd) for h in range(nh)]
    for m in range(nc):
        for s in range(ns):
            n = s * nc + m
            st = s_cur[s]
            ws = [_mm(jnp.concatenate([w[n][:, wz + h * hd:wz + (h + 1) * hd], q_dec[n][:, hs[h]]], axis=0),
                      st[h]) for h in range(nh)]
            v_new = jnp.concatenate([w[n][:, hs[h]] - ws[h][0:c] for h in range(nh)], axis=-1)
            o = jnp.concatenate([x[c:2 * c] for x in ws], axis=-1) + _mm(attn[n], bdiag(v_new, bd_cd))
            s_cur[s] = [st[h] * s_dec[n][:, hs[h]] + _mm_tn(k_dec[n][:, hs[h]], v_new[:, hs[h]])
                        for h in range(nh)]
            for h in range(nh):
                z = gd_ref[s, m * c:(m + 1) * c, w3 + h * hd:w3 + (h + 1) * hd]
                y_ref[s, m * c:(m + 1) * c, hs[h]] = _rms(o[:, hs[h]], nw_ref[...]) * (z * _sigmoid(z))
    for s in range(ns):
        for h in range(nh):
            s_s[s, h] = s_cur[s][h]

    @pl.when(i == last)
    def _fin():
        for s in range(ns):
            for h in range(nh):
                s_out[s, h] = s_cur[s][h]
            cv_out[s] = xp_s[s, SUBLANES - 3:SUBLANES, :]


def _gdn_prompt(gd3, p, layer, *, rg, ns):
    nb, length, n_gd = gd3.shape
    nh = p["gdn_nh"]
    hd = p["gdn_nw"].shape[-1]
    w3 = 3 * nh * hd
    consts = [p["gdn_cw"], p["gdn_alog"], p["gdn_dtb"], p["gdn_nw"]]
    return pl.pallas_call(
        functools.partial(_gdn_prompt_kernel, ns=ns, rg=rg, nh=nh, hd=hd),
        grid=(nb // ns, length // rg),
        in_specs=[pl.BlockSpec((ns, rg, n_gd), lambda b, i: (b, i, 0))]
        + [_layer_spec(a, layer) for a in consts],
        out_specs=[pl.BlockSpec((ns, rg, nh * hd), lambda b, i: (b, i, 0)),
                   pl.BlockSpec((ns, nh, hd, hd), lambda b, i: (b, 0, 0, 0)),
                   pl.BlockSpec((ns, 3, w3), lambda b, i: (b, 0, 0))],
        out_shape=[jax.ShapeDtypeStruct((nb, length, nh * hd), F32),
                   jax.ShapeDtypeStruct((nb, nh, hd, hd), F32),
                   jax.ShapeDtypeStruct((nb, 3, w3), F32)],
        scratch_shapes=[pltpu.VMEM((ns, nh, hd, hd), F32), pltpu.VMEM((ns, SUBLANES, w3), F32)],
        compiler_params=pltpu.CompilerParams(dimension_semantics=("arbitrary", "arbitrary"),
                                             vmem_limit_bytes=VMEM_LIMIT),
        name="gdn_prompt",
    )(gd3, *consts)


def _gdn_sample_kernel(*refs, nt, nbk, nh, hd, layer, has_acc):
    gd_ref, cv0_ref, s0_ref, cw_ref, alog_ref, dtb_ref, nw_ref = refs[:7]
    y_ref, s_out, cv_out, wk_s, kd_s, wv_s, ws_s = refs[7 + int(has_acc):]
    if not has_acc:
        for other in range(s_out.shape[0]):
            if other != layer:
                s_out[other] = jnp.zeros(s_out.shape[1:], F32)
        s_out = s_out.at[layer]
    w3 = 3 * nh * hd
    wz = nh * hd
    cw = cw_ref[...]
    scale = hd ** -0.5
    xs = [cv0_ref[j] for j in range(3)] + [gd_ref[t, :, 0:w3] for t in range(nt)]
    for j in range(3):
        cv_out[j] = xs[nt + j]
    q, k, v, gc, beta = [], [], [], [], []
    for t in range(nt):
        xc = xs[t] * cw[0:1] + xs[t + 1] * cw[1:2] + xs[t + 2] * cw[2:3] + xs[t + 3] * cw[3:4]
        xc = xc * _sigmoid(xc)
        ab = gd_ref[t, :, w3 + wz:w3 + wz + LANES]
        g_t = -jnp.exp(alog_ref[...]) * _softplus(ab + dtb_ref[...])
        gc.append(g_t if t == 0 else gc[-1] + g_t)
        beta.append(_sigmoid(ab))
        qn = _l2n_heads(xc[:, 0:wz], nh, hd, scale)
        kn = _l2n_heads(xc[:, wz:2 * wz], nh, hd, 1.0)
        q.append([qn[:, h * hd:(h + 1) * hd] for h in range(nh)])
        k.append([kn[:, h * hd:(h + 1) * hd] for h in range(nh)])
        v.append([xc[:, 2 * wz + h * hd:2 * wz + (h + 1) * hd] for h in range(nh)])

    def dot(a, b):
        return jnp.sum(a * b, axis=-1, keepdims=True)

    wk_rows = [[None] * nh for _ in range(nt)]
    qd_rows = [[None] * nh for _ in range(nt)]
    kd_rows = [[None] * nh for _ in range(nt)]
    wv_rows = [[None] * nh for _ in range(nt)]
    sd_rows = [None] * nh
    attn = [None] * nh
    for h in range(nh):
        gch = [gc[t][:, h:h + 1] for t in range(nt)]
        bh = [beta[t][:, nh + h:nh + h + 1] for t in range(nt)]
        e = [jnp.exp(x) for x in gch]
        kb = [k[t][h] * bh[t] for t in range(nt)]
        vb = [v[t][h] * bh[t] for t in range(nt)]
        kbe = [kb[t] * e[t] for t in range(nt)]
        dec = [[jnp.exp(gch[i] - gch[j]) for j in range(i)] for i in range(nt)]
        m = [[dot(kb[i], k[j][h]) * dec[i][j] for j in range(i)] for i in range(nt)]
        attn[h] = [[dot(q[i][h], k[j][h]) * (dec[i][j] if j < i else 1.0) for j in range(i + 1)]
                   for i in range(nt)]
        tinv = [[None] * nt for _ in range(nt)]
        for j in range(nt):
            for i in range(j + 1, nt):
                acc = m[i][j]
                for l in range(j + 1, i):
                    acc = acc + m[i][l] * tinv[l][j]
                tinv[i][j] = -acc
        for i in range(nt):
            wv_i, wk_i = vb[i], kbe[i]
            for j in range(i):
                wv_i = wv_i + tinv[i][j] * vb[j]
                wk_i = wk_i + tinv[i][j] * kbe[j]
            wv_rows[i][h] = wv_i
            wk_rows[i][h] = wk_i
            qd_rows[i][h] = q[i][h] * e[i]
            kd_rows[i][h] = k[i][h] * jnp.exp(gch[nt - 1] - gch[i])
        sd_rows[h] = jnp.broadcast_to(e[nt - 1], (nbk, hd))

    cat = lambda parts: jnp.concatenate(parts, axis=-1)
    to_seq = lambda slabs: jnp.swapaxes(jnp.stack(slabs, axis=0), 0, 1)
    zero = jnp.zeros((nbk, wz), F32)
    wk_s[...] = to_seq([cat(wk_rows[i]) for i in range(nt)] + [cat(qd_rows[i]) for i in range(nt)])
    kd_s[...] = to_seq([cat(kd_rows[i]) for i in range(nt)] + [zero] * nt)
    wv_s[...] = to_seq([cat(wv_rows[i]) for i in range(nt)] + [cat(sd_rows)] * nt)

    def per_seq(b, _):
        twk_all = wk_s[b]
        tkd_all = kd_s[b]
        twv_all = wv_s[b]
        for h in range(nh):
            cols = slice(h * hd, (h + 1) * hd)
            twk, tkd, twv = twk_all[:, cols], tkd_all[:, cols], twv_all[:, cols]
            a = _mm_tn(tkd, jnp.concatenate([twv, twk], axis=1))
            s = s0_ref[b, h]
            r = _mm(jnp.concatenate([twk, a[:, hd:]], axis=0), s)
            s_out[b, h] = s * twv[nt:nt + 1, :] + a[:, 0:hd] - r[2 * nt:, :]
            ws_s[b, :, cols] = r[0:2 * nt, :]
        return 0

    lax.fori_loop(0, nbk, per_seq, 0, unroll=min(DECODE_SEQ_UNROLL, nbk))

    ws_tm = jnp.swapaxes(ws_s[...], 0, 1)
    for h in range(nh):
        cols = slice(h * hd, (h + 1) * hd)
        v_new = [wv_rows[j][h] - ws_tm[j][:, cols] for j in range(nt)]
        for i in range(nt):
            o = ws_tm[nt + i][:, cols]
            for j in range(i + 1):
                o = o + attn[h][i][j] * v_new[j]
            z = gd_ref[i, :, w3 + h * hd:w3 + (h + 1) * hd]
            y_ref[i, :, cols] = _rms(o, nw_ref[...]) * (z * _sigmoid(z))


def _gdn_sample(gd3, cv0, s_all, s_acc, layer, p, *, nbk):
    nt, nb, n_gd = gd3.shape
    nh = p["gdn_nh"]
    hd = p["gdn_nw"].shape[-1]
    w3 = 3 * nh * hd
    wz = nh * hd
    s_spec = pl.BlockSpec((None, nbk, nh, hd, hd), lambda i: (layer, i, 0, 0, 0))
    consts = [p["gdn_cw"], p["gdn_alog"], p["gdn_dtb"], p["gdn_nw"]]
    args = [gd3, cv0, s_all] + consts
    in_specs = [pl.BlockSpec((nt, nbk, n_gd), lambda i: (0, i, 0)),
                pl.BlockSpec((3, nbk, w3), lambda i: (0, i, 0)),
                s_spec] + [_layer_spec(a, layer) for a in consts]
    aliases = {}
    s_out_spec = s_spec
    if s_acc is not None:
        aliases = {len(args): 1}
        args.append(s_acc)
        in_specs.append(pl.BlockSpec(memory_space=pl.ANY))
    else:
        s_out_spec = pl.BlockSpec((s_all.shape[0], nbk, nh, hd, hd), lambda i: (0, i, 0, 0, 0))
    return pl.pallas_call(
        functools.partial(_gdn_sample_kernel, nt=nt, nbk=nbk, nh=nh, hd=hd, layer=layer,
                          has_acc=s_acc is not None),
        grid=(nb // nbk,),
        in_specs=in_specs,
        out_specs=[pl.BlockSpec((nt, nbk, wz), lambda i: (0, i, 0)),
                   s_out_spec,
                   pl.BlockSpec((3, nbk, w3), lambda i: (0, i, 0))],
        out_shape=[jax.ShapeDtypeStruct((nt, nb, wz), F32),
                   jax.ShapeDtypeStruct(s_all.shape, F32),
                   jax.ShapeDtypeStruct((3, nb, w3), F32)],
        scratch_shapes=[pltpu.VMEM((nbk, 2 * nt, wz), F32) for _ in range(4)],
        input_output_aliases=aliases,
        compiler_params=pltpu.CompilerParams(dimension_semantics=("arbitrary",),
                                             vmem_limit_bytes=VMEM_LIMIT),
        name="gdn_sample",
    )(*args)


def _ffn_kernel(x_ref, ysl_ref, yg_ref, wo_ref, nmp_ref, nfp_ref, nfq_ref, wgu_ref, wd_ref, o_ref,
                *, from_time_major):
    nb, tc, d = x_ref.shape
    rows = nb * tc
    x = x_ref[...].reshape(rows, d)
    ysl = ysl_ref[...]
    if from_time_major:
        ysl = jnp.swapaxes(ysl, 0, 1)
    ysl = ysl.reshape(rows, ysl.shape[-1])
    yg = yg_ref[...].reshape(rows, yg_ref.shape[-1])
    mix = jnp.concatenate([ysl, yg], axis=-1)
    x = x + _rms(_mm(mix, wo_ref[...]), nmp_ref[...])
    h = _rms(x, nfp_ref[...])
    gu = _mm(h, wgu_ref[...])
    hid = gu.shape[1] // 2
    gt = gu[:, :hid]
    f = _mm(gt * _sigmoid(gt) * gu[:, hid:], wd_ref[...])
    x = x + _rms(f, nfq_ref[...])
    o_ref[...] = x.reshape(nb, tc, d)


def _ffn(x3, ysl3, yg3, p, layer, *, tc, from_time_major):
    nb, length, d = x3.shape
    n1 = ysl3.shape[-1]
    n2 = yg3.shape[-1]
    consts = [p["w_out"], p["n_mix_post"], p["n_ffn_pre"], p["n_ffn_post"], p["w_gu"], p["w_down"]]
    if from_time_major:
        ysl_spec = pl.BlockSpec((tc, nb, n1), lambda i: (i, 0, 0))
    else:
        ysl_spec = pl.BlockSpec((nb, tc, n1), lambda i: (0, i, 0))
    return pl.pallas_call(
        functools.partial(_ffn_kernel, from_time_major=from_time_major),
        grid=(length // tc,),
        in_specs=[pl.BlockSpec((nb, tc, d), lambda i: (0, i, 0)), ysl_spec,
                  pl.BlockSpec((nb, tc, n2), lambda i: (0, i, 0))]
        + [_layer_spec(a, layer) for a in consts],
        out_specs=pl.BlockSpec((nb, tc, d), lambda i: (0, i, 0)),
        out_shape=jax.ShapeDtypeStruct((nb, length, d), F32),
        compiler_params=pltpu.CompilerParams(dimension_semantics=("arbitrary",),
                                             vmem_limit_bytes=VMEM_LIMIT),
        name="ffn",
    )(x3, ysl3, yg3, *consts)


def _block_diag(blocks):
    dep, n, r, c = blocks.shape
    eye = jnp.eye(n, dtype=blocks.dtype)
    return (eye[None, :, None, :, None] * blocks[:, :, :, None, :]).reshape(dep, n * r, n * c)


def _params(norm_mix_pre, norm_mix_post, norm_ffn_pre, norm_ffn_post, w_in, s5_lambda_re,
            s5_lambda_im, s5_log_dt, s5_b_re, s5_b_im, s5_c_re, s5_c_im, s5_d, s5_w_glu, s5_b_glu,
            lru_conv_w, lru_conv_b, lru_w_a, lru_b_a, lru_w_x, lru_b_x, lru_lambda, gdn_conv_w,
            gdn_a_log, gdn_dt_bias, gdn_norm_w, w_out, ffn_w_gate_up, ffn_w_down):
    dep, g, pst, hgrp = s5_b_re.shape
    s5w = g * hgrp
    lw = lru_lambda.shape[1]
    nh = gdn_a_log.shape[1]
    row = lambda v: v[:, None, :]
    lanes = lambda v: jnp.pad(v, ((0, 0), (0, LANES - v.shape[1])))[:, None, :]
    return {
        "n_sl": s5w + 2 * lw,
        "n_mix_pre": row(norm_mix_pre), "n_mix_post": row(norm_mix_post),
        "n_ffn_pre": row(norm_ffn_pre), "n_ffn_post": row(norm_ffn_post),
        "w_in": w_in.astype(BF16),
        "s5_lre": s5_lambda_re.reshape(dep, 1, g * pst), "s5_lim": s5_lambda_im.reshape(dep, 1, g * pst),
        "s5_ldt": jnp.repeat(s5_log_dt, pst, axis=1)[:, None, :],
        "s5_bre": _block_diag(jnp.swapaxes(s5_b_re, 2, 3)),
        "s5_bim": _block_diag(jnp.swapaxes(s5_b_im, 2, 3)),
        "s5_cre": _block_diag(jnp.swapaxes(s5_c_re, 2, 3)),
        "s5_cim": _block_diag(jnp.swapaxes(s5_c_im, 2, 3)),
        "s5_d": row(s5_d), "s5_wglu": s5_w_glu.astype(BF16), "s5_bglu": row(s5_b_glu),
        "lru_cw": lru_conv_w, "lru_cb": row(lru_conv_b),
        "lru_wa": _block_diag(lru_w_a).astype(BF16), "lru_ba": row(lru_b_a),
        "lru_wx": _block_diag(lru_w_x).astype(BF16), "lru_bx": row(lru_b_x),
        "lru_lambda": row(lru_lambda),
        "gdn_cw": gdn_conv_w, "gdn_alog": lanes(gdn_a_log),
        "gdn_dtb": lanes(gdn_dt_bias), "gdn_nw": row(gdn_norm_w), "gdn_nh": nh,
        "w_out": w_out.astype(BF16), "w_gu": ffn_w_gate_up.astype(BF16),
        "w_down": ffn_w_down.astype(BF16),
    }


def _prompt_layer(x, p, layer, *, tc, rg):
    nb, length, _ = x.shape
    sl_tm, gd = _in_proj(x, p, layer, tc=min(IN_PROJ_TOKENS_PER_SEQ_STEP, length), to_time_major=True)
    lw = p["lru_lambda"].shape[-1]
    ysl, xr, xi, hl, cv = _scan(sl_tm.reshape(length * nb, -1), p, layer, None, nb=nb, tc=tc,
                                first_pos_is_zero=True)
    yg, s_new, gcv = _gdn_prompt(gd, p, layer, rg=rg, ns=GDN_SEQS_PER_STEP if nb % GDN_SEQS_PER_STEP == 0 else 1)
    x = _ffn(x, ysl.reshape(length, nb, -1), yg, p, layer, tc=tc, from_time_major=True)
    lcv = jnp.swapaxes(cv.reshape(3, nb, lw), 0, 1)
    return x, (xr, xi, hl, lcv, s_new, gcv)


def _sample_layer(x_tm, st, gdn_s_all, gdn_s_acc, layer, p, *, nbk):
    s5_re, s5_im, lru_h, lru_conv, gdn_conv = st
    nb = s5_re.shape[0]
    rows = x_tm.shape[1]
    nt = rows // nb
    sl, gd = _in_proj(x_tm, p, layer, tc=rows, to_time_major=False)
    lw = p["lru_lambda"].shape[-1]
    cv0 = jnp.swapaxes(lru_conv, 0, 1).reshape(3 * nb, lw)
    states = [s5_re.reshape(nb, -1), s5_im.reshape(nb, -1), lru_h, cv0]
    ysl, xr, xi, hl, cv = _scan(sl.reshape(rows, -1), p, layer, states, nb=nb, tc=nt,
                                first_pos_is_zero=(PAST_LEN == 0))
    yg, s_acc, gcv = _gdn_sample(gd.reshape(nt, nb, -1), jnp.swapaxes(gdn_conv, 0, 1), gdn_s_all, gdn_s_acc,
                                 layer, p, nbk=nbk)
    x_tm = _ffn(x_tm, ysl.reshape(1, rows, -1), yg.reshape(1, rows, -1), p, layer, tc=rows,
                from_time_major=False)
    lcv = jnp.swapaxes(cv.reshape(3, nb, lw), 0, 1)
    return x_tm, (xr, xi, hl, lcv, jnp.swapaxes(gcv, 0, 1)), s_acc


def kernel(x_prompt, x_sample, state_s5_re, state_s5_im, state_lru_h, state_lru_conv, state_gdn_S, state_gdn_conv, norm_mix_pre, norm_mix_post, norm_ffn_pre, norm_ffn_post, w_in, s5_lambda_re, s5_lambda_im, s5_log_dt, s5_b_re, s5_b_im, s5_c_re, s5_c_im, s5_d, s5_w_glu, s5_b_glu, lru_conv_w, lru_conv_b, lru_w_a, lru_b_a, lru_w_x, lru_b_x, lru_lambda, gdn_conv_w, gdn_a_log, gdn_dt_bias, gdn_norm_w, w_out, ffn_w_gate_up, ffn_w_down):
    depth = w_in.shape[0]
    bp, lp, d = x_prompt.shape
    bs, ls, _ = x_sample.shape
    g, pst = s5_lambda_re.shape[1:]
    tc = min(TOKENS_PER_SEQ_STEP, lp)
    rg = min(GDN_ROWS_PER_STEP, lp)
    y_p = x_prompt
    y_s = jnp.swapaxes(x_sample, 0, 1).reshape(1, ls * bs, d)
    new_p = [[] for _ in range(6)]
    new_s = [[] for _ in range(5)]
    s_gdn_s = None
    p = _params(norm_mix_pre, norm_mix_post, norm_ffn_pre, norm_ffn_post, w_in, s5_lambda_re,
                s5_lambda_im, s5_log_dt, s5_b_re, s5_b_im, s5_c_re, s5_c_im, s5_d, s5_w_glu,
                s5_b_glu, lru_conv_w, lru_conv_b, lru_w_a, lru_b_a, lru_w_x, lru_b_x, lru_lambda,
                gdn_conv_w, gdn_a_log, gdn_dt_bias, gdn_norm_w, w_out, ffn_w_gate_up, ffn_w_down)
    for l in range(depth):
        y_p, sp = _prompt_layer(y_p, p, l, tc=tc, rg=rg)
        st_s = (state_s5_re[l], state_s5_im[l], state_lru_h[l], state_lru_conv[l], state_gdn_conv[l])
        y_s, ss, s_gdn_s = _sample_layer(y_s, st_s, state_gdn_S, s_gdn_s, l, p,
                                         nbk=min(DECODE_SEQS_PER_STEP, bs))
        for j in range(6):
            new_p[j].append(sp[j])
        for j in range(5):
            new_s[j].append(ss[j])
    y_s = jnp.swapaxes(y_s.reshape(ls, bs, d), 0, 1)
    outs_p = [jnp.stack(t, axis=0) for t in new_p]
    outs_s = [jnp.stack(t, axis=0) for t in new_s]
    outs_s.insert(4, s_gdn_s)
    outs_p[0] = outs_p[0].reshape(depth, bp, g, pst)
    outs_p[1] = outs_p[1].reshape(depth, bp, g, pst)
    outs_s[0] = outs_s[0].reshape(depth, bs, g, pst)
    outs_s[1] = outs_s[1].reshape(depth, bs, g, pst)
    return (y_p, y_s, *outs_p, *outs_s)
```

```python
import functools

import jax
import jax.numpy as jnp
from jax import lax
from jax.experimental import pallas as pl
from jax.experimental.pallas import tpu as pltpu

F32 = jnp.float32
BF16 = jnp.bfloat16
HIGHEST = lax.Precision.HIGHEST

NORM_EPS = 1e-6
LRU_C = 8.0
GDN_CHUNK = 64
PAST_LEN = 16384
LANES = 128
SUBLANES = 8
VMEM_LIMIT = 56 * 1024 * 1024

TOKENS_PER_SEQ_STEP = 64
IN_PROJ_TOKENS_PER_SEQ_STEP = 128
GDN_ROWS_PER_STEP = 4 * GDN_CHUNK
GDN_SEQS_PER_STEP = 4
DECODE_SEQS_PER_STEP = 16
DECODE_SEQ_UNROLL = 4


def _rms(x, w):
    return x * lax.rsqrt(jnp.mean(x * x, axis=-1, keepdims=True) + NORM_EPS) * w


def _sigmoid(x):
    return jax.nn.sigmoid(x)


def _softplus(x):
    return jnp.maximum(x, 0.0) + jnp.log1p(jnp.exp(-jnp.abs(x)))


def _mm(a, b):
    return jnp.dot(a.astype(BF16), b.astype(BF16), preferred_element_type=F32)


def _mm_nt(a, b):
    return lax.dot_general(a.astype(BF16), b.astype(BF16), (((1,), (1,)), ((), ())),
                           preferred_element_type=F32)


def _mm_tn(a, b):
    return lax.dot_general(a.astype(BF16), b.astype(BF16), (((0,), (0,)), ((), ())),
                           preferred_element_type=F32)


def _const_spec(shape):
    n = len(shape)
    return pl.BlockSpec(shape, lambda *_: (0,) * n, pipeline_mode=pl.Buffered(1))


def _layer_spec(arr, layer):
    n = arr.ndim - 1
    return pl.BlockSpec((None,) + arr.shape[1:], lambda *_: (layer,) + (0,) * n,
                        pipeline_mode=pl.Buffered(1))


def _l2n_heads(x, nh, hd, mul):
    def one(h):
        xh = x[:, h * hd:(h + 1) * hd]
        return xh * (lax.rsqrt(jnp.sum(xh * xh, axis=-1, keepdims=True) + NORM_EPS) * mul)
    return jnp.concatenate([one(h) for h in range(nh)], axis=-1)


def _in_proj_kernel(x_ref, nw_ref, w_ref, sl_ref, gd_ref, *, n_sl, w3, n_ab, to_time_major):
    nb, tc, d = x_ref.shape
    rows = nb * tc
    x = x_ref[...].reshape(rows, d)
    h = _rms(x, nw_ref[...])
    p = _mm(h, w_ref[...])
    sl = p[:, :n_sl].reshape(nb, tc, n_sl)
    if to_time_major:
        sl = jnp.swapaxes(sl, 0, 1)
    sl_ref[...] = sl
    c_ab = n_sl + w3
    c_z = c_ab + n_ab
    wz = p.shape[1] - c_z
    gd_ref[:, :, 0:w3] = p[:, n_sl:c_ab].reshape(nb, tc, w3)
    gd_ref[:, :, w3:w3 + wz] = p[:, c_z:].reshape(nb, tc, wz)
    ab = jnp.concatenate([p[:, c_ab:c_z], jnp.zeros((rows, LANES - n_ab), F32)], axis=1)
    gd_ref[:, :, w3 + wz:] = ab.reshape(nb, tc, LANES)


def _in_proj(x3, p, layer, *, tc, to_time_major):
    nb, length, d = x3.shape
    nw, w, n_sl = p["n_mix_pre"], p["w_in"], p["n_sl"]
    n_tot = w.shape[-1]
    nh = p["gdn_nh"]
    w3 = 3 * nh * p["gdn_nw"].shape[-1]
    n_gd = n_tot - n_sl - 2 * nh + LANES
    if to_time_major:
        sl_shape, sl_spec = (length, nb, n_sl), pl.BlockSpec((tc, nb, n_sl), lambda i: (i, 0, 0))
    else:
        sl_shape, sl_spec = (nb, length, n_sl), pl.BlockSpec((nb, tc, n_sl), lambda i: (0, i, 0))
    return pl.pallas_call(
        functools.partial(_in_proj_kernel, n_sl=n_sl, w3=w3, n_ab=2 * nh, to_time_major=to_time_major),
        grid=(length // tc,),
        in_specs=[pl.BlockSpec((nb, tc, d), lambda i: (0, i, 0)),
                  _layer_spec(nw, layer), _layer_spec(w, layer)],
        out_specs=[sl_spec, pl.BlockSpec((nb, tc, n_gd), lambda i: (0, i, 0))],
        out_shape=[jax.ShapeDtypeStruct(sl_shape, F32), jax.ShapeDtypeStruct((nb, length, n_gd), F32)],
        compiler_params=pltpu.CompilerParams(dimension_semantics=("arbitrary",),
                                             vmem_limit_bytes=VMEM_LIMIT),
        name="in_proj",
    )(x3, nw, w)


def _scan_kernel(*refs, nb, tc, s5w, lw, first_pos_is_zero, zero_state):
    (sl_ref, lre_ref, lim_ref, ldt_ref, bre_ref, bim_ref, cre_ref, cim_ref, d_ref,
     wglu_ref, bglu_ref, cw_ref, cb_ref, wa_ref, ba_ref, wx_ref, bx_ref, lam_ref) = refs[:18]
    n_in = 18 if zero_state else 22
    (y_ref, xr_out, xi_out, h_out, cv_out,
     ar_s, ai_s, bbr_s, bbi_s, crb_s, cib_s, xr_c, xi_c, h_c, xr_s, xi_s, xpad_s, a_s, b_s) = refs[n_in:]
    i = pl.program_id(0)
    last = pl.num_programs(0) - 1
    rows = nb * tc

    @pl.when(i == 0)
    def _init():
        lr = jnp.minimum(lre_ref[...], -1e-4)
        li = lim_ref[...]
        dt = jnp.exp(ldt_ref[...])
        mag = jnp.exp(lr * dt)
        ar = mag * jnp.cos(li * dt)
        ai = mag * jnp.sin(li * dt)
        den = lr * lr + li * li
        fr = ((ar - 1.0) * lr + ai * li) / den
        fi = (ai * lr - (ar - 1.0) * li) / den
        ar_s[...] = ar
        ai_s[...] = ai
        bre = bre_ref[...]
        bim = bim_ref[...]
        bbr_s[...] = (fr * bre - fi * bim).astype(BF16)
        bbi_s[...] = (fr * bim + fi * bre).astype(BF16)
        crb_s[...] = cre_ref[...].astype(BF16)
        cib_s[...] = cim_ref[...].astype(BF16)
        if zero_state:
            xr_c[...] = jnp.zeros_like(xr_c)
            xi_c[...] = jnp.zeros_like(xi_c)
            h_c[...] = jnp.zeros_like(h_c)
            xpad_s[0:3 * nb, :] = jnp.zeros((3 * nb, lw), F32)
        else:
            xr0_ref, xi0_ref, h0_ref, cv0_ref = refs[18:22]
            xr_c[...] = xr0_ref[...]
            xi_c[...] = xi0_ref[...]
            h_c[...] = h0_ref[...]
            xpad_s[0:3 * nb, :] = cv0_ref[...]

    u = sl_ref[:, 0:s5w]
    xb = sl_ref[:, s5w:s5w + lw]
    gate = sl_ref[:, s5w + lw:s5w + 2 * lw]

    ub = u.astype(BF16)
    xr_s[...] = jnp.dot(ub, bbr_s[...], preferred_element_type=F32)
    xi_s[...] = jnp.dot(ub, bbi_s[...], preferred_element_type=F32)

    xpad_s[3 * nb:3 * nb + rows, :] = xb
    cw = cw_ref[...]
    xc = (xpad_s[0:rows, :] * cw[0:1] + xpad_s[nb:nb + rows, :] * cw[1:2]
          + xpad_s[2 * nb:2 * nb + rows, :] * cw[2:3] + xb * cw[3:4]) + cb_ref[...]
    new_cv = xpad_s[rows:rows + 3 * nb, :]
    xpad_s[0:3 * nb, :] = new_cv
    r = _sigmoid(_mm(xc, wa_ref[...]) + ba_ref[...])
    gi = _sigmoid(_mm(xc, wx_ref[...]) + bx_ref[...])
    log_a = -LRU_C * r * _softplus(-lam_ref[...])
    a = jnp.exp(log_a)
    m2 = -jnp.tanh(log_a) * (a * a + 1.0)
    mult = jnp.where(m2 > 0.0, m2 * lax.rsqrt(m2), 0.0)
    if first_pos_is_zero:
        rid = lax.broadcasted_iota(jnp.int32, (rows, 1), 0)
        mult = jnp.where(jnp.logical_and(rid < nb, i == 0), 1.0, mult)
    a_s[...] = a
    b_s[...] = mult * gi * xc

    ar = jnp.broadcast_to(ar_s[...], (nb, ar_s.shape[1]))
    ai = jnp.broadcast_to(ai_s[...], (nb, ai_s.shape[1]))

    def step(t, carry):
        xr, xi, h = carry
        sl_t = pl.ds(pl.multiple_of(t * nb, nb), nb)
        nxr = ar * xr - ai * xi + xr_s[sl_t, :]
        nxi = ar * xi + ai * xr + xi_s[sl_t, :]
        nh = a_s[sl_t, :] * h + b_s[sl_t, :]
        xr_s[sl_t, :] = nxr
        xi_s[sl_t, :] = nxi
        b_s[sl_t, :] = nh
        return nxr, nxi, nh

    xr, xi, h = lax.fori_loop(0, tc, step, (xr_c[...], xi_c[...], h_c[...]),
                              unroll=(True if tc <= 8 else 8))
    xr_c[...] = xr
    xi_c[...] = xi
    h_c[...] = h

    y = (jnp.dot(xr_s[...].astype(BF16), crb_s[...], preferred_element_type=F32)
         - jnp.dot(xi_s[...].astype(BF16), cib_s[...], preferred_element_type=F32))
    y = jax.nn.gelu(y + d_ref[...] * u)
    y = y * _sigmoid(_mm(y, wglu_ref[...]) + bglu_ref[...])
    y_ref[:, 0:s5w] = y
    y_ref[:, s5w:s5w + lw] = b_s[...] * jax.nn.gelu(gate)

    @pl.when(i == last)
    def _fin():
        xr_out[...] = xr
        xi_out[...] = xi
        h_out[...] = h
        cv_out[...] = new_cv


def _scan(sl2, p, layer, states, *, nb, tc, first_pos_is_zero):
    rows_total, n_sl = sl2.shape
    s5w = p["s5_d"].shape[-1]
    lw = p["lru_lambda"].shape[-1]
    n_state = p["s5_lre"].shape[-1]
    rows = nb * tc
    params = [p["s5_lre"], p["s5_lim"], p["s5_ldt"], p["s5_bre"], p["s5_bim"], p["s5_cre"], p["s5_cim"],
              p["s5_d"], p["s5_wglu"], p["s5_bglu"], p["lru_cw"], p["lru_cb"], p["lru_wa"], p["lru_ba"],
              p["lru_wx"], p["lru_bx"], p["lru_lambda"]]
    zero_state = states is None
    states = [] if zero_state else list(states)
    return pl.pallas_call(
        functools.partial(_scan_kernel, nb=nb, tc=tc, s5w=s5w, lw=lw, first_pos_is_zero=first_pos_is_zero,
                          zero_state=zero_state),
        grid=(rows_total // rows,),
        in_specs=[pl.BlockSpec((rows, n_sl), lambda i: (i, 0))]
        + [_layer_spec(a, layer) for a in params] + [_const_spec(a.shape) for a in states],
        out_specs=[pl.BlockSpec((rows, s5w + lw), lambda i: (i, 0)),
                   pl.BlockSpec((nb, n_state), lambda i: (0, 0)),
                   pl.BlockSpec((nb, n_state), lambda i: (0, 0)),
                   pl.BlockSpec((nb, lw), lambda i: (0, 0)),
                   pl.BlockSpec((3 * nb, lw), lambda i: (0, 0))],
        out_shape=[jax.ShapeDtypeStruct((rows_total, s5w + lw), F32),
                   jax.ShapeDtypeStruct((nb, n_state), F32),
                   jax.ShapeDtypeStruct((nb, n_state), F32),
                   jax.ShapeDtypeStruct((nb, lw), F32),
                   jax.ShapeDtypeStruct((3 * nb, lw), F32)],
        scratch_shapes=[pltpu.VMEM((1, n_state), F32), pltpu.VMEM((1, n_state), F32),
                        pltpu.VMEM((s5w, n_state), BF16), pltpu.VMEM((s5w, n_state), BF16),
                        pltpu.VMEM((n_state, s5w), BF16), pltpu.VMEM((n_state, s5w), BF16),
                        pltpu.VMEM((nb, n_state), F32), pltpu.VMEM((nb, n_state), F32),
                        pltpu.VMEM((nb, lw), F32),
                        pltpu.VMEM((rows, n_state), F32), pltpu.VMEM((rows, n_state), F32),
                        pltpu.VMEM((rows + 3 * nb, lw), F32),
                        pltpu.VMEM((rows, lw), F32), pltpu.VMEM((rows, lw), F32)],
        compiler_params=pltpu.CompilerParams(dimension_semantics=("arbitrary",),
                                             vmem_limit_bytes=VMEM_LIMIT),
        name="scan",
    )(sl2, *params, *states)


def _gdn_prompt_kernel(gd_ref, cw_ref, alog_ref, dtb_ref, nw_ref, y_ref, s_out, cv_out,
                       s_s, xp_s, *, ns, rg, nh, hd):
    i = pl.program_id(1)
    last = pl.num_programs(1) - 1
    w3 = 3 * nh * hd
    wz = nh * hd
    c = GDN_CHUNK
    nc = rg // c
    wc = nh * c
    rt = ns * rg

    @pl.when(i == 0)
    def _init():
        s_s[...] = jnp.zeros_like(s_s)
        xp_s[...] = jnp.zeros_like(xp_s)

    cw = cw_ref[...]
    xcs = []
    for s in range(ns):
        qkv = gd_ref[s, :, 0:w3]
        xe = jnp.concatenate([xp_s[s], qkv], axis=0)
        acc = qkv * cw[3:4]
        for j in range(1, 4):
            acc = acc + pltpu.roll(xe, j, 0)[SUBLANES:, :] * cw[3 - j:4 - j]
        xp_s[s] = qkv[rg - SUBLANES:, :]
        xcs.append(acc)
    xc = jnp.concatenate(xcs, axis=0)
    xc = xc * _sigmoid(xc)

    ab = jnp.concatenate([gd_ref[s, :, w3 + wz:w3 + wz + LANES] for s in range(ns)], axis=0)
    g_all = -jnp.exp(alog_ref[...]) * _softplus(ab + dtb_ref[...])
    beta_all = _sigmoid(ab)

    def per_head(fn):
        return jnp.concatenate([fn(h) for h in range(nh)], axis=-1)

    q_all = _l2n_heads(xc[:, 0:wz], nh, hd, hd ** -0.5)
    k_all = _l2n_heads(xc[:, wz:2 * wz], nh, hd, 1.0)
    v_all = xc[:, 2 * wz:3 * wz]
    beta_w = per_head(lambda h: jnp.broadcast_to(beta_all[:, nh + h:nh + h + 1], (rt, hd)))
    kb_all = k_all * beta_w
    vb_all = v_all * beta_w

    ri = lax.broadcasted_iota(jnp.int32, (c, wc), 0)
    cj = lax.broadcasted_iota(jnp.int32, (c, wc), 1) % c
    incl = ri >= cj
    strict = ri > cj
    eye = (ri == cj).astype(F32)
    r1 = lax.broadcasted_iota(jnp.int32, (c, c), 0)
    c1 = lax.broadcasted_iota(jnp.int32, (c, c), 1)
    ltri = (r1 >= c1).astype(F32)
    bd_cc = (lax.broadcasted_iota(jnp.int32, (wc, wc), 0) // c
             == lax.broadcasted_iota(jnp.int32, (wc, wc), 1) // c)
    bd_cd = (lax.broadcasted_iota(jnp.int32, (wc, wz), 0) // c
             == lax.broadcasted_iota(jnp.int32, (wc, wz), 1) // hd)

    def bdiag(x, mask):
        return jnp.where(mask, jnp.concatenate([x] * nh, axis=0), 0.0).astype(BF16)

    nc_all = ns * nc
    rows = [slice(cc * c, (cc + 1) * c) for cc in range(nc_all)]
    gcum = [jnp.dot(ltri, g_all[r], precision=HIGHEST, preferred_element_type=F32) for r in rows]
    cgc = [per_head(lambda h, x=x: jnp.broadcast_to(x[:, h:h + 1], (c, hd))) for x in gcum]
    gcol = [per_head(lambda h, x=x: jnp.broadcast_to(x[:, h:h + 1], (c, c))) for x in gcum]
    grow = [jnp.concatenate([xt[h:h + 1, :] for h in range(nh)], axis=1)
            for xt in [x.T for x in gcum]]
    every = range(nc_all)
    decay = [jnp.exp(jnp.where(incl, gcol[n] - grow[n], -jnp.inf)) for n in every]
    eg = [jnp.exp(x) for x in cgc]
    g_last = [x[c - 1:c, :] for x in cgc]
    kk = [_mm_nt(jnp.concatenate([kb_all[r], q_all[r]], axis=0), bdiag(k_all[r], bd_cd)) for r in rows]
    attn = [kk[n][c:2 * c] * decay[n] for n in every]
    rj = [-jnp.where(strict, kk[n][0:c] * decay[n], 0.0) for n in every]
    sj = [eye + rj[n] for n in every]
    rj = [_mm(rj[n], bdiag(rj[n], bd_cc)) for n in every]
    for _ in range(4):
        rs = [_mm(jnp.concatenate([rj[n], sj[n]], axis=0), bdiag(rj[n], bd_cc)) for n in every]
        rj = [x[0:c] for x in rs]
        sj = [sj[n] + rs[n][c:2 * c] for n in every]
    sj = [sj[n] + _mm(sj[n], bdiag(rj[n], bd_cc)) for n in every]
    w = [_mm(sj[n], jnp.concatenate([bdiag(vb_all[rows[n]], bd_cd),
                                     bdiag(kb_all[rows[n]] * eg[n], bd_cd)], axis=1)) for n in every]
    q_dec = [q_all[rows[n]] * eg[n] for n in every]
    k_dec = [k_all[rows[n]] * jnp.exp(g_last[n] - cgc[n]) for n in every]
    s_dec = [jnp.exp(x) for x in g_last]

    s_cur = [[s_s[s, h] for h in range(nh)] for s in range(ns)]
    hs = [slice(h * hd, (h + 1) * hd) for h in range(nh)]
    for m in range(nc):
        for s in range(ns):
            n = s * nc + m
            st = s_cur[s]
            ws = [_mm(jnp.concatenate([w[n][:, wz + h * hd:wz + (h + 1) * hd], q_dec[n][:, hs[h]]], axis=0),
                      st[h]) for h in range(nh)]
            v_new = jnp.concatenate([w[n][:, hs[h]] - ws[h][0:c] for h in range(nh)], axis=-1)
            o = jnp.concatenate([x[c:2 * c] for x in ws], axis=-1) + _mm(attn[n], bdiag(v_new, bd_cd))
            s_cur[s] = [st[h] * s_dec[n][:, hs[h]] + _mm_tn(k_dec[n][:, hs[h]], v_new[:, hs[h]])
                        for h in range(nh)]
            for h in range(nh):
                z = gd_ref[s, m * c:(m + 1) * c, w3 + h * hd:w3 + (h + 1) * hd]
                y_ref[s, m * c:(m + 1) * c, hs[h]] = _rms(o[:, hs[h]], nw_ref[...]) * (z * _sigmoid(z))
    for s in range(ns):
        for h in range(nh):
            s_s[s, h] = s_cur[s][h]

    @pl.when(i == last)
    def _fin():
        for s in range(ns):
            for h in range(nh):
                s_out[s, h] = s_cur[s][h]
            cv_out[s] = xp_s[s, SUBLANES - 3:SUBLANES, :]


def _gdn_prompt(gd3, p, layer, *, rg, ns):
    nb, length, n_gd = gd3.shape
    nh = p["gdn_nh"]
    hd = p["gdn_nw"].shape[-1]
    w3 = 3 * nh * hd
    consts = [p["gdn_cw"], p["gdn_alog"], p["gdn_dtb"], p["gdn_nw"]]
    return pl.pallas_call(
        functools.partial(_gdn_prompt_kernel, ns=ns, rg=rg, nh=nh, hd=hd),
        grid=(nb // ns, length // rg),
        in_specs=[pl.BlockSpec((ns, rg, n_gd), lambda b, i: (b, i, 0))]
        + [_layer_spec(a, layer) for a in consts],
        out_specs=[pl.BlockSpec((ns, rg, nh * hd), lambda b, i: (b, i, 0)),
                   pl.BlockSpec((ns, nh, hd, hd), lambda b, i: (b, 0, 0, 0)),
                   pl.BlockSpec((ns, 3, w3), lambda b, i: (b, 0, 0))],
        out_shape=[jax.ShapeDtypeStruct((nb, length, nh * hd), F32),
                   jax.ShapeDtypeStruct((nb, nh, hd, hd), F32),
                   jax.ShapeDtypeStruct((nb, 3, w3), F32)],
        scratch_shapes=[pltpu.VMEM((ns, nh, hd, hd), F32), pltpu.VMEM((ns, SUBLANES, w3), F32)],
        compiler_params=pltpu.CompilerParams(dimension_semantics=("arbitrary", "arbitrary"),
                                             vmem_limit_bytes=VMEM_LIMIT),
        name="gdn_prompt",
    )(gd3, *consts)


def _gdn_sample_kernel(*refs, nt, nbk, nh, hd, layer, has_acc):
    gd_ref, cv0_ref, s0_ref, cw_ref, alog_ref, dtb_ref, nw_ref = refs[:7]
    y_ref, s_out, cv_out, wk_s, kd_s, wv_s, ws_s = refs[7 + int(has_acc):]
    if not has_acc:
        for other in range(s_out.shape[0]):
            if other != layer:
                s_out[other] = jnp.zeros(s_out.shape[1:], F32)
        s_out = s_out.at[layer]
    w3 = 3 * nh * hd
    wz = nh * hd
    cw = cw_ref[...]
    scale = hd ** -0.5
    xs = [cv0_ref[j] for j in range(3)] + [gd_ref[t, :, 0:w3] for t in range(nt)]
    for j in range(3):
        cv_out[j] = xs[nt + j]
    q, k, v, gc, beta = [], [], [], [], []
    for t in range(nt):
        xc = xs[t] * cw[0:1] + xs[t + 1] * cw[1:2] + xs[t + 2] * cw[2:3] + xs[t + 3] * cw[3:4]
        xc = xc * _sigmoid(xc)
        ab = gd_ref[t, :, w3 + wz:w3 + wz + LANES]
        g_t = -jnp.exp(alog_ref[...]) * _softplus(ab + dtb_ref[...])
        gc.append(g_t if t == 0 else gc[-1] + g_t)
        beta.append(_sigmoid(ab))
        qn = _l2n_heads(xc[:, 0:wz], nh, hd, scale)
        kn = _l2n_heads(xc[:, wz:2 * wz], nh, hd, 1.0)
        q.append([qn[:, h * hd:(h + 1) * hd] for h in range(nh)])
        k.append([kn[:, h * hd:(h + 1) * hd] for h in range(nh)])
        v.append([xc[:, 2 * wz + h * hd:2 * wz + (h + 1) * hd] for h in range(nh)])

    def dot(a, b):
        return jnp.sum(a * b, axis=-1, keepdims=True)

    wk_rows = [[None] * nh for _ in range(nt)]
    qd_rows = [[None] * nh for _ in range(nt)]
    kd_rows = [[None] * nh for _ in range(nt)]
    wv_rows = [[None] * nh for _ in range(nt)]
    sd_rows = [None] * nh
    attn = [None] * nh
    for h in range(nh):
        gch = [gc[t][:, h:h + 1] for t in range(nt)]
        bh = [beta[t][:, nh + h:nh + h + 1] for t in range(nt)]
        e = [jnp.exp(x) for x in gch]
        kb = [k[t][h] * bh[t] for t in range(nt)]
        vb = [v[t][h] * bh[t] for t in range(nt)]
        kbe = [kb[t] * e[t] for t in range(nt)]
        dec = [[jnp.exp(gch[i] - gch[j]) for j in range(i)] for i in range(nt)]
        m = [[dot(kb[i], k[j][h]) * dec[i][j] for j in range(i)] for i in range(nt)]
        attn[h] = [[dot(q[i][h], k[j][h]) * (dec[i][j] if j < i else 1.0) for j in range(i + 1)]
                   for i in range(nt)]
        tinv = [[None] * nt for _ in range(nt)]
        for j in range(nt):
            for i in range(j + 1, nt):
                acc = m[i][j]
                for l in range(j + 1, i):
                    acc = acc + m[i][l] * tinv[l][j]
                tinv[i][j] = -acc
        for i in range(nt):
            wv_i, wk_i = vb[i], kbe[i]
            for j in range(i):
                wv_i = wv_i + tinv[i][j] * vb[j]
                wk_i = wk_i + tinv[i][j] * kbe[j]
            wv_rows[i][h] = wv_i
            wk_rows[i][h] = wk_i
            qd_rows[i][h] = q[i][h] * e[i]
            kd_rows[i][h] = k[i][h] * jnp.exp(gch[nt - 1] - gch[i])
        sd_rows[h] = jnp.broadcast_to(e[nt - 1], (nbk, hd))

    cat = lambda parts: jnp.concatenate(parts, axis=-1)
    to_seq = lambda slabs: jnp.swapaxes(jnp.stack(slabs, axis=0), 0, 1)
    zero = jnp.zeros((nbk, wz), F32)
    wk_s[...] = to_seq([cat(wk_rows[i]) for i in range(nt)] + [cat(qd_rows[i]) for i in range(nt)])
    kd_s[...] = to_seq([cat(kd_rows[i]) for i in range(nt)] + [zero] * nt)
    wv_s[...] = to_seq([cat(wv_rows[i]) for i in range(nt)] + [cat(sd_rows)] * nt)

    group = min(DECODE_SEQ_UNROLL, nbk)

    def per_group(gi, _):
        pairs = [(gi * group + u, h) for u in range(group) for h in range(nh)]
        tiles = {}
        for u in range(group):
            b = gi * group + u
            tiles[u] = (wk_s[b], kd_s[b], wv_s[b])
        cols = [slice(h * hd, (h + 1) * hd) for h in range(nh)]
        s_old = [s0_ref[b, h] for b, h in pairs]
        r = [_mm(tiles[n // nh][0][:, cols[h]], s_old[n]) for n, (b, h) in enumerate(pairs)]
        for n, (b, h) in enumerate(pairs):
            tkd, twv = tiles[n // nh][1][:, cols[h]], tiles[n // nh][2][:, cols[h]]
            s_out[b, h] = s_old[n] * twv[nt:nt + 1, :] + _mm_tn(tkd, twv - r[n])
            ws_s[b, :, cols[h]] = r[n]
        return 0

    lax.fori_loop(0, nbk // group, per_group, 0)

    ws_tm = jnp.swapaxes(ws_s[...], 0, 1)
    for h in range(nh):
        cols = slice(h * hd, (h + 1) * hd)
        v_new = [wv_rows[j][h] - ws_tm[j][:, cols] for j in range(nt)]
        for i in range(nt):
            o = ws_tm[nt + i][:, cols]
            for j in range(i + 1):
                o = o + attn[h][i][j] * v_new[j]
            z = gd_ref[i, :, w3 + h * hd:w3 + (h + 1) * hd]
            y_ref[i, :, cols] = _rms(o, nw_ref[...]) * (z * _sigmoid(z))


def _gdn_sample(gd3, cv0, s_all, s_acc, layer, p, *, nbk):
    nt, nb, n_gd = gd3.shape
    nh = p["gdn_nh"]
    hd = p["gdn_nw"].shape[-1]
    w3 = 3 * nh * hd
    wz = nh * hd
    s_spec = pl.BlockSpec((None, nbk, nh, hd, hd), lambda i: (layer, i, 0, 0, 0))
    consts = [p["gdn_cw"], p["gdn_alog"], p["gdn_dtb"], p["gdn_nw"]]
    args = [gd3, cv0, s_all] + consts
    in_specs = [pl.BlockSpec((nt, nbk, n_gd), lambda i: (0, i, 0)),
                pl.BlockSpec((3, nbk, w3), lambda i: (0, i, 0)),
                s_spec] + [_layer_spec(a, layer) for a in consts]
    aliases = {}
    s_out_spec = s_spec
    if s_acc is not None:
        aliases = {len(args): 1}
        args.append(s_acc)
        in_specs.append(pl.BlockSpec(memory_space=pl.ANY))
    else:
        s_out_spec = pl.BlockSpec((s_all.shape[0], nbk, nh, hd, hd), lambda i: (0, i, 0, 0, 0))
    return pl.pallas_call(
        functools.partial(_gdn_sample_kernel, nt=nt, nbk=nbk, nh=nh, hd=hd, layer=layer,
                          has_acc=s_acc is not None),
        grid=(nb // nbk,),
        in_specs=in_specs,
        out_specs=[pl.BlockSpec((nt, nbk, wz), lambda i: (0, i, 0)),
                   s_out_spec,
                   pl.BlockSpec((3, nbk, w3), lambda i: (0, i, 0))],
        out_shape=[jax.ShapeDtypeStruct((nt, nb, wz), F32),
                   jax.ShapeDtypeStruct(s_all.shape, F32),
                   jax.ShapeDtypeStruct((3, nb, w3), F32)],
        scratch_shapes=[pltpu.VMEM((nbk, 2 * nt, wz), F32) for _ in range(4)],
        input_output_aliases=aliases,
        compiler_params=pltpu.CompilerParams(dimension_semantics=("arbitrary",),
                                             vmem_limit_bytes=VMEM_LIMIT),
        name="gdn_sample",
    )(*args)


def _ffn_kernel(x_ref, ysl_ref, yg_ref, wo_ref, nmp_ref, nfp_ref, nfq_ref, wgu_ref, wd_ref, o_ref,
                *, from_time_major):
    nb, tc, d = x_ref.shape
    rows = nb * tc
    x = x_ref[...].reshape(rows, d)
    ysl = ysl_ref[...]
    if from_time_major:
        ysl = jnp.swapaxes(ysl, 0, 1)
    ysl = ysl.reshape(rows, ysl.shape[-1])
    yg = yg_ref[...].reshape(rows, yg_ref.shape[-1])
    mix = jnp.concatenate([ysl, yg], axis=-1)
    x = x + _rms(_mm(mix, wo_ref[...]), nmp_ref[...])
    h = _rms(x, nfp_ref[...])
    gu = _mm(h, wgu_ref[...])
    hid = gu.shape[1] // 2
    gt = gu[:, :hid]
    f = _mm(gt * _sigmoid(gt) * gu[:, hid:], wd_ref[...])
    x = x + _rms(f, nfq_ref[...])
    o_ref[...] = x.reshape(nb, tc, d)


def _ffn(x3, ysl3, yg3, p, layer, *, tc, from_time_major):
    nb, length, d = x3.shape
    n1 = ysl3.shape[-1]
    n2 = yg3.shape[-1]
    consts = [p["w_out"], p["n_mix_post"], p["n_ffn_pre"], p["n_ffn_post"], p["w_gu"], p["w_down"]]
    if from_time_major:
        ysl_spec = pl.BlockSpec((tc, nb, n1), lambda i: (i, 0, 0))
    else:
        ysl_spec = pl.BlockSpec((nb, tc, n1), lambda i: (0, i, 0))
    return pl.pallas_call(
        functools.partial(_ffn_kernel, from_time_major=from_time_major),
        grid=(length // tc,),
        in_specs=[pl.BlockSpec((nb, tc, d), lambda i: (0, i, 0)), ysl_spec,
                  pl.BlockSpec((nb, tc, n2), lambda i: (0, i, 0))]
        + [_layer_spec(a, layer) for a in consts],
        out_specs=pl.BlockSpec((nb, tc, d), lambda i: (0, i, 0)),
        out_shape=jax.ShapeDtypeStruct((nb, length, d), F32),
        compiler_params=pltpu.CompilerParams(dimension_semantics=("arbitrary",),
                                             vmem_limit_bytes=VMEM_LIMIT),
        name="ffn",
    )(x3, ysl3, yg3, *consts)


def _block_diag(blocks):
    dep, n, r, c = blocks.shape
    eye = jnp.eye(n, dtype=blocks.dtype)
    return (eye[None, :, None, :, None] * blocks[:, :, :, None, :]).reshape(dep, n * r, n * c)


def _params(norm_mix_pre, norm_mix_post, norm_ffn_pre, norm_ffn_post, w_in, s5_lambda_re,
            s5_lambda_im, s5_log_dt, s5_b_re, s5_b_im, s5_c_re, s5_c_im, s5_d, s5_w_glu, s5_b_glu,
            lru_conv_w, lru_conv_b, lru_w_a, lru_b_a, lru_w_x, lru_b_x, lru_lambda, gdn_conv_w,
            gdn_a_log, gdn_dt_bias, gdn_norm_w, w_out, ffn_w_gate_up, ffn_w_down):
    dep, g, pst, hgrp = s5_b_re.shape
    s5w = g * hgrp
    lw = lru_lambda.shape[1]
    nh = gdn_a_log.shape[1]
    row = lambda v: v[:, None, :]
    lanes = lambda v: jnp.pad(v, ((0, 0), (0, LANES - v.shape[1])))[:, None, :]
    return {
        "n_sl": s5w + 2 * lw,
        "n_mix_pre": row(norm_mix_pre), "n_mix_post": row(norm_mix_post),
        "n_ffn_pre": row(norm_ffn_pre), "n_ffn_post": row(norm_ffn_post),
        "w_in": w_in.astype(BF16),
        "s5_lre": s5_lambda_re.reshape(dep, 1, g * pst), "s5_lim": s5_lambda_im.reshape(dep, 1, g * pst),
        "s5_ldt": jnp.repeat(s5_log_dt, pst, axis=1)[:, None, :],
        "s5_bre": _block_diag(jnp.swapaxes(s5_b_re, 2, 3)),
        "s5_bim": _block_diag(jnp.swapaxes(s5_b_im, 2, 3)),
        "s5_cre": _block_diag(jnp.swapaxes(s5_c_re, 2, 3)),
        "s5_cim": _block_diag(jnp.swapaxes(s5_c_im, 2, 3)),
        "s5_d": row(s5_d), "s5_wglu": s5_w_glu.astype(BF16), "s5_bglu": row(s5_b_glu),
        "lru_cw": lru_conv_w, "lru_cb": row(lru_conv_b),
        "lru_wa": _block_diag(lru_w_a).astype(BF16), "lru_ba": row(lru_b_a),
        "lru_wx": _block_diag(lru_w_x).astype(BF16), "lru_bx": row(lru_b_x),
        "lru_lambda": row(lru_lambda),
        "gdn_cw": gdn_conv_w, "gdn_alog": lanes(gdn_a_log),
        "gdn_dtb": lanes(gdn_dt_bias), "gdn_nw": row(gdn_norm_w), "gdn_nh": nh,
        "w_out": w_out.astype(BF16), "w_gu": ffn_w_gate_up.astype(BF16),
        "w_down": ffn_w_down.astype(BF16),
    }


def _prompt_layer(x, p, layer, *, tc, rg):
    nb, length, _ = x.shape
    sl_tm, gd = _in_proj(x, p, layer, tc=min(IN_PROJ_TOKENS_PER_SEQ_STEP, length), to_time_major=True)
    lw = p["lru_lambda"].shape[-1]
    ysl, xr, xi, hl, cv = _scan(sl_tm.reshape(length * nb, -1), p, layer, None, nb=nb, tc=tc,
                                first_pos_is_zero=True)
    yg, s_new, gcv = _gdn_prompt(gd, p, layer, rg=rg, ns=GDN_SEQS_PER_STEP if nb % GDN_SEQS_PER_STEP == 0 else 1)
    x = _ffn(x, ysl.reshape(length, nb, -1), yg, p, layer, tc=tc, from_time_major=True)
    lcv = jnp.swapaxes(cv.reshape(3, nb, lw), 0, 1)
    return x, (xr, xi, hl, lcv, s_new, gcv)


def _sample_layer(x_tm, st, gdn_s_all, gdn_s_acc, layer, p, *, nbk):
    s5_re, s5_im, lru_h, lru_conv, gdn_conv = st
    nb = s5_re.shape[0]
    rows = x_tm.shape[1]
    nt = rows // nb
    sl, gd = _in_proj(x_tm, p, layer, tc=rows, to_time_major=False)
    lw = p["lru_lambda"].shape[-1]
    cv0 = jnp.swapaxes(lru_conv, 0, 1).reshape(3 * nb, lw)
    states = [s5_re.reshape(nb, -1), s5_im.reshape(nb, -1), lru_h, cv0]
    ysl, xr, xi, hl, cv = _scan(sl.reshape(rows, -1), p, layer, states, nb=nb, tc=nt,
                                first_pos_is_zero=(PAST_LEN == 0))
    yg, s_acc, gcv = _gdn_sample(gd.reshape(nt, nb, -1), jnp.swapaxes(gdn_conv, 0, 1), gdn_s_all, gdn_s_acc,
                                 layer, p, nbk=nbk)
    x_tm = _ffn(x_tm, ysl.reshape(1, rows, -1), yg.reshape(1, rows, -1), p, layer, tc=rows,
                from_time_major=False)
    lcv = jnp.swapaxes(cv.reshape(3, nb, lw), 0, 1)
    return x_tm, (xr, xi, hl, lcv, jnp.swapaxes(gcv, 0, 1)), s_acc


def kernel(x_prompt, x_sample, state_s5_re, state_s5_im, state_lru_h, state_lru_conv, state_gdn_S, state_gdn_conv, norm_mix_pre, norm_mix_post, norm_ffn_pre, norm_ffn_post, w_in, s5_lambda_re, s5_lambda_im, s5_log_dt, s5_b_re, s5_b_im, s5_c_re, s5_c_im, s5_d, s5_w_glu, s5_b_glu, lru_conv_w, lru_conv_b, lru_w_a, lru_b_a, lru_w_x, lru_b_x, lru_lambda, gdn_conv_w, gdn_a_log, gdn_dt_bias, gdn_norm_w, w_out, ffn_w_gate_up, ffn_w_down):
    depth = w_in.shape[0]
    bp, lp, d = x_prompt.shape
    bs, ls, _ = x_sample.shape
    g, pst = s5_lambda_re.shape[1:]
    tc = min(TOKENS_PER_SEQ_STEP, lp)
    rg = min(GDN_ROWS_PER_STEP, lp)
    y_p = x_prompt
    y_s = jnp.swapaxes(x_sample, 0, 1).reshape(1, ls * bs, d)
    new_p = [[] for _ in range(6)]
    new_s = [[] for _ in range(5)]
    s_gdn_s = None
    p = _params(norm_mix_pre, norm_mix_post, norm_ffn_pre, norm_ffn_post, w_in, s5_lambda_re,
                s5_lambda_im, s5_log_dt, s5_b_re, s5_b_im, s5_c_re, s5_c_im, s5_d, s5_w_glu,
                s5_b_glu, lru_conv_w, lru_conv_b, lru_w_a, lru_b_a, lru_w_x, lru_b_x, lru_lambda,
                gdn_conv_w, gdn_a_log, gdn_dt_bias, gdn_norm_w, w_out, ffn_w_gate_up, ffn_w_down)
    for l in range(depth):
        y_p, sp = _prompt_layer(y_p, p, l, tc=tc, rg=rg)
        st_s = (state_s5_re[l], state_s5_im[l], state_lru_h[l], state_lru_conv[l], state_gdn_conv[l])
        y_s, ss, s_gdn_s = _sample_layer(y_s, st_s, state_gdn_S, s_gdn_s, l, p,
                                         nbk=min(DECODE_SEQS_PER_STEP, bs))
        for j in range(6):
            new_p[j].append(sp[j])
        for j in range(5):
            new_s[j].append(ss[j])
    y_s = jnp.swapaxes(y_s.reshape(ls, bs, d), 0, 1)
    outs_p = [jnp.stack(t, axis=0) for t in new_p]
    outs_s = [jnp.stack(t, axis=0) for t in new_s]
    outs_s.insert(4, s_gdn_s)
    outs_p[0] = outs_p[0].reshape(depth, bp, g, pst)
    outs_p[1] = outs_p[1].reshape(depth, bp, g, pst)
    outs_s[0] = outs_s[0].reshape(depth, bs, g, pst)
    outs_s[1] = outs_s[1].reshape(depth, bs, g, pst)
    return (y_p, y_s, *outs_p, *outs_s)
```

```python
import functools

import jax
import jax.numpy as jnp
from jax import lax
from jax.experimental import pallas as pl
from jax.experimental.pallas import tpu as pltpu

F32 = jnp.float32
BF16 = jnp.bfloat16
HIGHEST = lax.Precision.HIGHEST

NORM_EPS = 1e-6
LRU_C = 8.0
GDN_CHUNK = 64
PAST_LEN = 16384
LANES = 128
SUBLANES = 8
VMEM_LIMIT = 56 * 1024 * 1024

TOKENS_PER_SEQ_STEP = 64
IN_PROJ_TOKENS_PER_SEQ_STEP = 128
GDN_ROWS_PER_STEP = 4 * GDN_CHUNK
GDN_SEQS_PER_STEP = 4
DECODE_SEQS_PER_STEP = 16
DECODE_SEQ_UNROLL = 4


def _rms(x, w):
    return x * lax.rsqrt(jnp.mean(x * x, axis=-1, keepdims=True) + NORM_EPS) * w


def _sigmoid(x):
    return jax.nn.sigmoid(x)


def _softplus(x):
    return jnp.maximum(x, 0.0) + jnp.log1p(jnp.exp(-jnp.abs(x)))


def _mm(a, b):
    return jnp.dot(a.astype(BF16), b.astype(BF16), preferred_element_type=F32)


def _mm_nt(a, b):
    return lax.dot_general(a.astype(BF16), b.astype(BF16), (((1,), (1,)), ((), ())),
                           preferred_element_type=F32)


def _mm_tn(a, b):
    return lax.dot_general(a.astype(BF16), b.astype(BF16), (((0,), (0,)), ((), ())),
                           preferred_element_type=F32)


def _const_spec(shape):
    n = len(shape)
    return pl.BlockSpec(shape, lambda *_: (0,) * n, pipeline_mode=pl.Buffered(1))


def _layer_spec(arr, layer):
    n = arr.ndim - 1
    return pl.BlockSpec((None,) + arr.shape[1:], lambda *_: (layer,) + (0,) * n,
                        pipeline_mode=pl.Buffered(1))


def _l2n_heads(x, nh, hd, mul):
    def one(h):
        xh = x[:, h * hd:(h + 1) * hd]
        return xh * (lax.rsqrt(jnp.sum(xh * xh, axis=-1, keepdims=True) + NORM_EPS) * mul)
    return jnp.concatenate([one(h) for h in range(nh)], axis=-1)


def _in_proj_kernel(x_ref, nw_ref, w_ref, sl_ref, gd_ref, *, n_sl, w3, n_ab, to_time_major):
    nb, tc, d = x_ref.shape
    rows = nb * tc
    x = x_ref[...].reshape(rows, d)
    h = _rms(x, nw_ref[...])
    p = _mm(h, w_ref[...])
    sl = p[:, :n_sl].reshape(nb, tc, n_sl)
    if to_time_major:
        sl = jnp.swapaxes(sl, 0, 1)
    sl_ref[...] = sl
    c_ab = n_sl + w3
    c_z = c_ab + n_ab
    wz = p.shape[1] - c_z
    gd_ref[:, :, 0:w3] = p[:, n_sl:c_ab].reshape(nb, tc, w3)
    gd_ref[:, :, w3:w3 + wz] = p[:, c_z:].reshape(nb, tc, wz)
    ab = jnp.concatenate([p[:, c_ab:c_z], jnp.zeros((rows, LANES - n_ab), F32)], axis=1)
    gd_ref[:, :, w3 + wz:] = ab.reshape(nb, tc, LANES)


def _in_proj(x3, p, layer, *, tc, to_time_major):
    nb, length, d = x3.shape
    nw, w, n_sl = p["n_mix_pre"], p["w_in"], p["n_sl"]
    n_tot = w.shape[-1]
    nh = p["gdn_nh"]
    w3 = 3 * nh * p["gdn_nw"].shape[-1]
    n_gd = n_tot - n_sl - 2 * nh + LANES
    if to_time_major:
        sl_shape, sl_spec = (length, nb, n_sl), pl.BlockSpec((tc, nb, n_sl), lambda i: (i, 0, 0))
    else:
        sl_shape, sl_spec = (nb, length, n_sl), pl.BlockSpec((nb, tc, n_sl), lambda i: (0, i, 0))
    return pl.pallas_call(
        functools.partial(_in_proj_kernel, n_sl=n_sl, w3=w3, n_ab=2 * nh, to_time_major=to_time_major),
        grid=(length // tc,),
        in_specs=[pl.BlockSpec((nb, tc, d), lambda i: (0, i, 0)),
                  _layer_spec(nw, layer), _layer_spec(w, layer)],
        out_specs=[sl_spec, pl.BlockSpec((nb, tc, n_gd), lambda i: (0, i, 0))],
        out_shape=[jax.ShapeDtypeStruct(sl_shape, F32), jax.ShapeDtypeStruct((nb, length, n_gd), F32)],
        compiler_params=pltpu.CompilerParams(dimension_semantics=("arbitrary",),
                                             vmem_limit_bytes=VMEM_LIMIT),
        name="in_proj",
    )(x3, nw, w)


def _scan_kernel(*refs, nb, tc, s5w, lw, first_pos_is_zero, zero_state):
    (sl_ref, lre_ref, lim_ref, ldt_ref, bre_ref, bim_ref, cre_ref, cim_ref, d_ref,
     wglu_ref, bglu_ref, cw_ref, cb_ref, wa_ref, ba_ref, wx_ref, bx_ref, lam_ref) = refs[:18]
    n_in = 18 if zero_state else 22
    (y_ref, xr_out, xi_out, h_out, cv_out,
     ar_s, ai_s, bbr_s, bbi_s, crb_s, cib_s, xr_c, xi_c, h_c, xr_s, xi_s, xpad_s, a_s, b_s) = refs[n_in:]
    i = pl.program_id(0)
    last = pl.num_programs(0) - 1
    rows = nb * tc

    @pl.when(i == 0)
    def _init():
        lr = jnp.minimum(lre_ref[...], -1e-4)
        li = lim_ref[...]
        dt = jnp.exp(ldt_ref[...])
        mag = jnp.exp(lr * dt)
        ar = mag * jnp.cos(li * dt)
        ai = mag * jnp.sin(li * dt)
        den = lr * lr + li * li
        fr = ((ar - 1.0) * lr + ai * li) / den
        fi = (ai * lr - (ar - 1.0) * li) / den
        ar_s[...] = ar
        ai_s[...] = ai
        bre = bre_ref[...]
        bim = bim_ref[...]
        bbr_s[...] = (fr * bre - fi * bim).astype(BF16)
        bbi_s[...] = (fr * bim + fi * bre).astype(BF16)
        crb_s[...] = cre_ref[...].astype(BF16)
        cib_s[...] = cim_ref[...].astype(BF16)
        if zero_state:
            xr_c[...] = jnp.zeros_like(xr_c)
            xi_c[...] = jnp.zeros_like(xi_c)
            h_c[...] = jnp.zeros_like(h_c)
            xpad_s[0:3 * nb, :] = jnp.zeros((3 * nb, lw), F32)
        else:
            xr0_ref, xi0_ref, h0_ref, cv0_ref = refs[18:22]
            xr_c[...] = xr0_ref[...]
            xi_c[...] = xi0_ref[...]
            h_c[...] = h0_ref[...]
            xpad_s[0:3 * nb, :] = cv0_ref[...]

    u = sl_ref[:, 0:s5w]
    xb = sl_ref[:, s5w:s5w + lw]
    gate = sl_ref[:, s5w + lw:s5w + 2 * lw]

    ub = u.astype(BF16)
    xr_s[...] = jnp.dot(ub, bbr_s[...], preferred_element_type=F32)
    xi_s[...] = jnp.dot(ub, bbi_s[...], preferred_element_type=F32)

    xpad_s[3 * nb:3 * nb + rows, :] = xb
    cw = cw_ref[...]
    xc = (xpad_s[0:rows, :] * cw[0:1] + xpad_s[nb:nb + rows, :] * cw[1:2]
          + xpad_s[2 * nb:2 * nb + rows, :] * cw[2:3] + xb * cw[3:4]) + cb_ref[...]
    new_cv = xpad_s[rows:rows + 3 * nb, :]
    xpad_s[0:3 * nb, :] = new_cv
    r = _sigmoid(_mm(xc, wa_ref[...]) + ba_ref[...])
    gi = _sigmoid(_mm(xc, wx_ref[...]) + bx_ref[...])
    log_a = -LRU_C * r * _softplus(-lam_ref[...])
    a = jnp.exp(log_a)
    m2 = -jnp.tanh(log_a) * (a * a + 1.0)
    mult = jnp.where(m2 > 0.0, m2 * lax.rsqrt(m2), 0.0)
    if first_pos_is_zero:
        rid = lax.broadcasted_iota(jnp.int32, (rows, 1), 0)
        mult = jnp.where(jnp.logical_and(rid < nb, i == 0), 1.0, mult)
    a_s[...] = a
    b_s[...] = mult * gi * xc

    ar = jnp.broadcast_to(ar_s[...], (nb, ar_s.shape[1]))
    ai = jnp.broadcast_to(ai_s[...], (nb, ai_s.shape[1]))

    def step(t, carry):
        xr, xi, h = carry
        sl_t = pl.ds(pl.multiple_of(t * nb, nb), nb)
        nxr = ar * xr - ai * xi + xr_s[sl_t, :]
        nxi = ar * xi + ai * xr + xi_s[sl_t, :]
        nh = a_s[sl_t, :] * h + b_s[sl_t, :]
        xr_s[sl_t, :] = nxr
        xi_s[sl_t, :] = nxi
        b_s[sl_t, :] = nh
        return nxr, nxi, nh

    xr, xi, h = lax.fori_loop(0, tc, step, (xr_c[...], xi_c[...], h_c[...]),
                              unroll=(True if tc <= 8 else 8))
    xr_c[...] = xr
    xi_c[...] = xi
    h_c[...] = h

    y = (jnp.dot(xr_s[...].astype(BF16), crb_s[...], preferred_element_type=F32)
         - jnp.dot(xi_s[...].astype(BF16), cib_s[...], preferred_element_type=F32))
    y = jax.nn.gelu(y + d_ref[...] * u)
    y = y * _sigmoid(_mm(y, wglu_ref[...]) + bglu_ref[...])
    y_ref[:, 0:s5w] = y
    y_ref[:, s5w:s5w + lw] = b_s[...] * jax.nn.gelu(gate)

    @pl.when(i == last)
    def _fin():
        xr_out[...] = xr
        xi_out[...] = xi
        h_out[...] = h
        cv_out[...] = new_cv


def _scan(sl2, p, layer, states, *, nb, tc, first_pos_is_zero):
    rows_total, n_sl = sl2.shape
    s5w = p["s5_d"].shape[-1]
    lw = p["lru_lambda"].shape[-1]
    n_state = p["s5_lre"].shape[-1]
    rows = nb * tc
    params = [p["s5_lre"], p["s5_lim"], p["s5_ldt"], p["s5_bre"], p["s5_bim"], p["s5_cre"], p["s5_cim"],
              p["s5_d"], p["s5_wglu"], p["s5_bglu"], p["lru_cw"], p["lru_cb"], p["lru_wa"], p["lru_ba"],
              p["lru_wx"], p["lru_bx"], p["lru_lambda"]]
    zero_state = states is None
    states = [] if zero_state else list(states)
    return pl.pallas_call(
        functools.partial(_scan_kernel, nb=nb, tc=tc, s5w=s5w, lw=lw, first_pos_is_zero=first_pos_is_zero,
                          zero_state=zero_state),
        grid=(rows_total // rows,),
        in_specs=[pl.BlockSpec((rows, n_sl), lambda i: (i, 0))]
        + [_layer_spec(a, layer) for a in params] + [_const_spec(a.shape) for a in states],
        out_specs=[pl.BlockSpec((rows, s5w + lw), lambda i: (i, 0)),
                   pl.BlockSpec((nb, n_state), lambda i: (0, 0)),
                   pl.BlockSpec((nb, n_state), lambda i: (0, 0)),
                   pl.BlockSpec((nb, lw), lambda i: (0, 0)),
                   pl.BlockSpec((3 * nb, lw), lambda i: (0, 0))],
        out_shape=[jax.ShapeDtypeStruct((rows_total, s5w + lw), F32),
                   jax.ShapeDtypeStruct((nb, n_state), F32),
                   jax.ShapeDtypeStruct((nb, n_state), F32),
                   jax.ShapeDtypeStruct((nb, lw), F32),
                   jax.ShapeDtypeStruct((3 * nb, lw), F32)],
        scratch_shapes=[pltpu.VMEM((1, n_state), F32), pltpu.VMEM((1, n_state), F32),
                        pltpu.VMEM((s5w, n_state), BF16), pltpu.VMEM((s5w, n_state), BF16),
                        pltpu.VMEM((n_state, s5w), BF16), pltpu.VMEM((n_state, s5w), BF16),
                        pltpu.VMEM((nb, n_state), F32), pltpu.VMEM((nb, n_state), F32),
                        pltpu.VMEM((nb, lw), F32),
                        pltpu.VMEM((rows, n_state), F32), pltpu.VMEM((rows, n_state), F32),
                        pltpu.VMEM((rows + 3 * nb, lw), F32),
                        pltpu.VMEM((rows, lw), F32), pltpu.VMEM((rows, lw), F32)],
        compiler_params=pltpu.CompilerParams(dimension_semantics=("arbitrary",),
                                             vmem_limit_bytes=VMEM_LIMIT),
        name="scan",
    )(sl2, *params, *states)


def _gdn_prompt_kernel(gd_ref, cw_ref, alog_ref, dtb_ref, nw_ref, y_ref, s_out, cv_out,
                       s_s, xp_s, *, ns, rg, nh, hd):
    i = pl.program_id(1)
    last = pl.num_programs(1) - 1
    w3 = 3 * nh * hd
    wz = nh * hd
    c = GDN_CHUNK
    nc = rg // c
    wc = nh * c
    rt = ns * rg

    @pl.when(i == 0)
    def _init():
        s_s[...] = jnp.zeros_like(s_s)
        xp_s[...] = jnp.zeros_like(xp_s)

    cw = cw_ref[...]
    xcs = []
    for s in range(ns):
        qkv = gd_ref[s, :, 0:w3]
        xe = jnp.concatenate([xp_s[s], qkv], axis=0)
        acc = qkv * cw[3:4]
        for j in range(1, 4):
            acc = acc + pltpu.roll(xe, j, 0)[SUBLANES:, :] * cw[3 - j:4 - j]
        xp_s[s] = qkv[rg - SUBLANES:, :]
        xcs.append(acc)
    xc = jnp.concatenate(xcs, axis=0)
    xc = xc * _sigmoid(xc)

    ab = jnp.concatenate([gd_ref[s, :, w3 + wz:w3 + wz + LANES] for s in range(ns)], axis=0)
    g_all = -jnp.exp(alog_ref[...]) * _softplus(ab + dtb_ref[...])
    beta_all = _sigmoid(ab)

    def per_head(fn):
        return jnp.concatenate([fn(h) for h in range(nh)], axis=-1)

    q_all = _l2n_heads(xc[:, 0:wz], nh, hd, hd ** -0.5)
    k_all = _l2n_heads(xc[:, wz:2 * wz], nh, hd, 1.0)
    v_all = xc[:, 2 * wz:3 * wz]
    beta_w = per_head(lambda h: jnp.broadcast_to(beta_all[:, nh + h:nh + h + 1], (rt, hd)))
    kb_all = k_all * beta_w
    vb_all = v_all * beta_w

    ri = lax.broadcasted_iota(jnp.int32, (c, wc), 0)
    cj = lax.broadcasted_iota(jnp.int32, (c, wc), 1) % c
    incl = ri >= cj
    strict = ri > cj
    eye = (ri == cj).astype(F32)
    r1 = lax.broadcasted_iota(jnp.int32, (c, c), 0)
    c1 = lax.broadcasted_iota(jnp.int32, (c, c), 1)
    ltri = (r1 >= c1).astype(F32)
    bd_cc = (lax.broadcasted_iota(jnp.int32, (wc, wc), 0) // c
             == lax.broadcasted_iota(jnp.int32, (wc, wc), 1) // c)
    bd_cd = (lax.broadcasted_iota(jnp.int32, (wc, wz), 0) // c
             == lax.broadcasted_iota(jnp.int32, (wc, wz), 1) // hd)

    def bdiag(x, mask):
        return jnp.where(mask, jnp.concatenate([x] * nh, axis=0), 0.0).astype(BF16)

    nc_all = ns * nc
    rows = [slice(cc * c, (cc + 1) * c) for cc in range(nc_all)]
    gcum = [jnp.dot(ltri, g_all[r], precision=HIGHEST, preferred_element_type=F32) for r in rows]
    cgc = [per_head(lambda h, x=x: jnp.broadcast_to(x[:, h:h + 1], (c, hd))) for x in gcum]
    gcol = [per_head(lambda h, x=x: jnp.broadcast_to(x[:, h:h + 1], (c, c))) for x in gcum]
    grow = [jnp.concatenate([xt[h:h + 1, :] for h in range(nh)], axis=1)
            for xt in [x.T for x in gcum]]
    every = range(nc_all)
    decay = [jnp.exp(jnp.where(incl, gcol[n] - grow[n], -jnp.inf)) for n in every]
    eg = [jnp.exp(x) for x in cgc]
    g_last = [x[c - 1:c, :] for x in cgc]
    kk = [_mm_nt(jnp.concatenate([kb_all[r], q_all[r]], axis=0), bdiag(k_all[r], bd_cd)) for r in rows]
    attn = [kk[n][c:2 * c] * decay[n] for n in every]
    rj = [-jnp.where(strict, kk[n][0:c] * decay[n], 0.0) for n in every]
    sj = [eye + rj[n] for n in every]
    rj = [_mm(rj[n], bdiag(rj[n], bd_cc)) for n in every]
    for _ in range(4):
        rs = [_mm(jnp.concatenate([rj[n], sj[n]], axis=0), bdiag(rj[n], bd_cc)) for n in every]
        rj = [x[0:c] for x in rs]
        sj = [sj[n] + rs[n][c:2 * c] for n in every]
    sj = [sj[n] + _mm(sj[n], bdiag(rj[n], bd_cc)) for n in every]
    w = [_mm(sj[n], jnp.concatenate([bdiag(vb_all[rows[n]], bd_cd),
                                     bdiag(kb_all[rows[n]] * eg[n], bd_cd)], axis=1)) for n in every]
    q_dec = [q_all[rows[n]] * eg[n] for n in every]
    k_dec = [k_all[rows[n]] * jnp.exp(g_last[n] - cgc[n]) for n in every]
    s_dec = [jnp.exp(x) for x in g_last]

    s_cur = [[s_s[s, h] for h in range(nh)] for s in range(ns)]
    hs = [slice(h * hd, (h + 1) * hd) for h in range(nh)]
    for m in range(nc):
        ws = [[_mm(jnp.concatenate([w[s * nc + m][:, wz + h * hd:wz + (h + 1) * hd],
                                    q_dec[s * nc + m][:, hs[h]]], axis=0), s_cur[s][h]) for h in range(nh)]
              for s in range(ns)]
        v_new = [jnp.concatenate([w[s * nc + m][:, hs[h]] - ws[s][h][0:c] for h in range(nh)], axis=-1)
                 for s in range(ns)]
        o = [jnp.concatenate([x[c:2 * c] for x in ws[s]], axis=-1) + _mm(attn[s * nc + m], bdiag(v_new[s], bd_cd))
             for s in range(ns)]
        s_cur = [[s_cur[s][h] * s_dec[s * nc + m][:, hs[h]]
                  + _mm_tn(k_dec[s * nc + m][:, hs[h]], v_new[s][:, hs[h]]) for h in range(nh)]
                 for s in range(ns)]
        for s in range(ns):
            for h in range(nh):
                z = gd_ref[s, m * c:(m + 1) * c, w3 + h * hd:w3 + (h + 1) * hd]
                y_ref[s, m * c:(m + 1) * c, hs[h]] = _rms(o[s][:, hs[h]], nw_ref[...]) * (z * _sigmoid(z))
    for s in range(ns):
        for h in range(nh):
            s_s[s, h] = s_cur[s][h]

    @pl.when(i == last)
    def _fin():
        for s in range(ns):
            for h in range(nh):
                s_out[s, h] = s_cur[s][h]
            cv_out[s] = xp_s[s, SUBLANES - 3:SUBLANES, :]


def _gdn_prompt(gd3, p, layer, *, rg, ns):
    nb, length, n_gd = gd3.shape
    nh = p["gdn_nh"]
    hd = p["gdn_nw"].shape[-1]
    w3 = 3 * nh * hd
    consts = [p["gdn_cw"], p["gdn_alog"], p["gdn_dtb"], p["gdn_nw"]]
    return pl.pallas_call(
        functools.partial(_gdn_prompt_kernel, ns=ns, rg=rg, nh=nh, hd=hd),
        grid=(nb // ns, length // rg),
        in_specs=[pl.BlockSpec((ns, rg, n_gd), lambda b, i: (b, i, 0))]
        + [_layer_spec(a, layer) for a in consts],
        out_specs=[pl.BlockSpec((ns, rg, nh * hd), lambda b, i: (b, i, 0)),
                   pl.BlockSpec((ns, nh, hd, hd), lambda b, i: (b, 0, 0, 0)),
                   pl.BlockSpec((ns, 3, w3), lambda b, i: (b, 0, 0))],
        out_shape=[jax.ShapeDtypeStruct((nb, length, nh * hd), F32),
                   jax.ShapeDtypeStruct((nb, nh, hd, hd), F32),
                   jax.ShapeDtypeStruct((nb, 3, w3), F32)],
        scratch_shapes=[pltpu.VMEM((ns, nh, hd, hd), F32), pltpu.VMEM((ns, SUBLANES, w3), F32)],
        compiler_params=pltpu.CompilerParams(dimension_semantics=("arbitrary", "arbitrary"),
                                             vmem_limit_bytes=VMEM_LIMIT),
        name="gdn_prompt",
    )(gd3, *consts)


def _gdn_sample_kernel(*refs, nt, nbk, nh, hd, layer, has_acc):
    gd_ref, cv0_ref, s0_ref, cw_ref, alog_ref, dtb_ref, nw_ref = refs[:7]
    y_ref, s_out, cv_out, wk_s, kd_s, wv_s, ws_s = refs[7 + int(has_acc):]
    if not has_acc:
        for other in range(s_out.shape[0]):
            if other != layer:
                s_out[other] = jnp.zeros(s_out.shape[1:], F32)
        s_out = s_out.at[layer]
    w3 = 3 * nh * hd
    wz = nh * hd
    cw = cw_ref[...]
    scale = hd ** -0.5
    xs = [cv0_ref[j] for j in range(3)] + [gd_ref[t, :, 0:w3] for t in range(nt)]
    for j in range(3):
        cv_out[j] = xs[nt + j]
    q, k, v, gc, beta = [], [], [], [], []
    for t in range(nt):
        xc = xs[t] * cw[0:1] + xs[t + 1] * cw[1:2] + xs[t + 2] * cw[2:3] + xs[t + 3] * cw[3:4]
        xc = xc * _sigmoid(xc)
        ab = gd_ref[t, :, w3 + wz:w3 + wz + LANES]
        g_t = -jnp.exp(alog_ref[...]) * _softplus(ab + dtb_ref[...])
        gc.append(g_t if t == 0 else gc[-1] + g_t)
        beta.append(_sigmoid(ab))
        qn = _l2n_heads(xc[:, 0:wz], nh, hd, scale)
        kn = _l2n_heads(xc[:, wz:2 * wz], nh, hd, 1.0)
        q.append([qn[:, h * hd:(h + 1) * hd] for h in range(nh)])
        k.append([kn[:, h * hd:(h + 1) * hd] for h in range(nh)])
        v.append([xc[:, 2 * wz + h * hd:2 * wz + (h + 1) * hd] for h in range(nh)])

    def dot(a, b):
        return jnp.sum(a * b, axis=-1, keepdims=True)

    wk_rows = [[None] * nh for _ in range(nt)]
    qd_rows = [[None] * nh for _ in range(nt)]
    kd_rows = [[None] * nh for _ in range(nt)]
    wv_rows = [[None] * nh for _ in range(nt)]
    sd_rows = [None] * nh
    attn = [None] * nh
    for h in range(nh):
        gch = [gc[t][:, h:h + 1] for t in range(nt)]
        bh = [beta[t][:, nh + h:nh + h + 1] for t in range(nt)]
        e = [jnp.exp(x) for x in gch]
        kb = [k[t][h] * bh[t] for t in range(nt)]
        vb = [v[t][h] * bh[t] for t in range(nt)]
        kbe = [kb[t] * e[t] for t in range(nt)]
        dec = [[jnp.exp(gch[i] - gch[j]) for j in range(i)] for i in range(nt)]
        m = [[dot(kb[i], k[j][h]) * dec[i][j] for j in range(i)] for i in range(nt)]
        attn[h] = [[dot(q[i][h], k[j][h]) * (dec[i][j] if j < i else 1.0) for j in range(i + 1)]
                   for i in range(nt)]
        tinv = [[None] * nt for _ in range(nt)]
        for j in range(nt):
            for i in range(j + 1, nt):
                acc = m[i][j]
                for l in range(j + 1, i):
                    acc = acc + m[i][l] * tinv[l][j]
                tinv[i][j] = -acc
        for i in range(nt):
            wv_i, wk_i = vb[i], kbe[i]
            for j in range(i):
                wv_i = wv_i + tinv[i][j] * vb[j]
                wk_i = wk_i + tinv[i][j] * kbe[j]
            wv_rows[i][h] = wv_i
            wk_rows[i][h] = wk_i
            qd_rows[i][h] = q[i][h] * e[i]
            kd_rows[i][h] = k[i][h] * jnp.exp(gch[nt - 1] - gch[i])
        sd_rows[h] = jnp.broadcast_to(e[nt - 1], (nbk, hd))

    cat = lambda parts: jnp.concatenate(parts, axis=-1)
    to_seq = lambda slabs: jnp.swapaxes(jnp.stack(slabs, axis=0), 0, 1)
    zero = jnp.zeros((nbk, wz), F32)
    wk_s[...] = to_seq([cat(wk_rows[i]) for i in range(nt)] + [cat(qd_rows[i]) for i in range(nt)])
    kd_s[...] = to_seq([cat(kd_rows[i]) for i in range(nt)] + [zero] * nt)
    wv_s[...] = to_seq([cat(wv_rows[i]) for i in range(nt)] + [cat(sd_rows)] * nt)

    group = min(DECODE_SEQ_UNROLL, nbk)

    def per_group(gi, _):
        pairs = [(gi * group + u, h) for u in range(group) for h in range(nh)]
        tiles = {}
        for u in range(group):
            b = gi * group + u
            tiles[u] = (wk_s[b], kd_s[b], wv_s[b])
        cols = [slice(h * hd, (h + 1) * hd) for h in range(nh)]
        s_old = [s0_ref[b, h] for b, h in pairs]
        r = [_mm(tiles[n // nh][0][:, cols[h]], s_old[n]) for n, (b, h) in enumerate(pairs)]
        for n, (b, h) in enumerate(pairs):
            tkd, twv = tiles[n // nh][1][:, cols[h]], tiles[n // nh][2][:, cols[h]]
            s_out[b, h] = s_old[n] * twv[nt:nt + 1, :] + _mm_tn(tkd, twv - r[n])
            ws_s[b, :, cols[h]] = r[n]
        return 0

    lax.fori_loop(0, nbk // group, per_group, 0)

    ws_tm = jnp.swapaxes(ws_s[...], 0, 1)
    for h in range(nh):
        cols = slice(h * hd, (h + 1) * hd)
        v_new = [wv_rows[j][h] - ws_tm[j][:, cols] for j in range(nt)]
        for i in range(nt):
            o = ws_tm[nt + i][:, cols]
            for j in range(i + 1):
                o = o + attn[h][i][j] * v_new[j]
            z = gd_ref[i, :, w3 + h * hd:w3 + (h + 1) * hd]
            y_ref[i, :, cols] = _rms(o, nw_ref[...]) * (z * _sigmoid(z))


def _gdn_sample(gd3, cv0, s_all, s_acc, layer, p, *, nbk):
    nt, nb, n_gd = gd3.shape
    nh = p["gdn_nh"]
    hd = p["gdn_nw"].shape[-1]
    w3 = 3 * nh * hd
    wz = nh * hd
    s_spec = pl.BlockSpec((None, nbk, nh, hd, hd), lambda i: (layer, i, 0, 0, 0))
    consts = [p["gdn_cw"], p["gdn_alog"], p["gdn_dtb"], p["gdn_nw"]]
    args = [gd3, cv0, s_all] + consts
    in_specs = [pl.BlockSpec((nt, nbk, n_gd), lambda i: (0, i, 0)),
                pl.BlockSpec((3, nbk, w3), lambda i: (0, i, 0)),
                s_spec] + [_layer_spec(a, layer) for a in consts]
    aliases = {}
    s_out_spec = s_spec
    if s_acc is not None:
        aliases = {len(args): 1}
        args.append(s_acc)
        in_specs.append(pl.BlockSpec(memory_space=pl.ANY))
    else:
        s_out_spec = pl.BlockSpec((s_all.shape[0], nbk, nh, hd, hd), lambda i: (0, i, 0, 0, 0))
    return pl.pallas_call(
        functools.partial(_gdn_sample_kernel, nt=nt, nbk=nbk, nh=nh, hd=hd, layer=layer,
                          has_acc=s_acc is not None),
        grid=(nb // nbk,),
        in_specs=in_specs,
        out_specs=[pl.BlockSpec((nt, nbk, wz), lambda i: (0, i, 0)),
                   s_out_spec,
                   pl.BlockSpec((3, nbk, w3), lambda i: (0, i, 0))],
        out_shape=[jax.ShapeDtypeStruct((nt, nb, wz), F32),
                   jax.ShapeDtypeStruct(s_all.shape, F32),
                   jax.ShapeDtypeStruct((3, nb, w3), F32)],
        scratch_shapes=[pltpu.VMEM((nbk, 2 * nt, wz), F32) for _ in range(4)],
        input_output_aliases=aliases,
        compiler_params=pltpu.CompilerParams(dimension_semantics=("arbitrary",),
                                             vmem_limit_bytes=VMEM_LIMIT),
        name="gdn_sample",
    )(*args)


def _ffn_kernel(x_ref, ysl_ref, yg_ref, wo_ref, nmp_ref, nfp_ref, nfq_ref, wgu_ref, wd_ref, o_ref,
                *, from_time_major):
    nb, tc, d = x_ref.shape
    rows = nb * tc
    x = x_ref[...].reshape(rows, d)
    ysl = ysl_ref[...]
    if from_time_major:
        ysl = jnp.swapaxes(ysl, 0, 1)
    ysl = ysl.reshape(rows, ysl.shape[-1])
    yg = yg_ref[...].reshape(rows, yg_ref.shape[-1])
    mix = jnp.concatenate([ysl, yg], axis=-1)
    x = x + _rms(_mm(mix, wo_ref[...]), nmp_ref[...])
    h = _rms(x, nfp_ref[...])
    gu = _mm(h, wgu_ref[...])
    hid = gu.shape[1] // 2
    gt = gu[:, :hid]
    f = _mm(gt * _sigmoid(gt) * gu[:, hid:], wd_ref[...])
    x = x + _rms(f, nfq_ref[...])
    o_ref[...] = x.reshape(nb, tc, d)


def _ffn(x3, ysl3, yg3, p, layer, *, tc, from_time_major):
    nb, length, d = x3.shape
    n1 = ysl3.shape[-1]
    n2 = yg3.shape[-1]
    consts = [p["w_out"], p["n_mix_post"], p["n_ffn_pre"], p["n_ffn_post"], p["w_gu"], p["w_down"]]
    if from_time_major:
        ysl_spec = pl.BlockSpec((tc, nb, n1), lambda i: (i, 0, 0))
    else:
        ysl_spec = pl.BlockSpec((nb, tc, n1), lambda i: (0, i, 0))
    return pl.pallas_call(
        functools.partial(_ffn_kernel, from_time_major=from_time_major),
        grid=(length // tc,),
        in_specs=[pl.BlockSpec((nb, tc, d), lambda i: (0, i, 0)), ysl_spec,
                  pl.BlockSpec((nb, tc, n2), lambda i: (0, i, 0))]
        + [_layer_spec(a, layer) for a in consts],
        out_specs=pl.BlockSpec((nb, tc, d), lambda i: (0, i, 0)),
        out_shape=jax.ShapeDtypeStruct((nb, length, d), F32),
        compiler_params=pltpu.CompilerParams(dimension_semantics=("arbitrary",),
                                             vmem_limit_bytes=VMEM_LIMIT),
        name="ffn",
    )(x3, ysl3, yg3, *consts)


def _block_diag(blocks):
    dep, n, r, c = blocks.shape
    eye = jnp.eye(n, dtype=blocks.dtype)
    return (eye[None, :, None, :, None] * blocks[:, :, :, None, :]).reshape(dep, n * r, n * c)


def _params(norm_mix_pre, norm_mix_post, norm_ffn_pre, norm_ffn_post, w_in, s5_lambda_re,
            s5_lambda_im, s5_log_dt, s5_b_re, s5_b_im, s5_c_re, s5_c_im, s5_d, s5_w_glu, s5_b_glu,
            lru_conv_w, lru_conv_b, lru_w_a, lru_b_a, lru_w_x, lru_b_x, lru_lambda, gdn_conv_w,
            gdn_a_log, gdn_dt_bias, gdn_norm_w, w_out, ffn_w_gate_up, ffn_w_down):
    dep, g, pst, hgrp = s5_b_re.shape
    s5w = g * hgrp
    lw = lru_lambda.shape[1]
    nh = gdn_a_log.shape[1]
    row = lambda v: v[:, None, :]
    lanes = lambda v: jnp.pad(v, ((0, 0), (0, LANES - v.shape[1])))[:, None, :]
    return {
        "n_sl": s5w + 2 * lw,
        "n_mix_pre": row(norm_mix_pre), "n_mix_post": row(norm_mix_post),
        "n_ffn_pre": row(norm_ffn_pre), "n_ffn_post": row(norm_ffn_post),
        "w_in": w_in.astype(BF16),
        "s5_lre": s5_lambda_re.reshape(dep, 1, g * pst), "s5_lim": s5_lambda_im.reshape(dep, 1, g * pst),
        "s5_ldt": jnp.repeat(s5_log_dt, pst, axis=1)[:, None, :],
        "s5_bre": _block_diag(jnp.swapaxes(s5_b_re, 2, 3)),
        "s5_bim": _block_diag(jnp.swapaxes(s5_b_im, 2, 3)),
        "s5_cre": _block_diag(jnp.swapaxes(s5_c_re, 2, 3)),
        "s5_cim": _block_diag(jnp.swapaxes(s5_c_im, 2, 3)),
        "s5_d": row(s5_d), "s5_wglu": s5_w_glu.astype(BF16), "s5_bglu": row(s5_b_glu),
        "lru_cw": lru_conv_w, "lru_cb": row(lru_conv_b),
        "lru_wa": _block_diag(lru_w_a).astype(BF16), "lru_ba": row(lru_b_a),
        "lru_wx": _block_diag(lru_w_x).astype(BF16), "lru_bx": row(lru_b_x),
        "lru_lambda": row(lru_lambda),
        "gdn_cw": gdn_conv_w, "gdn_alog": lanes(gdn_a_log),
        "gdn_dtb": lanes(gdn_dt_bias), "gdn_nw": row(gdn_norm_w), "gdn_nh": nh,
        "w_out": w_out.astype(BF16), "w_gu": ffn_w_gate_up.astype(BF16),
        "w_down": ffn_w_down.astype(BF16),
    }


def _prompt_layer(x, p, layer, *, tc, rg):
    nb, length, _ = x.shape
    sl_tm, gd = _in_proj(x, p, layer, tc=min(IN_PROJ_TOKENS_PER_SEQ_STEP, length), to_time_major=True)
    lw = p["lru_lambda"].shape[-1]
    ysl, xr, xi, hl, cv = _scan(sl_tm.reshape(length * nb, -1), p, layer, None, nb=nb, tc=tc,
                                first_pos_is_zero=True)
    yg, s_new, gcv = _gdn_prompt(gd, p, layer, rg=rg, ns=GDN_SEQS_PER_STEP if nb % GDN_SEQS_PER_STEP == 0 else 1)
    x = _ffn(x, ysl.reshape(length, nb, -1), yg, p, layer, tc=tc, from_time_major=True)
    lcv = jnp.swapaxes(cv.reshape(3, nb, lw), 0, 1)
    return x, (xr, xi, hl, lcv, s_new, gcv)


def _sample_layer(x_tm, st, gdn_s_all, gdn_s_acc, layer, p, *, nbk):
    s5_re, s5_im, lru_h, lru_conv, gdn_conv = st
    nb = s5_re.shape[0]
    rows = x_tm.shape[1]
    nt = rows // nb
    sl, gd = _in_proj(x_tm, p, layer, tc=rows, to_time_major=False)
    lw = p["lru_lambda"].shape[-1]
    cv0 = jnp.swapaxes(lru_conv, 0, 1).reshape(3 * nb, lw)
    states = [s5_re.reshape(nb, -1), s5_im.reshape(nb, -1), lru_h, cv0]
    ysl, xr, xi, hl, cv = _scan(sl.reshape(rows, -1), p, layer, states, nb=nb, tc=nt,
                                first_pos_is_zero=(PAST_LEN == 0))
    yg, s_acc, gcv = _gdn_sample(gd.reshape(nt, nb, -1), jnp.swapaxes(gdn_conv, 0, 1), gdn_s_all, gdn_s_acc,
                                 layer, p, nbk=nbk)
    x_tm = _ffn(x_tm, ysl.reshape(1, rows, -1), yg.reshape(1, rows, -1), p, layer, tc=rows,
                from_time_major=False)
    lcv = jnp.swapaxes(cv.reshape(3, nb, lw), 0, 1)
    return x_tm, (xr, xi, hl, lcv, jnp.swapaxes(gcv, 0, 1)), s_acc


def kernel(x_prompt, x_sample, state_s5_re, state_s5_im, state_lru_h, state_lru_conv, state_gdn_S, state_gdn_conv, norm_mix_pre, norm_mix_post, norm_ffn_pre, norm_ffn_post, w_in, s5_lambda_re, s5_lambda_im, s5_log_dt, s5_b_re, s5_b_im, s5_c_re, s5_c_im, s5_d, s5_w_glu, s5_b_glu, lru_conv_w, lru_conv_b, lru_w_a, lru_b_a, lru_w_x, lru_b_x, lru_lambda, gdn_conv_w, gdn_a_log, gdn_dt_bias, gdn_norm_w, w_out, ffn_w_gate_up, ffn_w_down):
    depth = w_in.shape[0]
    bp, lp, d = x_prompt.shape
    bs, ls, _ = x_sample.shape
    g, pst = s5_lambda_re.shape[1:]
    tc = min(TOKENS_PER_SEQ_STEP, lp)
    rg = min(GDN_ROWS_PER_STEP, lp)
    y_p = x_prompt
    y_s = jnp.swapaxes(x_sample, 0, 1).reshape(1, ls * bs, d)
    new_p = [[] for _ in range(6)]
    new_s = [[] for _ in range(5)]
    s_gdn_s = None
    p = _params(norm_mix_pre, norm_mix_post, norm_ffn_pre, norm_ffn_post, w_in, s5_lambda_re,
                s5_lambda_im, s5_log_dt, s5_b_re, s5_b_im, s5_c_re, s5_c_im, s5_d, s5_w_glu,
                s5_b_glu, lru_conv_w, lru_conv_b, lru_w_a, lru_b_a, lru_w_x, lru_b_x, lru_lambda,
                gdn_conv_w, gdn_a_log, gdn_dt_bias, gdn_norm_w, w_out, ffn_w_gate_up, ffn_w_down)
    for l in range(depth):
        y_p, sp = _prompt_layer(y_p, p, l, tc=tc, rg=rg)
        st_s = (state_s5_re[l], state_s5_im[l], state_lru_h[l], state_lru_conv[l], state_gdn_conv[l])
        y_s, ss, s_gdn_s = _sample_layer(y_s, st_s, state_gdn_S, s_gdn_s, l, p,
                                         nbk=min(DECODE_SEQS_PER_STEP, bs))
        for j in range(6):
            new_p[j].append(sp[j])
        for j in range(5):
            new_s[j].append(ss[j])
    y_s = jnp.swapaxes(y_s.reshape(ls, bs, d), 0, 1)
    outs_p = [jnp.stack(t, axis=0) for t in new_p]
    outs_s = [jnp.stack(t, axis=0) for t in new_s]
    outs_s.insert(4, s_gdn_s)
    outs_p[0] = outs_p[0].reshape(depth, bp, g, pst)
    outs_p[1] = outs_p[1].reshape(depth, bp, g, pst)
    outs_s[0] = outs_s[0].reshape(depth, bs, g, pst)
    outs_s[1] = outs_s[1].reshape(depth, bs, g, pst)
    return (y_p, y_s, *outs_p, *outs_s)
```

```python
import functools

import jax
import jax.numpy as jnp
from jax import lax
from jax.experimental import pallas as pl
from jax.experimental.pallas import tpu as pltpu

F32 = jnp.float32
BF16 = jnp.bfloat16
HIGHEST = lax.Precision.HIGHEST

NORM_EPS = 1e-6
LRU_C = 8.0
GDN_CHUNK = 64
PAST_LEN = 16384
LANES = 128
SUBLANES = 8
VMEM_LIMIT = 56 * 1024 * 1024

TOKENS_PER_SEQ_STEP = 64
IN_PROJ_TOKENS_PER_SEQ_STEP = 128
GDN_ROWS_PER_STEP = 4 * GDN_CHUNK
GDN_SEQS_PER_STEP = 4
DECODE_SEQS_PER_STEP = 16
DECODE_SEQ_UNROLL = 4
ROW_GROUPS = 2


def _rms(x, w):
    return x * lax.rsqrt(jnp.mean(x * x, axis=-1, keepdims=True) + NORM_EPS) * w


def _sigmoid(x):
    return jax.nn.sigmoid(x)


def _softplus(x):
    return jnp.maximum(x, 0.0) + jnp.log1p(jnp.exp(-jnp.abs(x)))


def _mm(a, b):
    return jnp.dot(a.astype(BF16), b.astype(BF16), preferred_element_type=F32)


def _mm_nt(a, b):
    return lax.dot_general(a.astype(BF16), b.astype(BF16), (((1,), (1,)), ((), ())),
                           preferred_element_type=F32)


def _mm_tn(a, b):
    return lax.dot_general(a.astype(BF16), b.astype(BF16), (((0,), (0,)), ((), ())),
                           preferred_element_type=F32)


def _const_spec(shape):
    n = len(shape)
    return pl.BlockSpec(shape, lambda *_: (0,) * n, pipeline_mode=pl.Buffered(1))


def _layer_spec(arr, layer):
    n = arr.ndim - 1
    return pl.BlockSpec((None,) + arr.shape[1:], lambda *_: (layer,) + (0,) * n,
                        pipeline_mode=pl.Buffered(1))


def _l2n_heads(x, nh, hd, mul):
    def one(h):
        xh = x[:, h * hd:(h + 1) * hd]
        return xh * (lax.rsqrt(jnp.sum(xh * xh, axis=-1, keepdims=True) + NORM_EPS) * mul)
    return jnp.concatenate([one(h) for h in range(nh)], axis=-1)


def _in_proj_kernel(x_ref, nw_ref, w_ref, sl_ref, gd_ref, *, n_sl, w3, n_ab, to_time_major):
    nb, tc, d = x_ref.shape
    rows = nb * tc
    x = x_ref[...].reshape(rows, d)
    groups = [slice(k * rows // ROW_GROUPS, (k + 1) * rows // ROW_GROUPS) for k in range(ROW_GROUPS)]
    p = jnp.concatenate([_mm(_rms(x[r], nw_ref[...]), w_ref[...]) for r in groups], axis=0)
    sl = p[:, :n_sl].reshape(nb, tc, n_sl)
    if to_time_major:
        sl = jnp.swapaxes(sl, 0, 1)
    sl_ref[...] = sl
    c_ab = n_sl + w3
    c_z = c_ab + n_ab
    wz = p.shape[1] - c_z
    gd_ref[:, :, 0:w3] = p[:, n_sl:c_ab].reshape(nb, tc, w3)
    gd_ref[:, :, w3:w3 + wz] = p[:, c_z:].reshape(nb, tc, wz)
    ab = jnp.concatenate([p[:, c_ab:c_z], jnp.zeros((rows, LANES - n_ab), F32)], axis=1)
    gd_ref[:, :, w3 + wz:] = ab.reshape(nb, tc, LANES)


def _in_proj(x3, p, layer, *, tc, to_time_major):
    nb, length, d = x3.shape
    nw, w, n_sl = p["n_mix_pre"], p["w_in"], p["n_sl"]
    n_tot = w.shape[-1]
    nh = p["gdn_nh"]
    w3 = 3 * nh * p["gdn_nw"].shape[-1]
    n_gd = n_tot - n_sl - 2 * nh + LANES
    if to_time_major:
        sl_shape, sl_spec = (length, nb, n_sl), pl.BlockSpec((tc, nb, n_sl), lambda i: (i, 0, 0))
    else:
        sl_shape, sl_spec = (nb, length, n_sl), pl.BlockSpec((nb, tc, n_sl), lambda i: (0, i, 0))
    return pl.pallas_call(
        functools.partial(_in_proj_kernel, n_sl=n_sl, w3=w3, n_ab=2 * nh, to_time_major=to_time_major),
        grid=(length // tc,),
        in_specs=[pl.BlockSpec((nb, tc, d), lambda i: (0, i, 0)),
                  _layer_spec(nw, layer), _layer_spec(w, layer)],
        out_specs=[sl_spec, pl.BlockSpec((nb, tc, n_gd), lambda i: (0, i, 0))],
        out_shape=[jax.ShapeDtypeStruct(sl_shape, F32), jax.ShapeDtypeStruct((nb, length, n_gd), F32)],
        compiler_params=pltpu.CompilerParams(dimension_semantics=("arbitrary",),
                                             vmem_limit_bytes=VMEM_LIMIT),
        name="in_proj",
    )(x3, nw, w)


def _scan_kernel(*refs, nb, tc, s5w, lw, first_pos_is_zero, zero_state):
    (sl_ref, lre_ref, lim_ref, ldt_ref, bre_ref, bim_ref, cre_ref, cim_ref, d_ref,
     wglu_ref, bglu_ref, cw_ref, cb_ref, wa_ref, ba_ref, wx_ref, bx_ref, lam_ref) = refs[:18]
    n_in = 18 if zero_state else 22
    (y_ref, xr_out, xi_out, h_out, cv_out,
     ar_s, ai_s, bbr_s, bbi_s, crb_s, cib_s, xr_c, xi_c, h_c, xr_s, xi_s, xpad_s, a_s, b_s) = refs[n_in:]
    i = pl.program_id(0)
    last = pl.num_programs(0) - 1
    rows = nb * tc

    @pl.when(i == 0)
    def _init():
        lr = jnp.minimum(lre_ref[...], -1e-4)
        li = lim_ref[...]
        dt = jnp.exp(ldt_ref[...])
        mag = jnp.exp(lr * dt)
        ar = mag * jnp.cos(li * dt)
        ai = mag * jnp.sin(li * dt)
        den = lr * lr + li * li
        fr = ((ar - 1.0) * lr + ai * li) / den
        fi = (ai * lr - (ar - 1.0) * li) / den
        ar_s[...] = ar
        ai_s[...] = ai
        bre = bre_ref[...]
        bim = bim_ref[...]
        bbr_s[...] = (fr * bre - fi * bim).astype(BF16)
        bbi_s[...] = (fr * bim + fi * bre).astype(BF16)
        crb_s[...] = cre_ref[...].astype(BF16)
        cib_s[...] = cim_ref[...].astype(BF16)
        if zero_state:
            xr_c[...] = jnp.zeros_like(xr_c)
            xi_c[...] = jnp.zeros_like(xi_c)
            h_c[...] = jnp.zeros_like(h_c)
            xpad_s[0:3 * nb, :] = jnp.zeros((3 * nb, lw), F32)
        else:
            xr0_ref, xi0_ref, h0_ref, cv0_ref = refs[18:22]
            xr_c[...] = xr0_ref[...]
            xi_c[...] = xi0_ref[...]
            h_c[...] = h0_ref[...]
            xpad_s[0:3 * nb, :] = cv0_ref[...]

    u = sl_ref[:, 0:s5w]
    xb = sl_ref[:, s5w:s5w + lw]
    gate = sl_ref[:, s5w + lw:s5w + 2 * lw]

    ub = u.astype(BF16)
    xr_s[...] = jnp.dot(ub, bbr_s[...], preferred_element_type=F32)
    xi_s[...] = jnp.dot(ub, bbi_s[...], preferred_element_type=F32)

    xpad_s[3 * nb:3 * nb + rows, :] = xb
    cw = cw_ref[...]
    xc = (xpad_s[0:rows, :] * cw[0:1] + xpad_s[nb:nb + rows, :] * cw[1:2]
          + xpad_s[2 * nb:2 * nb + rows, :] * cw[2:3] + xb * cw[3:4]) + cb_ref[...]
    new_cv = xpad_s[rows:rows + 3 * nb, :]
    xpad_s[0:3 * nb, :] = new_cv
    r = _sigmoid(_mm(xc, wa_ref[...]) + ba_ref[...])
    gi = _sigmoid(_mm(xc, wx_ref[...]) + bx_ref[...])
    log_a = -LRU_C * r * _softplus(-lam_ref[...])
    a = jnp.exp(log_a)
    m2 = -jnp.tanh(log_a) * (a * a + 1.0)
    mult = jnp.where(m2 > 0.0, m2 * lax.rsqrt(m2), 0.0)
    if first_pos_is_zero:
        rid = lax.broadcasted_iota(jnp.int32, (rows, 1), 0)
        mult = jnp.where(jnp.logical_and(rid < nb, i == 0), 1.0, mult)
    a_s[...] = a
    b_s[...] = mult * gi * xc

    ar = jnp.broadcast_to(ar_s[...], (nb, ar_s.shape[1]))
    ai = jnp.broadcast_to(ai_s[...], (nb, ai_s.shape[1]))

    def step(t, carry):
        xr, xi, h = carry
        sl_t = pl.ds(pl.multiple_of(t * nb, nb), nb)
        nxr = ar * xr - ai * xi + xr_s[sl_t, :]
        nxi = ar * xi + ai * xr + xi_s[sl_t, :]
        nh = a_s[sl_t, :] * h + b_s[sl_t, :]
        xr_s[sl_t, :] = nxr
        xi_s[sl_t, :] = nxi
        b_s[sl_t, :] = nh
        return nxr, nxi, nh

    xr, xi, h = lax.fori_loop(0, tc, step, (xr_c[...], xi_c[...], h_c[...]),
                              unroll=(True if tc <= 8 else 8))
    xr_c[...] = xr
    xi_c[...] = xi
    h_c[...] = h

    y = (jnp.dot(xr_s[...].astype(BF16), crb_s[...], preferred_element_type=F32)
         - jnp.dot(xi_s[...].astype(BF16), cib_s[...], preferred_element_type=F32))
    y = jax.nn.gelu(y + d_ref[...] * u)
    y = y * _sigmoid(_mm(y, wglu_ref[...]) + bglu_ref[...])
    y_ref[:, 0:s5w] = y
    y_ref[:, s5w:s5w + lw] = b_s[...] * jax.nn.gelu(gate)

    @pl.when(i == last)
    def _fin():
        xr_out[...] = xr
        xi_out[...] = xi
        h_out[...] = h
        cv_out[...] = new_cv


def _scan(sl2, p, layer, states, *, nb, tc, first_pos_is_zero):
    rows_total, n_sl = sl2.shape
    s5w = p["s5_d"].shape[-1]
    lw = p["lru_lambda"].shape[-1]
    n_state = p["s5_lre"].shape[-1]
    rows = nb * tc
    params = [p["s5_lre"], p["s5_lim"], p["s5_ldt"], p["s5_bre"], p["s5_bim"], p["s5_cre"], p["s5_cim"],
              p["s5_d"], p["s5_wglu"], p["s5_bglu"], p["lru_cw"], p["lru_cb"], p["lru_wa"], p["lru_ba"],
              p["lru_wx"], p["lru_bx"], p["lru_lambda"]]
    zero_state = states is None
    states = [] if zero_state else list(states)
    return pl.pallas_call(
        functools.partial(_scan_kernel, nb=nb, tc=tc, s5w=s5w, lw=lw, first_pos_is_zero=first_pos_is_zero,
                          zero_state=zero_state),
        grid=(rows_total // rows,),
        in_specs=[pl.BlockSpec((rows, n_sl), lambda i: (i, 0))]
        + [_layer_spec(a, layer) for a in params] + [_const_spec(a.shape) for a in states],
        out_specs=[pl.BlockSpec((rows, s5w + lw), lambda i: (i, 0)),
                   pl.BlockSpec((nb, n_state), lambda i: (0, 0)),
                   pl.BlockSpec((nb, n_state), lambda i: (0, 0)),
                   pl.BlockSpec((nb, lw), lambda i: (0, 0)),
                   pl.BlockSpec((3 * nb, lw), lambda i: (0, 0))],
        out_shape=[jax.ShapeDtypeStruct((rows_total, s5w + lw), F32),
                   jax.ShapeDtypeStruct((nb, n_state), F32),
                   jax.ShapeDtypeStruct((nb, n_state), F32),
                   jax.ShapeDtypeStruct((nb, lw), F32),
                   jax.ShapeDtypeStruct((3 * nb, lw), F32)],
        scratch_shapes=[pltpu.VMEM((1, n_state), F32), pltpu.VMEM((1, n_state), F32),
                        pltpu.VMEM((s5w, n_state), BF16), pltpu.VMEM((s5w, n_state), BF16),
                        pltpu.VMEM((n_state, s5w), BF16), pltpu.VMEM((n_state, s5w), BF16),
                        pltpu.VMEM((nb, n_state), F32), pltpu.VMEM((nb, n_state), F32),
                        pltpu.VMEM((nb, lw), F32),
                        pltpu.VMEM((rows, n_state), F32), pltpu.VMEM((rows, n_state), F32),
                        pltpu.VMEM((rows + 3 * nb, lw), F32),
                        pltpu.VMEM((rows, lw), F32), pltpu.VMEM((rows, lw), F32)],
        compiler_params=pltpu.CompilerParams(dimension_semantics=("arbitrary",),
                                             vmem_limit_bytes=VMEM_LIMIT),
        name="scan",
    )(sl2, *params, *states)


def _gdn_prompt_kernel(gd_ref, cw_ref, alog_ref, dtb_ref, nw_ref, y_ref, s_out, cv_out,
                       s_s, xp_s, *, ns, rg, nh, hd):
    i = pl.program_id(1)
    last = pl.num_programs(1) - 1
    w3 = 3 * nh * hd
    wz = nh * hd
    c = GDN_CHUNK
    nc = rg // c
    wc = nh * c
    rt = ns * rg

    @pl.when(i == 0)
    def _init():
        s_s[...] = jnp.zeros_like(s_s)
        xp_s[...] = jnp.zeros_like(xp_s)

    cw = cw_ref[...]
    xcs = []
    for s in range(ns):
        qkv = gd_ref[s, :, 0:w3]
        xe = jnp.concatenate([xp_s[s], qkv], axis=0)
        acc = qkv * cw[3:4]
        for j in range(1, 4):
            acc = acc + pltpu.roll(xe, j, 0)[SUBLANES:, :] * cw[3 - j:4 - j]
        xp_s[s] = qkv[rg - SUBLANES:, :]
        xcs.append(acc)
    xc = jnp.concatenate(xcs, axis=0)
    xc = xc * _sigmoid(xc)

    ab = jnp.concatenate([gd_ref[s, :, w3 + wz:w3 + wz + LANES] for s in range(ns)], axis=0)
    g_all = -jnp.exp(alog_ref[...]) * _softplus(ab + dtb_ref[...])
    beta_all = _sigmoid(ab)

    def per_head(fn):
        return jnp.concatenate([fn(h) for h in range(nh)], axis=-1)

    q_all = _l2n_heads(xc[:, 0:wz], nh, hd, hd ** -0.5)
    k_all = _l2n_heads(xc[:, wz:2 * wz], nh, hd, 1.0)
    v_all = xc[:, 2 * wz:3 * wz]
    beta_w = per_head(lambda h: jnp.broadcast_to(beta_all[:, nh + h:nh + h + 1], (rt, hd)))
    kb_all = k_all * beta_w
    vb_all = v_all * beta_w

    ri = lax.broadcasted_iota(jnp.int32, (c, wc), 0)
    cj = lax.broadcasted_iota(jnp.int32, (c, wc), 1) % c
    incl = ri >= cj
    strict = ri > cj
    eye = (ri == cj).astype(F32)
    r1 = lax.broadcasted_iota(jnp.int32, (c, c), 0)
    c1 = lax.broadcasted_iota(jnp.int32, (c, c), 1)
    ltri = (r1 >= c1).astype(F32)
    bd_cc = (lax.broadcasted_iota(jnp.int32, (wc, wc), 0) // c
             == lax.broadcasted_iota(jnp.int32, (wc, wc), 1) // c)
    bd_cd = (lax.broadcasted_iota(jnp.int32, (wc, wz), 0) // c
             == lax.broadcasted_iota(jnp.int32, (wc, wz), 1) // hd)

    def bdiag(x, mask):
        return jnp.where(mask, jnp.concatenate([x] * nh, axis=0), 0.0).astype(BF16)

    nc_all = ns * nc
    rows = [slice(cc * c, (cc + 1) * c) for cc in range(nc_all)]
    gcum = [jnp.dot(ltri, g_all[r], precision=HIGHEST, preferred_element_type=F32) for r in rows]
    cgc = [per_head(lambda h, x=x: jnp.broadcast_to(x[:, h:h + 1], (c, hd))) for x in gcum]
    gcol = [per_head(lambda h, x=x: jnp.broadcast_to(x[:, h:h + 1], (c, c))) for x in gcum]
    grow = [jnp.concatenate([xt[h:h + 1, :] for h in range(nh)], axis=1)
            for xt in [x.T for x in gcum]]
    every = range(nc_all)
    decay = [jnp.exp(jnp.where(incl, gcol[n] - grow[n], -jnp.inf)) for n in every]
    eg = [jnp.exp(x) for x in cgc]
    g_last = [x[c - 1:c, :] for x in cgc]
    kk = [_mm_nt(jnp.concatenate([kb_all[r], q_all[r]], axis=0), bdiag(k_all[r], bd_cd)) for r in rows]
    attn = [kk[n][c:2 * c] * decay[n] for n in every]
    rj = [-jnp.where(strict, kk[n][0:c] * decay[n], 0.0) for n in every]
    sj = [eye + rj[n] for n in every]
    rj = [_mm(rj[n], bdiag(rj[n], bd_cc)) for n in every]
    for _ in range(4):
        rs = [_mm(jnp.concatenate([rj[n], sj[n]], axis=0), bdiag(rj[n], bd_cc)) for n in every]
        rj = [x[0:c] for x in rs]
        sj = [sj[n] + rs[n][c:2 * c] for n in every]
    sj = [sj[n] + _mm(sj[n], bdiag(rj[n], bd_cc)) for n in every]
    w = [_mm(sj[n], jnp.concatenate([bdiag(vb_all[rows[n]], bd_cd),
                                     bdiag(kb_all[rows[n]] * eg[n], bd_cd)], axis=1)) for n in every]
    q_dec = [q_all[rows[n]] * eg[n] for n in every]
    k_dec = [k_all[rows[n]] * jnp.exp(g_last[n] - cgc[n]) for n in every]
    s_dec = [jnp.exp(x) for x in g_last]

    s_cur = [[s_s[s, h] for h in range(nh)] for s in range(ns)]
    hs = [slice(h * hd, (h + 1) * hd) for h in range(nh)]
    for m in range(nc):
        ws = [[_mm(jnp.concatenate([w[s * nc + m][:, wz + h * hd:wz + (h + 1) * hd],
                                    q_dec[s * nc + m][:, hs[h]]], axis=0), s_cur[s][h]) for h in range(nh)]
              for s in range(ns)]
        v_new = [jnp.concatenate([w[s * nc + m][:, hs[h]] - ws[s][h][0:c] for h in range(nh)], axis=-1)
                 for s in range(ns)]
        o = [jnp.concatenate([x[c:2 * c] for x in ws[s]], axis=-1) + _mm(attn[s * nc + m], bdiag(v_new[s], bd_cd))
             for s in range(ns)]
        s_cur = [[s_cur[s][h] * s_dec[s * nc + m][:, hs[h]]
                  + _mm_tn(k_dec[s * nc + m][:, hs[h]], v_new[s][:, hs[h]]) for h in range(nh)]
                 for s in range(ns)]
        for s in range(ns):
            for h in range(nh):
                z = gd_ref[s, m * c:(m + 1) * c, w3 + h * hd:w3 + (h + 1) * hd]
                y_ref[s, m * c:(m + 1) * c, hs[h]] = _rms(o[s][:, hs[h]], nw_ref[...]) * (z * _sigmoid(z))
    for s in range(ns):
        for h in range(nh):
            s_s[s, h] = s_cur[s][h]

    @pl.when(i == last)
    def _fin():
        for s in range(ns):
            for h in range(nh):
                s_out[s, h] = s_cur[s][h]
            cv_out[s] = xp_s[s, SUBLANES - 3:SUBLANES, :]


def _gdn_prompt(gd3, p, layer, *, rg, ns):
    nb, length, n_gd = gd3.shape
    nh = p["gdn_nh"]
    hd = p["gdn_nw"].shape[-1]
    w3 = 3 * nh * hd
    consts = [p["gdn_cw"], p["gdn_alog"], p["gdn_dtb"], p["gdn_nw"]]
    return pl.pallas_call(
        functools.partial(_gdn_prompt_kernel, ns=ns, rg=rg, nh=nh, hd=hd),
        grid=(nb // ns, length // rg),
        in_specs=[pl.BlockSpec((ns, rg, n_gd), lambda b, i: (b, i, 0))]
        + [_layer_spec(a, layer) for a in consts],
        out_specs=[pl.BlockSpec((ns, rg, nh * hd), lambda b, i: (b, i, 0)),
                   pl.BlockSpec((ns, nh, hd, hd), lambda b, i: (b, 0, 0, 0)),
                   pl.BlockSpec((ns, 3, w3), lambda b, i: (b, 0, 0))],
        out_shape=[jax.ShapeDtypeStruct((nb, length, nh * hd), F32),
                   jax.ShapeDtypeStruct((nb, nh, hd, hd), F32),
                   jax.ShapeDtypeStruct((nb, 3, w3), F32)],
        scratch_shapes=[pltpu.VMEM((ns, nh, hd, hd), F32), pltpu.VMEM((ns, SUBLANES, w3), F32)],
        compiler_params=pltpu.CompilerParams(dimension_semantics=("arbitrary", "arbitrary"),
                                             vmem_limit_bytes=VMEM_LIMIT),
        name="gdn_prompt",
    )(gd3, *consts)


def _gdn_sample_kernel(*refs, nt, nbk, nh, hd, layer, has_acc):
    gd_ref, cv0_ref, s0_ref, cw_ref, alog_ref, dtb_ref, nw_ref = refs[:7]
    y_ref, s_out, cv_out, wk_s, kd_s, wv_s, ws_s = refs[7 + int(has_acc):]
    if not has_acc:
        for other in range(s_out.shape[0]):
            if other != layer:
                s_out[other] = jnp.zeros(s_out.shape[1:], F32)
        s_out = s_out.at[layer]
    w3 = 3 * nh * hd
    wz = nh * hd
    cw = cw_ref[...]
    scale = hd ** -0.5
    xs = [cv0_ref[j] for j in range(3)] + [gd_ref[t, :, 0:w3] for t in range(nt)]
    for j in range(3):
        cv_out[j] = xs[nt + j]
    q, k, v, gc, beta = [], [], [], [], []
    for t in range(nt):
        xc = xs[t] * cw[0:1] + xs[t + 1] * cw[1:2] + xs[t + 2] * cw[2:3] + xs[t + 3] * cw[3:4]
        xc = xc * _sigmoid(xc)
        ab = gd_ref[t, :, w3 + wz:w3 + wz + LANES]
        g_t = -jnp.exp(alog_ref[...]) * _softplus(ab + dtb_ref[...])
        gc.append(g_t if t == 0 else gc[-1] + g_t)
        beta.append(_sigmoid(ab))
        qn = _l2n_heads(xc[:, 0:wz], nh, hd, scale)
        kn = _l2n_heads(xc[:, wz:2 * wz], nh, hd, 1.0)
        q.append([qn[:, h * hd:(h + 1) * hd] for h in range(nh)])
        k.append([kn[:, h * hd:(h + 1) * hd] for h in range(nh)])
        v.append([xc[:, 2 * wz + h * hd:2 * wz + (h + 1) * hd] for h in range(nh)])

    def dot(a, b):
        return jnp.sum(a * b, axis=-1, keepdims=True)

    wk_rows = [[None] * nh for _ in range(nt)]
    qd_rows = [[None] * nh for _ in range(nt)]
    kd_rows = [[None] * nh for _ in range(nt)]
    wv_rows = [[None] * nh for _ in range(nt)]
    sd_rows = [None] * nh
    attn = [None] * nh
    for h in range(nh):
        gch = [gc[t][:, h:h + 1] for t in range(nt)]
        bh = [beta[t][:, nh + h:nh + h + 1] for t in range(nt)]
        e = [jnp.exp(x) for x in gch]
        kb = [k[t][h] * bh[t] for t in range(nt)]
        vb = [v[t][h] * bh[t] for t in range(nt)]
        kbe = [kb[t] * e[t] for t in range(nt)]
        dec = [[jnp.exp(gch[i] - gch[j]) for j in range(i)] for i in range(nt)]
        m = [[dot(kb[i], k[j][h]) * dec[i][j] for j in range(i)] for i in range(nt)]
        attn[h] = [[dot(q[i][h], k[j][h]) * (dec[i][j] if j < i else 1.0) for j in range(i + 1)]
                   for i in range(nt)]
        tinv = [[None] * nt for _ in range(nt)]
        for j in range(nt):
            for i in range(j + 1, nt):
                acc = m[i][j]
                for l in range(j + 1, i):
                    acc = acc + m[i][l] * tinv[l][j]
                tinv[i][j] = -acc
        for i in range(nt):
            wv_i, wk_i = vb[i], kbe[i]
            for j in range(i):
                wv_i = wv_i + tinv[i][j] * vb[j]
                wk_i = wk_i + tinv[i][j] * kbe[j]
            wv_rows[i][h] = wv_i
            wk_rows[i][h] = wk_i
            qd_rows[i][h] = q[i][h] * e[i]
            kd_rows[i][h] = k[i][h] * jnp.exp(gch[nt - 1] - gch[i])
        sd_rows[h] = jnp.broadcast_to(e[nt - 1], (nbk, hd))

    cat = lambda parts: jnp.concatenate(parts, axis=-1)
    to_seq = lambda slabs: jnp.swapaxes(jnp.stack(slabs, axis=0), 0, 1)
    zero = jnp.zeros((nbk, wz), F32)
    wk_s[...] = to_seq([cat(wk_rows[i]) for i in range(nt)] + [cat(qd_rows[i]) for i in range(nt)])
    kd_s[...] = to_seq([cat(kd_rows[i]) for i in range(nt)] + [zero] * nt)
    wv_s[...] = to_seq([cat(wv_rows[i]) for i in range(nt)] + [cat(sd_rows)] * nt)

    group = min(DECODE_SEQ_UNROLL, nbk)

    def per_group(gi, _):
        pairs = [(gi * group + u, h) for u in range(group) for h in range(nh)]
        tiles = {}
        for u in range(group):
            b = gi * group + u
            tiles[u] = (wk_s[b], kd_s[b], wv_s[b])
        cols = [slice(h * hd, (h + 1) * hd) for h in range(nh)]
        s_old = [s0_ref[b, h] for b, h in pairs]
        r = [_mm(tiles[n // nh][0][:, cols[h]], s_old[n]) for n, (b, h) in enumerate(pairs)]
        for n, (b, h) in enumerate(pairs):
            tkd, twv = tiles[n // nh][1][:, cols[h]], tiles[n // nh][2][:, cols[h]]
            s_out[b, h] = s_old[n] * twv[nt:nt + 1, :] + _mm_tn(tkd, twv - r[n])
            ws_s[b, :, cols[h]] = r[n]
        return 0

    lax.fori_loop(0, nbk // group, per_group, 0)

    ws_tm = jnp.swapaxes(ws_s[...], 0, 1)
    for h in range(nh):
        cols = slice(h * hd, (h + 1) * hd)
        v_new = [wv_rows[j][h] - ws_tm[j][:, cols] for j in range(nt)]
        for i in range(nt):
            o = ws_tm[nt + i][:, cols]
            for j in range(i + 1):
                o = o + attn[h][i][j] * v_new[j]
            z = gd_ref[i, :, w3 + h * hd:w3 + (h + 1) * hd]
            y_ref[i, :, cols] = _rms(o, nw_ref[...]) * (z * _sigmoid(z))


def _gdn_sample(gd3, cv0, s_all, s_acc, layer, p, *, nbk):
    nt, nb, n_gd = gd3.shape
    nh = p["gdn_nh"]
    hd = p["gdn_nw"].shape[-1]
    w3 = 3 * nh * hd
    wz = nh * hd
    s_spec = pl.BlockSpec((None, nbk, nh, hd, hd), lambda i: (layer, i, 0, 0, 0))
    consts = [p["gdn_cw"], p["gdn_alog"], p["gdn_dtb"], p["gdn_nw"]]
    args = [gd3, cv0, s_all] + consts
    in_specs = [pl.BlockSpec((nt, nbk, n_gd), lambda i: (0, i, 0)),
                pl.BlockSpec((3, nbk, w3), lambda i: (0, i, 0)),
                s_spec] + [_layer_spec(a, layer) for a in consts]
    aliases = {}
    s_out_spec = s_spec
    if s_acc is not None:
        aliases = {len(args): 1}
        args.append(s_acc)
        in_specs.append(pl.BlockSpec(memory_space=pl.ANY))
    else:
        s_out_spec = pl.BlockSpec((s_all.shape[0], nbk, nh, hd, hd), lambda i: (0, i, 0, 0, 0))
    return pl.pallas_call(
        functools.partial(_gdn_sample_kernel, nt=nt, nbk=nbk, nh=nh, hd=hd, layer=layer,
                          has_acc=s_acc is not None),
        grid=(nb // nbk,),
        in_specs=in_specs,
        out_specs=[pl.BlockSpec((nt, nbk, wz), lambda i: (0, i, 0)),
                   s_out_spec,
                   pl.BlockSpec((3, nbk, w3), lambda i: (0, i, 0))],
        out_shape=[jax.ShapeDtypeStruct((nt, nb, wz), F32),
                   jax.ShapeDtypeStruct(s_all.shape, F32),
                   jax.ShapeDtypeStruct((3, nb, w3), F32)],
        scratch_shapes=[pltpu.VMEM((nbk, 2 * nt, wz), F32) for _ in range(4)],
        input_output_aliases=aliases,
        compiler_params=pltpu.CompilerParams(dimension_semantics=("arbitrary",),
                                             vmem_limit_bytes=VMEM_LIMIT),
        name="gdn_sample",
    )(*args)


def _ffn_kernel(x_ref, ysl_ref, yg_ref, wo_ref, nmp_ref, nfp_ref, nfq_ref, wgu_ref, wd_ref, o_ref,
                *, from_time_major):
    nb, tc, d = x_ref.shape
    rows = nb * tc
    x = x_ref[...].reshape(rows, d)
    ysl = ysl_ref[...]
    if from_time_major:
        ysl = jnp.swapaxes(ysl, 0, 1)
    ysl = ysl.reshape(rows, ysl.shape[-1])
    yg = yg_ref[...].reshape(rows, yg_ref.shape[-1])
    mix = jnp.concatenate([ysl, yg], axis=-1)
    groups = [slice(k * rows // ROW_GROUPS, (k + 1) * rows // ROW_GROUPS) for k in range(ROW_GROUPS)]
    hid = wd_ref.shape[0]
    o1 = [_mm(mix[r], wo_ref[...]) for r in groups]
    x1 = [x[r] + _rms(o1[k], nmp_ref[...]) for k, r in enumerate(groups)]
    gu = [_mm(_rms(x1[k], nfp_ref[...]), wgu_ref[...]) for k in range(ROW_GROUPS)]
    f = [_mm(g[:, :hid] * _sigmoid(g[:, :hid]) * g[:, hid:], wd_ref[...]) for g in gu]
    out = jnp.concatenate([x1[k] + _rms(f[k], nfq_ref[...]) for k in range(ROW_GROUPS)], axis=0)
    o_ref[...] = out.reshape(nb, tc, d)


def _ffn(x3, ysl3, yg3, p, layer, *, tc, from_time_major):
    nb, length, d = x3.shape
    n1 = ysl3.shape[-1]
    n2 = yg3.shape[-1]
    consts = [p["w_out"], p["n_mix_post"], p["n_ffn_pre"], p["n_ffn_post"], p["w_gu"], p["w_down"]]
    if from_time_major:
        ysl_spec = pl.BlockSpec((tc, nb, n1), lambda i: (i, 0, 0))
    else:
        ysl_spec = pl.BlockSpec((nb, tc, n1), lambda i: (0, i, 0))
    return pl.pallas_call(
        functools.partial(_ffn_kernel, from_time_major=from_time_major),
        grid=(length // tc,),
        in_specs=[pl.BlockSpec((nb, tc, d), lambda i: (0, i, 0)), ysl_spec,
                  pl.BlockSpec((nb, tc, n2), lambda i: (0, i, 0))]
        + [_layer_spec(a, layer) for a in consts],
        out_specs=pl.BlockSpec((nb, tc, d), lambda i: (0, i, 0)),
        out_shape=jax.ShapeDtypeStruct((nb, length, d), F32),
        compiler_params=pltpu.CompilerParams(dimension_semantics=("arbitrary",),
                                             vmem_limit_bytes=VMEM_LIMIT),
        name="ffn",
    )(x3, ysl3, yg3, *consts)


def _block_diag(blocks):
    dep, n, r, c = blocks.shape
    eye = jnp.eye(n, dtype=blocks.dtype)
    return (eye[None, :, None, :, None] * blocks[:, :, :, None, :]).reshape(dep, n * r, n * c)


def _params(norm_mix_pre, norm_mix_post, norm_ffn_pre, norm_ffn_post, w_in, s5_lambda_re,
            s5_lambda_im, s5_log_dt, s5_b_re, s5_b_im, s5_c_re, s5_c_im, s5_d, s5_w_glu, s5_b_glu,
            lru_conv_w, lru_conv_b, lru_w_a, lru_b_a, lru_w_x, lru_b_x, lru_lambda, gdn_conv_w,
            gdn_a_log, gdn_dt_bias, gdn_norm_w, w_out, ffn_w_gate_up, ffn_w_down):
    dep, g, pst, hgrp = s5_b_re.shape
    s5w = g * hgrp
    lw = lru_lambda.shape[1]
    nh = gdn_a_log.shape[1]
    row = lambda v: v[:, None, :]
    lanes = lambda v: jnp.pad(v, ((0, 0), (0, LANES - v.shape[1])))[:, None, :]
    return {
        "n_sl": s5w + 2 * lw,
        "n_mix_pre": row(norm_mix_pre), "n_mix_post": row(norm_mix_post),
        "n_ffn_pre": row(norm_ffn_pre), "n_ffn_post": row(norm_ffn_post),
        "w_in": w_in.astype(BF16),
        "s5_lre": s5_lambda_re.reshape(dep, 1, g * pst), "s5_lim": s5_lambda_im.reshape(dep, 1, g * pst),
        "s5_ldt": jnp.repeat(s5_log_dt, pst, axis=1)[:, None, :],
        "s5_bre": _block_diag(jnp.swapaxes(s5_b_re, 2, 3)),
        "s5_bim": _block_diag(jnp.swapaxes(s5_b_im, 2, 3)),
        "s5_cre": _block_diag(jnp.swapaxes(s5_c_re, 2, 3)),
        "s5_cim": _block_diag(jnp.swapaxes(s5_c_im, 2, 3)),
        "s5_d": row(s5_d), "s5_wglu": s5_w_glu.astype(BF16), "s5_bglu": row(s5_b_glu),
        "lru_cw": lru_conv_w, "lru_cb": row(lru_conv_b),
        "lru_wa": _block_diag(lru_w_a).astype(BF16), "lru_ba": row(lru_b_a),
        "lru_wx": _block_diag(lru_w_x).astype(BF16), "lru_bx": row(lru_b_x),
        "lru_lambda": row(lru_lambda),
        "gdn_cw": gdn_conv_w, "gdn_alog": lanes(gdn_a_log),
        "gdn_dtb": lanes(gdn_dt_bias), "gdn_nw": row(gdn_norm_w), "gdn_nh": nh,
        "w_out": w_out.astype(BF16), "w_gu": ffn_w_gate_up.astype(BF16),
        "w_down": ffn_w_down.astype(BF16),
    }


def _prompt_layer(x, p, layer, *, tc, rg):
    nb, length, _ = x.shape
    sl_tm, gd = _in_proj(x, p, layer, tc=min(IN_PROJ_TOKENS_PER_SEQ_STEP, length), to_time_major=True)
    lw = p["lru_lambda"].shape[-1]
    ysl, xr, xi, hl, cv = _scan(sl_tm.reshape(length * nb, -1), p, layer, None, nb=nb, tc=tc,
                                first_pos_is_zero=True)
    yg, s_new, gcv = _gdn_prompt(gd, p, layer, rg=rg, ns=GDN_SEQS_PER_STEP if nb % GDN_SEQS_PER_STEP == 0 else 1)
    x = _ffn(x, ysl.reshape(length, nb, -1), yg, p, layer, tc=tc, from_time_major=True)
    lcv = jnp.swapaxes(cv.reshape(3, nb, lw), 0, 1)
    return x, (xr, xi, hl, lcv, s_new, gcv)


def _sample_layer(x_tm, st, gdn_s_all, gdn_s_acc, layer, p, *, nbk):
    s5_re, s5_im, lru_h, lru_conv, gdn_conv = st
    nb = s5_re.shape[0]
    rows = x_tm.shape[1]
    nt = rows // nb
    sl, gd = _in_proj(x_tm, p, layer, tc=rows, to_time_major=False)
    lw = p["lru_lambda"].shape[-1]
    cv0 = jnp.swapaxes(lru_conv, 0, 1).reshape(3 * nb, lw)
    states = [s5_re.reshape(nb, -1), s5_im.reshape(nb, -1), lru_h, cv0]
    ysl, xr, xi, hl, cv = _scan(sl.reshape(rows, -1), p, layer, states, nb=nb, tc=nt,
                                first_pos_is_zero=(PAST_LEN == 0))
    yg, s_acc, gcv = _gdn_sample(gd.reshape(nt, nb, -1), jnp.swapaxes(gdn_conv, 0, 1), gdn_s_all, gdn_s_acc,
                                 layer, p, nbk=nbk)
    x_tm = _ffn(x_tm, ysl.reshape(1, rows, -1), yg.reshape(1, rows, -1), p, layer, tc=rows,
                from_time_major=False)
    lcv = jnp.swapaxes(cv.reshape(3, nb, lw), 0, 1)
    return x_tm, (xr, xi, hl, lcv, jnp.swapaxes(gcv, 0, 1)), s_acc


def kernel(x_prompt, x_sample, state_s5_re, state_s5_im, state_lru_h, state_lru_conv, state_gdn_S, state_gdn_conv, norm_mix_pre, norm_mix_post, norm_ffn_pre, norm_ffn_post, w_in, s5_lambda_re, s5_lambda_im, s5_log_dt, s5_b_re, s5_b_im, s5_c_re, s5_c_im, s5_d, s5_w_glu, s5_b_glu, lru_conv_w, lru_conv_b, lru_w_a, lru_b_a, lru_w_x, lru_b_x, lru_lambda, gdn_conv_w, gdn_a_log, gdn_dt_bias, gdn_norm_w, w_out, ffn_w_gate_up, ffn_w_down):
    depth = w_in.shape[0]
    bp, lp, d = x_prompt.shape
    bs, ls, _ = x_sample.shape
    g, pst = s5_lambda_re.shape[1:]
    tc = min(TOKENS_PER_SEQ_STEP, lp)
    rg = min(GDN_ROWS_PER_STEP, lp)
    y_p = x_prompt
    y_s = jnp.swapaxes(x_sample, 0, 1).reshape(1, ls * bs, d)
    new_p = [[] for _ in range(6)]
    new_s = [[] for _ in range(5)]
    s_gdn_s = None
    p = _params(norm_mix_pre, norm_mix_post, norm_ffn_pre, norm_ffn_post, w_in, s5_lambda_re,
                s5_lambda_im, s5_log_dt, s5_b_re, s5_b_im, s5_c_re, s5_c_im, s5_d, s5_w_glu,
                s5_b_glu, lru_conv_w, lru_conv_b, lru_w_a, lru_b_a, lru_w_x, lru_b_x, lru_lambda,
                gdn_conv_w, gdn_a_log, gdn_dt_bias, gdn_norm_w, w_out, ffn_w_gate_up, ffn_w_down)
    for l in range(depth):
        y_p, sp = _prompt_layer(y_p, p, l, tc=tc, rg=rg)
        st_s = (state_s5_re[l], state_s5_im[l], state_lru_h[l], state_lru_conv[l], state_gdn_conv[l])
        y_s, ss, s_gdn_s = _sample_layer(y_s, st_s, state_gdn_S, s_gdn_s, l, p,
                                         nbk=min(DECODE_SEQS_PER_STEP, bs))
        for j in range(6):
            new_p[j].append(sp[j])
        for j in range(5):
            new_s[j].append(ss[j])
    y_s = jnp.swapaxes(y_s.reshape(ls, bs, d), 0, 1)
    outs_p = [jnp.stack(t, axis=0) for t in new_p]
    outs_s = [jnp.stack(t, axis=0) for t in new_s]
    outs_s.insert(4, s_gdn_s)
    outs_p[0] = outs_p[0].reshape(depth, bp, g, pst)
    outs_p[1] = outs_p[1].reshape(depth, bp, g, pst)
    outs_s[0] = outs_s[0].reshape(depth, bs, g, pst)
    outs_s[1] = outs_s[1].reshape(depth, bs, g, pst)
    return (y_p, y_s, *outs_p, *outs_s)
```

```python
import functools

import jax
import jax.numpy as jnp
from jax import lax
from jax.experimental import pallas as pl
from jax.experimental.pallas import tpu as pltpu

F32 = jnp.float32
BF16 = jnp.bfloat16
HIGHEST = lax.Precision.HIGHEST

NORM_EPS = 1e-6
LRU_C = 8.0
GDN_CHUNK = 64
PAST_LEN = 16384
LANES = 128
SUBLANES = 8
VMEM_LIMIT = 56 * 1024 * 1024
FFN_VMEM_LIMIT = 60 * 1024 * 1024

TOKENS_PER_SEQ_STEP = 64
IN_PROJ_TOKENS_PER_SEQ_STEP = 128
FFN_TOKENS_PER_SEQ_STEP = 128
FFN_GROUP_ROWS = 256
GDN_ROWS_PER_STEP = 4 * GDN_CHUNK
GDN_SEQS_PER_STEP = 4
DECODE_SEQS_PER_STEP = 16
DECODE_SEQ_UNROLL = 4
ROW_GROUPS = 2


def _rms(x, w):
    return x * lax.rsqrt(jnp.mean(x * x, axis=-1, keepdims=True) + NORM_EPS) * w


def _sigmoid(x):
    return jax.nn.sigmoid(x)


def _softplus(x):
    return jnp.maximum(x, 0.0) + jnp.log1p(jnp.exp(-jnp.abs(x)))


def _mm(a, b):
    return jnp.dot(a.astype(BF16), b.astype(BF16), preferred_element_type=F32)


def _mm_nt(a, b):
    return lax.dot_general(a.astype(BF16), b.astype(BF16), (((1,), (1,)), ((), ())),
                           preferred_element_type=F32)


def _mm_tn(a, b):
    return lax.dot_general(a.astype(BF16), b.astype(BF16), (((0,), (0,)), ((), ())),
                           preferred_element_type=F32)


def _const_spec(shape):
    n = len(shape)
    return pl.BlockSpec(shape, lambda *_: (0,) * n, pipeline_mode=pl.Buffered(1))


def _layer_spec(arr, layer):
    n = arr.ndim - 1
    return pl.BlockSpec((None,) + arr.shape[1:], lambda *_: (layer,) + (0,) * n,
                        pipeline_mode=pl.Buffered(1))


def _l2n_heads(x, nh, hd, mul):
    def one(h):
        xh = x[:, h * hd:(h + 1) * hd]
        return xh * (lax.rsqrt(jnp.sum(xh * xh, axis=-1, keepdims=True) + NORM_EPS) * mul)
    return jnp.concatenate([one(h) for h in range(nh)], axis=-1)


def _in_proj_kernel(x_ref, nw_ref, w_ref, sl_ref, gd_ref, *, n_sl, w3, n_ab, to_time_major):
    nb, tc, d = x_ref.shape
    rows = nb * tc
    x = x_ref[...].reshape(rows, d)
    groups = [slice(k * rows // ROW_GROUPS, (k + 1) * rows // ROW_GROUPS) for k in range(ROW_GROUPS)]
    p = jnp.concatenate([_mm(_rms(x[r], nw_ref[...]), w_ref[...]) for r in groups], axis=0)
    sl = p[:, :n_sl].reshape(nb, tc, n_sl)
    if to_time_major:
        sl = jnp.swapaxes(sl, 0, 1)
    sl_ref[...] = sl
    c_ab = n_sl + w3
    c_z = c_ab + n_ab
    wz = p.shape[1] - c_z
    gd_ref[:, :, 0:w3] = p[:, n_sl:c_ab].reshape(nb, tc, w3)
    gd_ref[:, :, w3:w3 + wz] = p[:, c_z:].reshape(nb, tc, wz)
    ab = jnp.concatenate([p[:, c_ab:c_z], jnp.zeros((rows, LANES - n_ab), F32)], axis=1)
    gd_ref[:, :, w3 + wz:] = ab.reshape(nb, tc, LANES)


def _in_proj(x3, p, layer, *, tc, to_time_major):
    nb, length, d = x3.shape
    nw, w, n_sl = p["n_mix_pre"], p["w_in"], p["n_sl"]
    n_tot = w.shape[-1]
    nh = p["gdn_nh"]
    w3 = 3 * nh * p["gdn_nw"].shape[-1]
    n_gd = n_tot - n_sl - 2 * nh + LANES
    if to_time_major:
        sl_shape, sl_spec = (length, nb, n_sl), pl.BlockSpec((tc, nb, n_sl), lambda i: (i, 0, 0))
    else:
        sl_shape, sl_spec = (nb, length, n_sl), pl.BlockSpec((nb, tc, n_sl), lambda i: (0, i, 0))
    return pl.pallas_call(
        functools.partial(_in_proj_kernel, n_sl=n_sl, w3=w3, n_ab=2 * nh, to_time_major=to_time_major),
        grid=(length // tc,),
        in_specs=[pl.BlockSpec((nb, tc, d), lambda i: (0, i, 0)),
                  _layer_spec(nw, layer), _layer_spec(w, layer)],
        out_specs=[sl_spec, pl.BlockSpec((nb, tc, n_gd), lambda i: (0, i, 0))],
        out_shape=[jax.ShapeDtypeStruct(sl_shape, F32), jax.ShapeDtypeStruct((nb, length, n_gd), F32)],
        compiler_params=pltpu.CompilerParams(dimension_semantics=("arbitrary",),
                                             vmem_limit_bytes=VMEM_LIMIT),
        name="in_proj",
    )(x3, nw, w)


def _scan_kernel(*refs, nb, tc, s5w, lw, first_pos_is_zero, zero_state):
    (sl_ref, lre_ref, lim_ref, ldt_ref, bre_ref, bim_ref, cre_ref, cim_ref, d_ref,
     wglu_ref, bglu_ref, cw_ref, cb_ref, wa_ref, ba_ref, wx_ref, bx_ref, lam_ref) = refs[:18]
    n_in = 18 if zero_state else 22
    (y_ref, xr_out, xi_out, h_out, cv_out,
     ar_s, ai_s, bbr_s, bbi_s, crb_s, cib_s, xr_c, xi_c, h_c, xr_s, xi_s, xpad_s, a_s, b_s) = refs[n_in:]
    i = pl.program_id(0)
    last = pl.num_programs(0) - 1
    rows = nb * tc

    @pl.when(i == 0)
    def _init():
        lr = jnp.minimum(lre_ref[...], -1e-4)
        li = lim_ref[...]
        dt = jnp.exp(ldt_ref[...])
        mag = jnp.exp(lr * dt)
        ar = mag * jnp.cos(li * dt)
        ai = mag * jnp.sin(li * dt)
        den = lr * lr + li * li
        fr = ((ar - 1.0) * lr + ai * li) / den
        fi = (ai * lr - (ar - 1.0) * li) / den
        ar_s[...] = ar
        ai_s[...] = ai
        bre = bre_ref[...]
        bim = bim_ref[...]
        bbr_s[...] = (fr * bre - fi * bim).astype(BF16)
        bbi_s[...] = (fr * bim + fi * bre).astype(BF16)
        crb_s[...] = cre_ref[...].astype(BF16)
        cib_s[...] = cim_ref[...].astype(BF16)
        if zero_state:
            xr_c[...] = jnp.zeros_like(xr_c)
            xi_c[...] = jnp.zeros_like(xi_c)
            h_c[...] = jnp.zeros_like(h_c)
            xpad_s[0:3 * nb, :] = jnp.zeros((3 * nb, lw), F32)
        else:
            xr0_ref, xi0_ref, h0_ref, cv0_ref = refs[18:22]
            xr_c[...] = xr0_ref[...]
            xi_c[...] = xi0_ref[...]
            h_c[...] = h0_ref[...]
            xpad_s[0:3 * nb, :] = cv0_ref[...]

    u = sl_ref[:, 0:s5w]
    xb = sl_ref[:, s5w:s5w + lw]
    gate = sl_ref[:, s5w + lw:s5w + 2 * lw]

    ub = u.astype(BF16)
    xr_s[...] = jnp.dot(ub, bbr_s[...], preferred_element_type=F32)
    xi_s[...] = jnp.dot(ub, bbi_s[...], preferred_element_type=F32)

    xpad_s[3 * nb:3 * nb + rows, :] = xb
    cw = cw_ref[...]
    xc = (xpad_s[0:rows, :] * cw[0:1] + xpad_s[nb:nb + rows, :] * cw[1:2]
          + xpad_s[2 * nb:2 * nb + rows, :] * cw[2:3] + xb * cw[3:4]) + cb_ref[...]
    new_cv = xpad_s[rows:rows + 3 * nb, :]
    xpad_s[0:3 * nb, :] = new_cv
    r = _sigmoid(_mm(xc, wa_ref[...]) + ba_ref[...])
    gi = _sigmoid(_mm(xc, wx_ref[...]) + bx_ref[...])
    log_a = -LRU_C * r * _softplus(-lam_ref[...])
    a = jnp.exp(log_a)
    m2 = -jnp.tanh(log_a) * (a * a + 1.0)
    mult = jnp.where(m2 > 0.0, m2 * lax.rsqrt(m2), 0.0)
    if first_pos_is_zero:
        rid = lax.broadcasted_iota(jnp.int32, (rows, 1), 0)
        mult = jnp.where(jnp.logical_and(rid < nb, i == 0), 1.0, mult)
    a_s[...] = a
    b_s[...] = mult * gi * xc

    ar = jnp.broadcast_to(ar_s[...], (nb, ar_s.shape[1]))
    ai = jnp.broadcast_to(ai_s[...], (nb, ai_s.shape[1]))

    def step(t, carry):
        xr, xi, h = carry
        sl_t = pl.ds(pl.multiple_of(t * nb, nb), nb)
        nxr = ar * xr - ai * xi + xr_s[sl_t, :]
        nxi = ar * xi + ai * xr + xi_s[sl_t, :]
        nh = a_s[sl_t, :] * h + b_s[sl_t, :]
        xr_s[sl_t, :] = nxr
        xi_s[sl_t, :] = nxi
        b_s[sl_t, :] = nh
        return nxr, nxi, nh

    xr, xi, h = lax.fori_loop(0, tc, step, (xr_c[...], xi_c[...], h_c[...]),
                              unroll=(True if tc <= 8 else 8))
    xr_c[...] = xr
    xi_c[...] = xi
    h_c[...] = h

    y = (jnp.dot(xr_s[...].astype(BF16), crb_s[...], preferred_element_type=F32)
         - jnp.dot(xi_s[...].astype(BF16), cib_s[...], preferred_element_type=F32))
    y = jax.nn.gelu(y + d_ref[...] * u)
    y = y * _sigmoid(_mm(y, wglu_ref[...]) + bglu_ref[...])
    y_ref[:, 0:s5w] = y
    y_ref[:, s5w:s5w + lw] = b_s[...] * jax.nn.gelu(gate)

    @pl.when(i == last)
    def _fin():
        xr_out[...] = xr
        xi_out[...] = xi
        h_out[...] = h
        cv_out[...] = new_cv


def _scan(sl2, p, layer, states, *, nb, tc, first_pos_is_zero):
    rows_total, n_sl = sl2.shape
    s5w = p["s5_d"].shape[-1]
    lw = p["lru_lambda"].shape[-1]
    n_state = p["s5_lre"].shape[-1]
    rows = nb * tc
    params = [p["s5_lre"], p["s5_lim"], p["s5_ldt"], p["s5_bre"], p["s5_bim"], p["s5_cre"], p["s5_cim"],
              p["s5_d"], p["s5_wglu"], p["s5_bglu"], p["lru_cw"], p["lru_cb"], p["lru_wa"], p["lru_ba"],
              p["lru_wx"], p["lru_bx"], p["lru_lambda"]]
    zero_state = states is None
    states = [] if zero_state else list(states)
    return pl.pallas_call(
        functools.partial(_scan_kernel, nb=nb, tc=tc, s5w=s5w, lw=lw, first_pos_is_zero=first_pos_is_zero,
                          zero_state=zero_state),
        grid=(rows_total // rows,),
        in_specs=[pl.BlockSpec((rows, n_sl), lambda i: (i, 0))]
        + [_layer_spec(a, layer) for a in params] + [_const_spec(a.shape) for a in states],
        out_specs=[pl.BlockSpec((rows, s5w + lw), lambda i: (i, 0)),
                   pl.BlockSpec((nb, n_state), lambda i: (0, 0)),
                   pl.BlockSpec((nb, n_state), lambda i: (0, 0)),
                   pl.BlockSpec((nb, lw), lambda i: (0, 0)),
                   pl.BlockSpec((3 * nb, lw), lambda i: (0, 0))],
        out_shape=[jax.ShapeDtypeStruct((rows_total, s5w + lw), F32),
                   jax.ShapeDtypeStruct((nb, n_state), F32),
                   jax.ShapeDtypeStruct((nb, n_state), F32),
                   jax.ShapeDtypeStruct((nb, lw), F32),
                   jax.ShapeDtypeStruct((3 * nb, lw), F32)],
        scratch_shapes=[pltpu.VMEM((1, n_state), F32), pltpu.VMEM((1, n_state), F32),
                        pltpu.VMEM((s5w, n_state), BF16), pltpu.VMEM((s5w, n_state), BF16),
                        pltpu.VMEM((n_state, s5w), BF16), pltpu.VMEM((n_state, s5w), BF16),
                        pltpu.VMEM((nb, n_state), F32), pltpu.VMEM((nb, n_state), F32),
                        pltpu.VMEM((nb, lw), F32),
                        pltpu.VMEM((rows, n_state), F32), pltpu.VMEM((rows, n_state), F32),
                        pltpu.VMEM((rows + 3 * nb, lw), F32),
                        pltpu.VMEM((rows, lw), F32), pltpu.VMEM((rows, lw), F32)],
        compiler_params=pltpu.CompilerParams(dimension_semantics=("arbitrary",),
                                             vmem_limit_bytes=VMEM_LIMIT),
        name="scan",
    )(sl2, *params, *states)


def _gdn_prompt_kernel(gd_ref, cw_ref, alog_ref, dtb_ref, nw_ref, y_ref, s_out, cv_out,
                       s_s, xp_s, *, ns, rg, nh, hd):
    i = pl.program_id(1)
    last = pl.num_programs(1) - 1
    w3 = 3 * nh * hd
    wz = nh * hd
    c = GDN_CHUNK
    nc = rg // c
    wc = nh * c
    rt = ns * rg

    @pl.when(i == 0)
    def _init():
        s_s[...] = jnp.zeros_like(s_s)
        xp_s[...] = jnp.zeros_like(xp_s)

    cw = cw_ref[...]
    xcs = []
    for s in range(ns):
        qkv = gd_ref[s, :, 0:w3]
        xe = jnp.concatenate([xp_s[s], qkv], axis=0)
        acc = qkv * cw[3:4]
        for j in range(1, 4):
            acc = acc + pltpu.roll(xe, j, 0)[SUBLANES:, :] * cw[3 - j:4 - j]
        xp_s[s] = qkv[rg - SUBLANES:, :]
        xcs.append(acc)
    xc = jnp.concatenate(xcs, axis=0)
    xc = xc * _sigmoid(xc)

    ab = jnp.concatenate([gd_ref[s, :, w3 + wz:w3 + wz + LANES] for s in range(ns)], axis=0)
    g_all = -jnp.exp(alog_ref[...]) * _softplus(ab + dtb_ref[...])
    beta_all = _sigmoid(ab)

    def per_head(fn):
        return jnp.concatenate([fn(h) for h in range(nh)], axis=-1)

    q_all = _l2n_heads(xc[:, 0:wz], nh, hd, hd ** -0.5)
    k_all = _l2n_heads(xc[:, wz:2 * wz], nh, hd, 1.0)
    v_all = xc[:, 2 * wz:3 * wz]
    beta_w = per_head(lambda h: jnp.broadcast_to(beta_all[:, nh + h:nh + h + 1], (rt, hd)))
    kb_all = k_all * beta_w
    vb_all = v_all * beta_w

    ri = lax.broadcasted_iota(jnp.int32, (c, wc), 0)
    cj = lax.broadcasted_iota(jnp.int32, (c, wc), 1) % c
    incl = ri >= cj
    strict = ri > cj
    eye = (ri == cj).astype(F32)
    r1 = lax.broadcasted_iota(jnp.int32, (c, c), 0)
    c1 = lax.broadcasted_iota(jnp.int32, (c, c), 1)
    ltri = (r1 >= c1).astype(F32)
    bd_cc = (lax.broadcasted_iota(jnp.int32, (wc, wc), 0) // c
             == lax.broadcasted_iota(jnp.int32, (wc, wc), 1) // c)
    bd_cd = (lax.broadcasted_iota(jnp.int32, (wc, wz), 0) // c
             == lax.broadcasted_iota(jnp.int32, (wc, wz), 1) // hd)

    def bdiag(x, mask):
        return jnp.where(mask, jnp.concatenate([x] * nh, axis=0), 0.0).astype(BF16)

    nc_all = ns * nc
    rows = [slice(cc * c, (cc + 1) * c) for cc in range(nc_all)]
    gcum = [jnp.dot(ltri, g_all[r], precision=HIGHEST, preferred_element_type=F32) for r in rows]
    cgc = [per_head(lambda h, x=x: jnp.broadcast_to(x[:, h:h + 1], (c, hd))) for x in gcum]
    gcol = [per_head(lambda h, x=x: jnp.broadcast_to(x[:, h:h + 1], (c, c))) for x in gcum]
    grow = [jnp.concatenate([xt[h:h + 1, :] for h in range(nh)], axis=1)
            for xt in [x.T for x in gcum]]
    every = range(nc_all)
    decay = [jnp.exp(jnp.where(incl, gcol[n] - grow[n], -jnp.inf)) for n in every]
    eg = [jnp.exp(x) for x in cgc]
    g_last = [x[c - 1:c, :] for x in cgc]
    kk = [_mm_nt(jnp.concatenate([kb_all[r], q_all[r]], axis=0), bdiag(k_all[r], bd_cd)) for r in rows]
    attn = [kk[n][c:2 * c] * decay[n] for n in every]
    rj = [-jnp.where(strict, kk[n][0:c] * decay[n], 0.0) for n in every]
    sj = [eye + rj[n] for n in every]
    rj = [_mm(rj[n], bdiag(rj[n], bd_cc)) for n in every]
    for _ in range(4):
        rs = [_mm(jnp.concatenate([rj[n], sj[n]], axis=0), bdiag(rj[n], bd_cc)) for n in every]
        rj = [x[0:c] for x in rs]
        sj = [sj[n] + rs[n][c:2 * c] for n in every]
    sj = [sj[n] + _mm(sj[n], bdiag(rj[n], bd_cc)) for n in every]
    w = [_mm(sj[n], jnp.concatenate([bdiag(vb_all[rows[n]], bd_cd),
                                     bdiag(kb_all[rows[n]] * eg[n], bd_cd)], axis=1)) for n in every]
    q_dec = [q_all[rows[n]] * eg[n] for n in every]
    k_dec = [k_all[rows[n]] * jnp.exp(g_last[n] - cgc[n]) for n in every]
    s_dec = [jnp.exp(x) for x in g_last]

    s_cur = [[s_s[s, h] for h in range(nh)] for s in range(ns)]
    hs = [slice(h * hd, (h + 1) * hd) for h in range(nh)]
    for m in range(nc):
        ws = [[_mm(jnp.concatenate([w[s * nc + m][:, wz + h * hd:wz + (h + 1) * hd],
                                    q_dec[s * nc + m][:, hs[h]]], axis=0), s_cur[s][h]) for h in range(nh)]
              for s in range(ns)]
        v_new = [jnp.concatenate([w[s * nc + m][:, hs[h]] - ws[s][h][0:c] for h in range(nh)], axis=-1)
                 for s in range(ns)]
        o = [jnp.concatenate([x[c:2 * c] for x in ws[s]], axis=-1) + _mm(attn[s * nc + m], bdiag(v_new[s], bd_cd))
             for s in range(ns)]
        s_cur = [[s_cur[s][h] * s_dec[s * nc + m][:, hs[h]]
                  + _mm_tn(k_dec[s * nc + m][:, hs[h]], v_new[s][:, hs[h]]) for h in range(nh)]
                 for s in range(ns)]
        for s in range(ns):
            for h in range(nh):
                z = gd_ref[s, m * c:(m + 1) * c, w3 + h * hd:w3 + (h + 1) * hd]
                y_ref[s, m * c:(m + 1) * c, hs[h]] = _rms(o[s][:, hs[h]], nw_ref[...]) * (z * _sigmoid(z))
    for s in range(ns):
        for h in range(nh):
            s_s[s, h] = s_cur[s][h]

    @pl.when(i == last)
    def _fin():
        for s in range(ns):
            for h in range(nh):
                s_out[s, h] = s_cur[s][h]
            cv_out[s] = xp_s[s, SUBLANES - 3:SUBLANES, :]


def _gdn_prompt(gd3, p, layer, *, rg, ns):
    nb, length, n_gd = gd3.shape
    nh = p["gdn_nh"]
    hd = p["gdn_nw"].shape[-1]
    w3 = 3 * nh * hd
    consts = [p["gdn_cw"], p["gdn_alog"], p["gdn_dtb"], p["gdn_nw"]]
    return pl.pallas_call(
        functools.partial(_gdn_prompt_kernel, ns=ns, rg=rg, nh=nh, hd=hd),
        grid=(nb // ns, length // rg),
        in_specs=[pl.BlockSpec((ns, rg, n_gd), lambda b, i: (b, i, 0))]
        + [_layer_spec(a, layer) for a in consts],
        out_specs=[pl.BlockSpec((ns, rg, nh * hd), lambda b, i: (b, i, 0)),
                   pl.BlockSpec((ns, nh, hd, hd), lambda b, i: (b, 0, 0, 0)),
                   pl.BlockSpec((ns, 3, w3), lambda b, i: (b, 0, 0))],
        out_shape=[jax.ShapeDtypeStruct((nb, length, nh * hd), F32),
                   jax.ShapeDtypeStruct((nb, nh, hd, hd), F32),
                   jax.ShapeDtypeStruct((nb, 3, w3), F32)],
        scratch_shapes=[pltpu.VMEM((ns, nh, hd, hd), F32), pltpu.VMEM((ns, SUBLANES, w3), F32)],
        compiler_params=pltpu.CompilerParams(dimension_semantics=("arbitrary", "arbitrary"),
                                             vmem_limit_bytes=VMEM_LIMIT),
        name="gdn_prompt",
    )(gd3, *consts)


def _gdn_sample_kernel(*refs, nt, nbk, nh, hd, layer, has_acc):
    gd_ref, cv0_ref, s0_ref, cw_ref, alog_ref, dtb_ref, nw_ref = refs[:7]
    y_ref, s_out, cv_out, wk_s, kd_s, wv_s, ws_s = refs[7 + int(has_acc):]
    if not has_acc:
        for other in range(s_out.shape[0]):
            if other != layer:
                s_out[other] = jnp.zeros(s_out.shape[1:], F32)
        s_out = s_out.at[layer]
    w3 = 3 * nh * hd
    wz = nh * hd
    cw = cw_ref[...]
    scale = hd ** -0.5
    xs = [cv0_ref[j] for j in range(3)] + [gd_ref[t, :, 0:w3] for t in range(nt)]
    for j in range(3):
        cv_out[j] = xs[nt + j]
    q, k, v, gc, beta = [], [], [], [], []
    for t in range(nt):
        xc = xs[t] * cw[0:1] + xs[t + 1] * cw[1:2] + xs[t + 2] * cw[2:3] + xs[t + 3] * cw[3:4]
        xc = xc * _sigmoid(xc)
        ab = gd_ref[t, :, w3 + wz:w3 + wz + LANES]
        g_t = -jnp.exp(alog_ref[...]) * _softplus(ab + dtb_ref[...])
        gc.append(g_t if t == 0 else gc[-1] + g_t)
        beta.append(_sigmoid(ab))
        qn = _l2n_heads(xc[:, 0:wz], nh, hd, scale)
        kn = _l2n_heads(xc[:, wz:2 * wz], nh, hd, 1.0)
        q.append([qn[:, h * hd:(h + 1) * hd] for h in range(nh)])
        k.append([kn[:, h * hd:(h + 1) * hd] for h in range(nh)])
        v.append([xc[:, 2 * wz + h * hd:2 * wz + (h + 1) * hd] for h in range(nh)])

    def dot(a, b):
        return jnp.sum(a * b, axis=-1, keepdims=True)

    wk_rows = [[None] * nh for _ in range(nt)]
    qd_rows = [[None] * nh for _ in range(nt)]
    kd_rows = [[None] * nh for _ in range(nt)]
    wv_rows = [[None] * nh for _ in range(nt)]
    sd_rows = [None] * nh
    attn = [None] * nh
    for h in range(nh):
        gch = [gc[t][:, h:h + 1] for t in range(nt)]
        bh = [beta[t][:, nh + h:nh + h + 1] for t in range(nt)]
        e = [jnp.exp(x) for x in gch]
        kb = [k[t][h] * bh[t] for t in range(nt)]
        vb = [v[t][h] * bh[t] for t in range(nt)]
        kbe = [kb[t] * e[t] for t in range(nt)]
        dec = [[jnp.exp(gch[i] - gch[j]) for j in range(i)] for i in range(nt)]
        m = [[dot(kb[i], k[j][h]) * dec[i][j] for j in range(i)] for i in range(nt)]
        attn[h] = [[dot(q[i][h], k[j][h]) * (dec[i][j] if j < i else 1.0) for j in range(i + 1)]
                   for i in range(nt)]
        tinv = [[None] * nt for _ in range(nt)]
        for j in range(nt):
            for i in range(j + 1, nt):
                acc = m[i][j]
                for l in range(j + 1, i):
                    acc = acc + m[i][l] * tinv[l][j]
                tinv[i][j] = -acc
        for i in range(nt):
            wv_i, wk_i = vb[i], kbe[i]
            for j in range(i):
                wv_i = wv_i + tinv[i][j] * vb[j]
                wk_i = wk_i + tinv[i][j] * kbe[j]
            wv_rows[i][h] = wv_i
            wk_rows[i][h] = wk_i
            qd_rows[i][h] = q[i][h] * e[i]
            kd_rows[i][h] = k[i][h] * jnp.exp(gch[nt - 1] - gch[i])
        sd_rows[h] = jnp.broadcast_to(e[nt - 1], (nbk, hd))

    cat = lambda parts: jnp.concatenate(parts, axis=-1)
    to_seq = lambda slabs: jnp.swapaxes(jnp.stack(slabs, axis=0), 0, 1)
    zero = jnp.zeros((nbk, wz), F32)
    wk_s[...] = to_seq([cat(wk_rows[i]) for i in range(nt)] + [cat(qd_rows[i]) for i in range(nt)])
    kd_s[...] = to_seq([cat(kd_rows[i]) for i in range(nt)] + [zero] * nt)
    wv_s[...] = to_seq([cat(wv_rows[i]) for i in range(nt)] + [cat(sd_rows)] * nt)

    group = min(DECODE_SEQ_UNROLL, nbk)

    def per_group(gi, _):
        pairs = [(gi * group + u, h) for u in range(group) for h in range(nh)]
        tiles = {}
        for u in range(group):
            b = gi * group + u
            tiles[u] = (wk_s[b], kd_s[b], wv_s[b])
        cols = [slice(h * hd, (h + 1) * hd) for h in range(nh)]
        s_old = [s0_ref[b, h] for b, h in pairs]
        r = [_mm(tiles[n // nh][0][:, cols[h]], s_old[n]) for n, (b, h) in enumerate(pairs)]
        for n, (b, h) in enumerate(pairs):
            tkd, twv = tiles[n // nh][1][:, cols[h]], tiles[n // nh][2][:, cols[h]]
            s_out[b, h] = s_old[n] * twv[nt:nt + 1, :] + _mm_tn(tkd, twv - r[n])
            ws_s[b, :, cols[h]] = r[n]
        return 0

    lax.fori_loop(0, nbk // group, per_group, 0)

    ws_tm = jnp.swapaxes(ws_s[...], 0, 1)
    for h in range(nh):
        cols = slice(h * hd, (h + 1) * hd)
        v_new = [wv_rows[j][h] - ws_tm[j][:, cols] for j in range(nt)]
        for i in range(nt):
            o = ws_tm[nt + i][:, cols]
            for j in range(i + 1):
                o = o + attn[h][i][j] * v_new[j]
            z = gd_ref[i, :, w3 + h * hd:w3 + (h + 1) * hd]
            y_ref[i, :, cols] = _rms(o, nw_ref[...]) * (z * _sigmoid(z))


def _gdn_sample(gd3, cv0, s_all, s_acc, layer, p, *, nbk):
    nt, nb, n_gd = gd3.shape
    nh = p["gdn_nh"]
    hd = p["gdn_nw"].shape[-1]
    w3 = 3 * nh * hd
    wz = nh * hd
    s_spec = pl.BlockSpec((None, nbk, nh, hd, hd), lambda i: (layer, i, 0, 0, 0))
    consts = [p["gdn_cw"], p["gdn_alog"], p["gdn_dtb"], p["gdn_nw"]]
    args = [gd3, cv0, s_all] + consts
    in_specs = [pl.BlockSpec((nt, nbk, n_gd), lambda i: (0, i, 0)),
                pl.BlockSpec((3, nbk, w3), lambda i: (0, i, 0)),
                s_spec] + [_layer_spec(a, layer) for a in consts]
    aliases = {}
    s_out_spec = s_spec
    if s_acc is not None:
        aliases = {len(args): 1}
        args.append(s_acc)
        in_specs.append(pl.BlockSpec(memory_space=pl.ANY))
    else:
        s_out_spec = pl.BlockSpec((s_all.shape[0], nbk, nh, hd, hd), lambda i: (0, i, 0, 0, 0))
    return pl.pallas_call(
        functools.partial(_gdn_sample_kernel, nt=nt, nbk=nbk, nh=nh, hd=hd, layer=layer,
                          has_acc=s_acc is not None),
        grid=(nb // nbk,),
        in_specs=in_specs,
        out_specs=[pl.BlockSpec((nt, nbk, wz), lambda i: (0, i, 0)),
                   s_out_spec,
                   pl.BlockSpec((3, nbk, w3), lambda i: (0, i, 0))],
        out_shape=[jax.ShapeDtypeStruct((nt, nb, wz), F32),
                   jax.ShapeDtypeStruct(s_all.shape, F32),
                   jax.ShapeDtypeStruct((3, nb, w3), F32)],
        scratch_shapes=[pltpu.VMEM((nbk, 2 * nt, wz), F32) for _ in range(4)],
        input_output_aliases=aliases,
        compiler_params=pltpu.CompilerParams(dimension_semantics=("arbitrary",),
                                             vmem_limit_bytes=VMEM_LIMIT),
        name="gdn_sample",
    )(*args)


def _ffn_kernel(x_ref, ysl_ref, yg_ref, wo_ref, nmp_ref, nfp_ref, nfq_ref, wgu_ref, wd_ref, o_ref,
                *, from_time_major):
    nb, tc, d = x_ref.shape
    rows = nb * tc
    x = x_ref[...].reshape(rows, d)
    ysl = ysl_ref[...]
    if from_time_major:
        ysl = jnp.swapaxes(ysl, 0, 1)
    ysl = ysl.reshape(rows, ysl.shape[-1])
    yg = yg_ref[...].reshape(rows, yg_ref.shape[-1])
    mix = jnp.concatenate([ysl, yg], axis=-1)
    n_groups = max(1, rows // FFN_GROUP_ROWS)
    groups = [slice(k * rows // n_groups, (k + 1) * rows // n_groups) for k in range(n_groups)]
    hid = wd_ref.shape[0]
    o1 = [_mm(mix[r], wo_ref[...]) for r in groups]
    x1 = [x[r] + _rms(o1[k], nmp_ref[...]) for k, r in enumerate(groups)]
    gu = [_mm(_rms(x1[k], nfp_ref[...]), wgu_ref[...]) for k in range(n_groups)]
    f = [_mm(g[:, :hid] * _sigmoid(g[:, :hid]) * g[:, hid:], wd_ref[...]) for g in gu]
    out = jnp.concatenate([x1[k] + _rms(f[k], nfq_ref[...]) for k in range(n_groups)], axis=0)
    o_ref[...] = out.reshape(nb, tc, d)


def _ffn(x3, ysl3, yg3, p, layer, *, tc, from_time_major):
    nb, length, d = x3.shape
    n1 = ysl3.shape[-1]
    n2 = yg3.shape[-1]
    consts = [p["w_out"], p["n_mix_post"], p["n_ffn_pre"], p["n_ffn_post"], p["w_gu"], p["w_down"]]
    if from_time_major:
        ysl_spec = pl.BlockSpec((tc, nb, n1), lambda i: (i, 0, 0))
    else:
        ysl_spec = pl.BlockSpec((nb, tc, n1), lambda i: (0, i, 0))
    return pl.pallas_call(
        functools.partial(_ffn_kernel, from_time_major=from_time_major),
        grid=(length // tc,),
        in_specs=[pl.BlockSpec((nb, tc, d), lambda i: (0, i, 0)), ysl_spec,
                  pl.BlockSpec((nb, tc, n2), lambda i: (0, i, 0))]
        + [_layer_spec(a, layer) for a in consts],
        out_specs=pl.BlockSpec((nb, tc, d), lambda i: (0, i, 0)),
        out_shape=jax.ShapeDtypeStruct((nb, length, d), F32),
        compiler_params=pltpu.CompilerParams(dimension_semantics=("arbitrary",),
                                             vmem_limit_bytes=FFN_VMEM_LIMIT),
        name="ffn",
    )(x3, ysl3, yg3, *consts)


def _block_diag(blocks):
    dep, n, r, c = blocks.shape
    eye = jnp.eye(n, dtype=blocks.dtype)
    return (eye[None, :, None, :, None] * blocks[:, :, :, None, :]).reshape(dep, n * r, n * c)


def _params(norm_mix_pre, norm_mix_post, norm_ffn_pre, norm_ffn_post, w_in, s5_lambda_re,
            s5_lambda_im, s5_log_dt, s5_b_re, s5_b_im, s5_c_re, s5_c_im, s5_d, s5_w_glu, s5_b_glu,
            lru_conv_w, lru_conv_b, lru_w_a, lru_b_a, lru_w_x, lru_b_x, lru_lambda, gdn_conv_w,
            gdn_a_log, gdn_dt_bias, gdn_norm_w, w_out, ffn_w_gate_up, ffn_w_down):
    dep, g, pst, hgrp = s5_b_re.shape
    s5w = g * hgrp
    lw = lru_lambda.shape[1]
    nh = gdn_a_log.shape[1]
    row = lambda v: v[:, None, :]
    lanes = lambda v: jnp.pad(v, ((0, 0), (0, LANES - v.shape[1])))[:, None, :]
    return {
        "n_sl": s5w + 2 * lw,
        "n_mix_pre": row(norm_mix_pre), "n_mix_post": row(norm_mix_post),
        "n_ffn_pre": row(norm_ffn_pre), "n_ffn_post": row(norm_ffn_post),
        "w_in": w_in.astype(BF16),
        "s5_lre": s5_lambda_re.reshape(dep, 1, g * pst), "s5_lim": s5_lambda_im.reshape(dep, 1, g * pst),
        "s5_ldt": jnp.repeat(s5_log_dt, pst, axis=1)[:, None, :],
        "s5_bre": _block_diag(jnp.swapaxes(s5_b_re, 2, 3)),
        "s5_bim": _block_diag(jnp.swapaxes(s5_b_im, 2, 3)),
        "s5_cre": _block_diag(jnp.swapaxes(s5_c_re, 2, 3)),
        "s5_cim": _block_diag(jnp.swapaxes(s5_c_im, 2, 3)),
        "s5_d": row(s5_d), "s5_wglu": s5_w_glu.astype(BF16), "s5_bglu": row(s5_b_glu),
        "lru_cw": lru_conv_w, "lru_cb": row(lru_conv_b),
        "lru_wa": _block_diag(lru_w_a).astype(BF16), "lru_ba": row(lru_b_a),
        "lru_wx": _block_diag(lru_w_x).astype(BF16), "lru_bx": row(lru_b_x),
        "lru_lambda": row(lru_lambda),
        "gdn_cw": gdn_conv_w, "gdn_alog": lanes(gdn_a_log),
        "gdn_dtb": lanes(gdn_dt_bias), "gdn_nw": row(gdn_norm_w), "gdn_nh": nh,
        "w_out": w_out.astype(BF16), "w_gu": ffn_w_gate_up.astype(BF16),
        "w_down": ffn_w_down.astype(BF16),
    }


def _prompt_layer(x, p, layer, *, tc, rg):
    nb, length, _ = x.shape
    sl_tm, gd = _in_proj(x, p, layer, tc=min(IN_PROJ_TOKENS_PER_SEQ_STEP, length), to_time_major=True)
    lw = p["lru_lambda"].shape[-1]
    ysl, xr, xi, hl, cv = _scan(sl_tm.reshape(length * nb, -1), p, layer, None, nb=nb, tc=tc,
                                first_pos_is_zero=True)
    yg, s_new, gcv = _gdn_prompt(gd, p, layer, rg=rg, ns=GDN_SEQS_PER_STEP if nb % GDN_SEQS_PER_STEP == 0 else 1)
    x = _ffn(x, ysl.reshape(length, nb, -1), yg, p, layer, tc=min(FFN_TOKENS_PER_SEQ_STEP, length),
             from_time_major=True)
    lcv = jnp.swapaxes(cv.reshape(3, nb, lw), 0, 1)
    return x, (xr, xi, hl, lcv, s_new, gcv)


def _sample_layer(x_tm, st, gdn_s_all, gdn_s_acc, layer, p, *, nbk):
    s5_re, s5_im, lru_h, lru_conv, gdn_conv = st
    nb = s5_re.shape[0]
    rows = x_tm.shape[1]
    nt = rows // nb
    sl, gd = _in_proj(x_tm, p, layer, tc=rows, to_time_major=False)
    lw = p["lru_lambda"].shape[-1]
    cv0 = jnp.swapaxes(lru_conv, 0, 1).reshape(3 * nb, lw)
    states = [s5_re.reshape(nb, -1), s5_im.reshape(nb, -1), lru_h, cv0]
    ysl, xr, xi, hl, cv = _scan(sl.reshape(rows, -1), p, layer, states, nb=nb, tc=nt,
                                first_pos_is_zero=(PAST_LEN == 0))
    yg, s_acc, gcv = _gdn_sample(gd.reshape(nt, nb, -1), jnp.swapaxes(gdn_conv, 0, 1), gdn_s_all, gdn_s_acc,
                                 layer, p, nbk=nbk)
    x_tm = _ffn(x_tm, ysl.reshape(1, rows, -1), yg.reshape(1, rows, -1), p, layer, tc=rows,
                from_time_major=False)
    lcv = jnp.swapaxes(cv.reshape(3, nb, lw), 0, 1)
    return x_tm, (xr, xi, hl, lcv, jnp.swapaxes(gcv, 0, 1)), s_acc


def kernel(x_prompt, x_sample, state_s5_re, state_s5_im, state_lru_h, state_lru_conv, state_gdn_S, state_gdn_conv, norm_mix_pre, norm_mix_post, norm_ffn_pre, norm_ffn_post, w_in, s5_lambda_re, s5_lambda_im, s5_log_dt, s5_b_re, s5_b_im, s5_c_re, s5_c_im, s5_d, s5_w_glu, s5_b_glu, lru_conv_w, lru_conv_b, lru_w_a, lru_b_a, lru_w_x, lru_b_x, lru_lambda, gdn_conv_w, gdn_a_log, gdn_dt_bias, gdn_norm_w, w_out, ffn_w_gate_up, ffn_w_down):
    depth = w_in.shape[0]
    bp, lp, d = x_prompt.shape
    bs, ls, _ = x_sample.shape
    g, pst = s5_lambda_re.shape[1:]
    tc = min(TOKENS_PER_SEQ_STEP, lp)
    rg = min(GDN_ROWS_PER_STEP, lp)
    y_p = x_prompt
    y_s = jnp.swapaxes(x_sample, 0, 1).reshape(1, ls * bs, d)
    new_p = [[] for _ in range(6)]
    new_s = [[] for _ in range(5)]
    s_gdn_s = None
    p = _params(norm_mix_pre, norm_mix_post, norm_ffn_pre, norm_ffn_post, w_in, s5_lambda_re,
                s5_lambda_im, s5_log_dt, s5_b_re, s5_b_im, s5_c_re, s5_c_im, s5_d, s5_w_glu,
                s5_b_glu, lru_conv_w, lru_conv_b, lru_w_a, lru_b_a, lru_w_x, lru_b_x, lru_lambda,
                gdn_conv_w, gdn_a_log, gdn_dt_bias, gdn_norm_w, w_out, ffn_w_gate_up, ffn_w_down)
    for l in range(depth):
        y_p, sp = _prompt_layer(y_p, p, l, tc=tc, rg=rg)
        st_s = (state_s5_re[l], state_s5_im[l], state_lru_h[l], state_lru_conv[l], state_gdn_conv[l])
        y_s, ss, s_gdn_s = _sample_layer(y_s, st_s, state_gdn_S, s_gdn_s, l, p,
                                         nbk=min(DECODE_SEQS_PER_STEP, bs))
        for j in range(6):
            new_p[j].append(sp[j])
        for j in range(5):
            new_s[j].append(ss[j])
    y_s = jnp.swapaxes(y_s.reshape(ls, bs, d), 0, 1)
    outs_p = [jnp.stack(t, axis=0) for t in new_p]
    outs_s = [jnp.stack(t, axis=0) for t in new_s]
    outs_s.insert(4, s_gdn_s)
    outs_p[0] = outs_p[0].reshape(depth, bp, g, pst)
    outs_p[1] = outs_p[1].reshape(depth, bp, g, pst)
    outs_s[0] = outs_s[0].reshape(depth, bs, g, pst)
    outs_s[1] = outs_s[1].reshape(depth, bs, g, pst)
    return (y_p, y_s, *outs_p, *outs_s)
```

```python
import functools

import jax
import jax.numpy as jnp
from jax import lax
from jax.experimental import pallas as pl
from jax.experimental.pallas import tpu as pltpu

F32 = jnp.float32
BF16 = jnp.bfloat16
HIGHEST = lax.Precision.HIGHEST

NORM_EPS = 1e-6
LRU_C = 8.0
GDN_CHUNK = 64
PAST_LEN = 16384
LANES = 128
SUBLANES = 8
VMEM_LIMIT = 56 * 1024 * 1024
FFN_VMEM_LIMIT = 60 * 1024 * 1024

TOKENS_PER_SEQ_STEP = 64
IN_PROJ_TOKENS_PER_SEQ_STEP = 128
FFN_TOKENS_PER_SEQ_STEP = 128
FFN_GROUP_ROWS = 256
GDN_ROWS_PER_STEP = 2 * GDN_CHUNK
GDN_SEQS_PER_STEP = 8
DECODE_SEQS_PER_STEP = 16
DECODE_SEQ_UNROLL = 8
ROW_GROUPS = 2


def _rms(x, w):
    return x * lax.rsqrt(jnp.mean(x * x, axis=-1, keepdims=True) + NORM_EPS) * w


def _sigmoid(x):
    return jax.nn.sigmoid(x)


def _softplus(x):
    return jnp.maximum(x, 0.0) + jnp.log1p(jnp.exp(-jnp.abs(x)))


def _mm(a, b):
    return jnp.dot(a.astype(BF16), b.astype(BF16), preferred_element_type=F32)


def _mm_nt(a, b):
    return lax.dot_general(a.astype(BF16), b.astype(BF16), (((1,), (1,)), ((), ())),
                           preferred_element_type=F32)


def _mm_tn(a, b):
    return lax.dot_general(a.astype(BF16), b.astype(BF16), (((0,), (0,)), ((), ())),
                           preferred_element_type=F32)


def _const_spec(shape):
    n = len(shape)
    return pl.BlockSpec(shape, lambda *_: (0,) * n, pipeline_mode=pl.Buffered(1))


def _layer_spec(arr, layer):
    n = arr.ndim - 1
    return pl.BlockSpec((None,) + arr.shape[1:], lambda *_: (layer,) + (0,) * n,
                        pipeline_mode=pl.Buffered(1))


def _l2n_heads(x, nh, hd, mul):
    def one(h):
        xh = x[:, h * hd:(h + 1) * hd]
        return xh * (lax.rsqrt(jnp.sum(xh * xh, axis=-1, keepdims=True) + NORM_EPS) * mul)
    return jnp.concatenate([one(h) for h in range(nh)], axis=-1)


def _in_proj_kernel(x_ref, nw_ref, w_ref, sl_ref, gd_ref, *, n_sl, w3, n_ab, to_time_major):
    nb, tc, d = x_ref.shape
    rows = nb * tc
    x = x_ref[...].reshape(rows, d)
    groups = [slice(k * rows // ROW_GROUPS, (k + 1) * rows // ROW_GROUPS) for k in range(ROW_GROUPS)]
    p = jnp.concatenate([_mm(_rms(x[r], nw_ref[...]), w_ref[...]) for r in groups], axis=0)
    sl = p[:, :n_sl].reshape(nb, tc, n_sl)
    if to_time_major:
        sl = jnp.swapaxes(sl, 0, 1)
    sl_ref[...] = sl
    c_ab = n_sl + w3
    c_z = c_ab + n_ab
    wz = p.shape[1] - c_z
    gd_ref[:, :, 0:w3] = p[:, n_sl:c_ab].reshape(nb, tc, w3)
    gd_ref[:, :, w3:w3 + wz] = p[:, c_z:].reshape(nb, tc, wz)
    ab = jnp.concatenate([p[:, c_ab:c_z], jnp.zeros((rows, LANES - n_ab), F32)], axis=1)
    gd_ref[:, :, w3 + wz:] = ab.reshape(nb, tc, LANES)


def _in_proj(x3, p, layer, *, tc, to_time_major):
    nb, length, d = x3.shape
    nw, w, n_sl = p["n_mix_pre"], p["w_in"], p["n_sl"]
    n_tot = w.shape[-1]
    nh = p["gdn_nh"]
    w3 = 3 * nh * p["gdn_nw"].shape[-1]
    n_gd = n_tot - n_sl - 2 * nh + LANES
    if to_time_major:
        sl_shape, sl_spec = (length, nb, n_sl), pl.BlockSpec((tc, nb, n_sl), lambda i: (i, 0, 0))
    else:
        sl_shape, sl_spec = (nb, length, n_sl), pl.BlockSpec((nb, tc, n_sl), lambda i: (0, i, 0))
    return pl.pallas_call(
        functools.partial(_in_proj_kernel, n_sl=n_sl, w3=w3, n_ab=2 * nh, to_time_major=to_time_major),
        grid=(length // tc,),
        in_specs=[pl.BlockSpec((nb, tc, d), lambda i: (0, i, 0)),
                  _layer_spec(nw, layer), _layer_spec(w, layer)],
        out_specs=[sl_spec, pl.BlockSpec((nb, tc, n_gd), lambda i: (0, i, 0))],
        out_shape=[jax.ShapeDtypeStruct(sl_shape, F32), jax.ShapeDtypeStruct((nb, length, n_gd), F32)],
        compiler_params=pltpu.CompilerParams(dimension_semantics=("arbitrary",),
                                             vmem_limit_bytes=VMEM_LIMIT),
        name="in_proj",
    )(x3, nw, w)


def _scan_kernel(*refs, nb, tc, s5w, lw, first_pos_is_zero, zero_state):
    (sl_ref, lre_ref, lim_ref, ldt_ref, bre_ref, bim_ref, cre_ref, cim_ref, d_ref,
     wglu_ref, bglu_ref, cw_ref, cb_ref, wa_ref, ba_ref, wx_ref, bx_ref, lam_ref) = refs[:18]
    n_in = 18 if zero_state else 22
    (y_ref, xr_out, xi_out, h_out, cv_out,
     ar_s, ai_s, bbr_s, bbi_s, crb_s, cib_s, xr_c, xi_c, h_c, xr_s, xi_s, xpad_s, a_s, b_s) = refs[n_in:]
    i = pl.program_id(0)
    last = pl.num_programs(0) - 1
    rows = nb * tc

    @pl.when(i == 0)
    def _init():
        lr = jnp.minimum(lre_ref[...], -1e-4)
        li = lim_ref[...]
        dt = jnp.exp(ldt_ref[...])
        mag = jnp.exp(lr * dt)
        ar = mag * jnp.cos(li * dt)
        ai = mag * jnp.sin(li * dt)
        den = lr * lr + li * li
        fr = ((ar - 1.0) * lr + ai * li) / den
        fi = (ai * lr - (ar - 1.0) * li) / den
        ar_s[...] = ar
        ai_s[...] = ai
        bre = bre_ref[...]
        bim = bim_ref[...]
        bbr_s[...] = (fr * bre - fi * bim).astype(BF16)
        bbi_s[...] = (fr * bim + fi * bre).astype(BF16)
        crb_s[...] = cre_ref[...].astype(BF16)
        cib_s[...] = cim_ref[...].astype(BF16)
        if zero_state:
            xr_c[...] = jnp.zeros_like(xr_c)
            xi_c[...] = jnp.zeros_like(xi_c)
            h_c[...] = jnp.zeros_like(h_c)
            xpad_s[0:3 * nb, :] = jnp.zeros((3 * nb, lw), F32)
        else:
            xr0_ref, xi0_ref, h0_ref, cv0_ref = refs[18:22]
            xr_c[...] = xr0_ref[...]
            xi_c[...] = xi0_ref[...]
            h_c[...] = h0_ref[...]
            xpad_s[0:3 * nb, :] = cv0_ref[...]

    u = sl_ref[:, 0:s5w]
    xb = sl_ref[:, s5w:s5w + lw]
    gate = sl_ref[:, s5w + lw:s5w + 2 * lw]

    ub = u.astype(BF16)
    xr_s[...] = jnp.dot(ub, bbr_s[...], preferred_element_type=F32)
    xi_s[...] = jnp.dot(ub, bbi_s[...], preferred_element_type=F32)

    xpad_s[3 * nb:3 * nb + rows, :] = xb
    cw = cw_ref[...]
    xc = (xpad_s[0:rows, :] * cw[0:1] + xpad_s[nb:nb + rows, :] * cw[1:2]
          + xpad_s[2 * nb:2 * nb + rows, :] * cw[2:3] + xb * cw[3:4]) + cb_ref[...]
    new_cv = xpad_s[rows:rows + 3 * nb, :]
    xpad_s[0:3 * nb, :] = new_cv
    r = _sigmoid(_mm(xc, wa_ref[...]) + ba_ref[...])
    gi = _sigmoid(_mm(xc, wx_ref[...]) + bx_ref[...])
    log_a = -LRU_C * r * _softplus(-lam_ref[...])
    a = jnp.exp(log_a)
    m2 = -jnp.tanh(log_a) * (a * a + 1.0)
    mult = jnp.where(m2 > 0.0, m2 * lax.rsqrt(m2), 0.0)
    if first_pos_is_zero:
        rid = lax.broadcasted_iota(jnp.int32, (rows, 1), 0)
        mult = jnp.where(jnp.logical_and(rid < nb, i == 0), 1.0, mult)
    a_s[...] = a
    b_s[...] = mult * gi * xc

    ar = jnp.broadcast_to(ar_s[...], (nb, ar_s.shape[1]))
    ai = jnp.broadcast_to(ai_s[...], (nb, ai_s.shape[1]))

    def step(t, carry):
        xr, xi, h = carry
        sl_t = pl.ds(pl.multiple_of(t * nb, nb), nb)
        nxr = ar * xr - ai * xi + xr_s[sl_t, :]
        nxi = ar * xi + ai * xr + xi_s[sl_t, :]
        nh = a_s[sl_t, :] * h + b_s[sl_t, :]
        xr_s[sl_t, :] = nxr
        xi_s[sl_t, :] = nxi
        b_s[sl_t, :] = nh
        return nxr, nxi, nh

    xr, xi, h = lax.fori_loop(0, tc, step, (xr_c[...], xi_c[...], h_c[...]),
                              unroll=(True if tc <= 8 else 8))
    xr_c[...] = xr
    xi_c[...] = xi
    h_c[...] = h

    y = (jnp.dot(xr_s[...].astype(BF16), crb_s[...], preferred_element_type=F32)
         - jnp.dot(xi_s[...].astype(BF16), cib_s[...], preferred_element_type=F32))
    y = jax.nn.gelu(y + d_ref[...] * u)
    y = y * _sigmoid(_mm(y, wglu_ref[...]) + bglu_ref[...])
    y_ref[:, 0:s5w] = y
    y_ref[:, s5w:s5w + lw] = b_s[...] * jax.nn.gelu(gate)

    @pl.when(i == last)
    def _fin():
        xr_out[...] = xr
        xi_out[...] = xi
        h_out[...] = h
        cv_out[...] = new_cv


def _scan(sl2, p, layer, states, *, nb, tc, first_pos_is_zero):
    rows_total, n_sl = sl2.shape
    s5w = p["s5_d"].shape[-1]
    lw = p["lru_lambda"].shape[-1]
    n_state = p["s5_lre"].shape[-1]
    rows = nb * tc
    params = [p["s5_lre"], p["s5_lim"], p["s5_ldt"], p["s5_bre"], p["s5_bim"], p["s5_cre"], p["s5_cim"],
              p["s5_d"], p["s5_wglu"], p["s5_bglu"], p["lru_cw"], p["lru_cb"], p["lru_wa"], p["lru_ba"],
              p["lru_wx"], p["lru_bx"], p["lru_lambda"]]
    zero_state = states is None
    states = [] if zero_state else list(states)
    return pl.pallas_call(
        functools.partial(_scan_kernel, nb=nb, tc=tc, s5w=s5w, lw=lw, first_pos_is_zero=first_pos_is_zero,
                          zero_state=zero_state),
        grid=(rows_total // rows,),
        in_specs=[pl.BlockSpec((rows, n_sl), lambda i: (i, 0))]
        + [_layer_spec(a, layer) for a in params] + [_const_spec(a.shape) for a in states],
        out_specs=[pl.BlockSpec((rows, s5w + lw), lambda i: (i, 0)),
                   pl.BlockSpec((nb, n_state), lambda i: (0, 0)),
                   pl.BlockSpec((nb, n_state), lambda i: (0, 0)),
                   pl.BlockSpec((nb, lw), lambda i: (0, 0)),
                   pl.BlockSpec((3 * nb, lw), lambda i: (0, 0))],
        out_shape=[jax.ShapeDtypeStruct((rows_total, s5w + lw), F32),
                   jax.ShapeDtypeStruct((nb, n_state), F32),
                   jax.ShapeDtypeStruct((nb, n_state), F32),
                   jax.ShapeDtypeStruct((nb, lw), F32),
                   jax.ShapeDtypeStruct((3 * nb, lw), F32)],
        scratch_shapes=[pltpu.VMEM((1, n_state), F32), pltpu.VMEM((1, n_state), F32),
                        pltpu.VMEM((s5w, n_state), BF16), pltpu.VMEM((s5w, n_state), BF16),
                        pltpu.VMEM((n_state, s5w), BF16), pltpu.VMEM((n_state, s5w), BF16),
                        pltpu.VMEM((nb, n_state), F32), pltpu.VMEM((nb, n_state), F32),
                        pltpu.VMEM((nb, lw), F32),
                        pltpu.VMEM((rows, n_state), F32), pltpu.VMEM((rows, n_state), F32),
                        pltpu.VMEM((rows + 3 * nb, lw), F32),
                        pltpu.VMEM((rows, lw), F32), pltpu.VMEM((rows, lw), F32)],
        compiler_params=pltpu.CompilerParams(dimension_semantics=("arbitrary",),
                                             vmem_limit_bytes=VMEM_LIMIT),
        name="scan",
    )(sl2, *params, *states)


def _gdn_prompt_kernel(gd_ref, cw_ref, alog_ref, dtb_ref, nw_ref, y_ref, s_out, cv_out,
                       s_s, xp_s, *, ns, rg, nh, hd):
    i = pl.program_id(1)
    last = pl.num_programs(1) - 1
    w3 = 3 * nh * hd
    wz = nh * hd
    c = GDN_CHUNK
    nc = rg // c
    wc = nh * c
    rt = ns * rg

    @pl.when(i == 0)
    def _init():
        s_s[...] = jnp.zeros_like(s_s)
        xp_s[...] = jnp.zeros_like(xp_s)

    cw = cw_ref[...]
    xcs = []
    for s in range(ns):
        qkv = gd_ref[s, :, 0:w3]
        xe = jnp.concatenate([xp_s[s], qkv], axis=0)
        acc = qkv * cw[3:4]
        for j in range(1, 4):
            acc = acc + pltpu.roll(xe, j, 0)[SUBLANES:, :] * cw[3 - j:4 - j]
        xp_s[s] = qkv[rg - SUBLANES:, :]
        xcs.append(acc)
    xc = jnp.concatenate(xcs, axis=0)
    xc = xc * _sigmoid(xc)

    ab = jnp.concatenate([gd_ref[s, :, w3 + wz:w3 + wz + LANES] for s in range(ns)], axis=0)
    g_all = -jnp.exp(alog_ref[...]) * _softplus(ab + dtb_ref[...])
    beta_all = _sigmoid(ab)

    def per_head(fn):
        return jnp.concatenate([fn(h) for h in range(nh)], axis=-1)

    q_all = _l2n_heads(xc[:, 0:wz], nh, hd, hd ** -0.5)
    k_all = _l2n_heads(xc[:, wz:2 * wz], nh, hd, 1.0)
    v_all = xc[:, 2 * wz:3 * wz]
    beta_w = per_head(lambda h: jnp.broadcast_to(beta_all[:, nh + h:nh + h + 1], (rt, hd)))
    kb_all = k_all * beta_w
    vb_all = v_all * beta_w

    ri = lax.broadcasted_iota(jnp.int32, (c, wc), 0)
    cj = lax.broadcasted_iota(jnp.int32, (c, wc), 1) % c
    incl = ri >= cj
    strict = ri > cj
    eye = (ri == cj).astype(F32)
    r1 = lax.broadcasted_iota(jnp.int32, (c, c), 0)
    c1 = lax.broadcasted_iota(jnp.int32, (c, c), 1)
    ltri = (r1 >= c1).astype(F32)
    bd_cc = (lax.broadcasted_iota(jnp.int32, (wc, wc), 0) // c
             == lax.broadcasted_iota(jnp.int32, (wc, wc), 1) // c)
    bd_cd = (lax.broadcasted_iota(jnp.int32, (wc, wz), 0) // c
             == lax.broadcasted_iota(jnp.int32, (wc, wz), 1) // hd)

    def bdiag(x, mask):
        return jnp.where(mask, jnp.concatenate([x] * nh, axis=0), 0.0).astype(BF16)

    nc_all = ns * nc
    rows = [slice(cc * c, (cc + 1) * c) for cc in range(nc_all)]
    gcum = [jnp.dot(ltri, g_all[r], precision=HIGHEST, preferred_element_type=F32) for r in rows]
    cgc = [per_head(lambda h, x=x: jnp.broadcast_to(x[:, h:h + 1], (c, hd))) for x in gcum]
    gcol = [per_head(lambda h, x=x: jnp.broadcast_to(x[:, h:h + 1], (c, c))) for x in gcum]
    grow = [jnp.concatenate([xt[h:h + 1, :] for h in range(nh)], axis=1)
            for xt in [x.T for x in gcum]]
    every = range(nc_all)
    decay = [jnp.exp(jnp.where(incl, gcol[n] - grow[n], -jnp.inf)) for n in every]
    eg = [jnp.exp(x) for x in cgc]
    g_last = [x[c - 1:c, :] for x in cgc]
    kk = [_mm_nt(jnp.concatenate([kb_all[r], q_all[r]], axis=0), bdiag(k_all[r], bd_cd)) for r in rows]
    attn = [kk[n][c:2 * c] * decay[n] for n in every]
    rj = [-jnp.where(strict, kk[n][0:c] * decay[n], 0.0) for n in every]
    sj = [eye + rj[n] for n in every]
    rj = [_mm(rj[n], bdiag(rj[n], bd_cc)) for n in every]
    for _ in range(4):
        rs = [_mm(jnp.concatenate([rj[n], sj[n]], axis=0), bdiag(rj[n], bd_cc)) for n in every]
        rj = [x[0:c] for x in rs]
        sj = [sj[n] + rs[n][c:2 * c] for n in every]
    sj = [sj[n] + _mm(sj[n], bdiag(rj[n], bd_cc)) for n in every]
    w = [_mm(sj[n], jnp.concatenate([bdiag(vb_all[rows[n]], bd_cd),
                                     bdiag(kb_all[rows[n]] * eg[n], bd_cd)], axis=1)) for n in every]
    q_dec = [q_all[rows[n]] * eg[n] for n in every]
    k_dec = [k_all[rows[n]] * jnp.exp(g_last[n] - cgc[n]) for n in every]
    s_dec = [jnp.exp(x) for x in g_last]

    s_cur = [[s_s[s, h] for h in range(nh)] for s in range(ns)]
    hs = [slice(h * hd, (h + 1) * hd) for h in range(nh)]
    for m in range(nc):
        ws = [[_mm(jnp.concatenate([w[s * nc + m][:, wz + h * hd:wz + (h + 1) * hd],
                                    q_dec[s * nc + m][:, hs[h]]], axis=0), s_cur[s][h]) for h in range(nh)]
              for s in range(ns)]
        v_new = [jnp.concatenate([w[s * nc + m][:, hs[h]] - ws[s][h][0:c] for h in range(nh)], axis=-1)
                 for s in range(ns)]
        o = [jnp.concatenate([x[c:2 * c] for x in ws[s]], axis=-1) + _mm(attn[s * nc + m], bdiag(v_new[s], bd_cd))
             for s in range(ns)]
        s_cur = [[s_cur[s][h] * s_dec[s * nc + m][:, hs[h]]
                  + _mm_tn(k_dec[s * nc + m][:, hs[h]], v_new[s][:, hs[h]]) for h in range(nh)]
                 for s in range(ns)]
        for s in range(ns):
            for h in range(nh):
                z = gd_ref[s, m * c:(m + 1) * c, w3 + h * hd:w3 + (h + 1) * hd]
                y_ref[s, m * c:(m + 1) * c, hs[h]] = _rms(o[s][:, hs[h]], nw_ref[...]) * (z * _sigmoid(z))
    for s in range(ns):
        for h in range(nh):
            s_s[s, h] = s_cur[s][h]

    @pl.when(i == last)
    def _fin():
        for s in range(ns):
            for h in range(nh):
                s_out[s, h] = s_cur[s][h]
            cv_out[s] = xp_s[s, SUBLANES - 3:SUBLANES, :]


def _gdn_prompt(gd3, p, layer, *, rg, ns):
    nb, length, n_gd = gd3.shape
    nh = p["gdn_nh"]
    hd = p["gdn_nw"].shape[-1]
    w3 = 3 * nh * hd
    consts = [p["gdn_cw"], p["gdn_alog"], p["gdn_dtb"], p["gdn_nw"]]
    return pl.pallas_call(
        functools.partial(_gdn_prompt_kernel, ns=ns, rg=rg, nh=nh, hd=hd),
        grid=(nb // ns, length // rg),
        in_specs=[pl.BlockSpec((ns, rg, n_gd), lambda b, i: (b, i, 0))]
        + [_layer_spec(a, layer) for a in consts],
        out_specs=[pl.BlockSpec((ns, rg, nh * hd), lambda b, i: (b, i, 0)),
                   pl.BlockSpec((ns, nh, hd, hd), lambda b, i: (b, 0, 0, 0)),
                   pl.BlockSpec((ns, 3, w3), lambda b, i: (b, 0, 0))],
        out_shape=[jax.ShapeDtypeStruct((nb, length, nh * hd), F32),
                   jax.ShapeDtypeStruct((nb, nh, hd, hd), F32),
                   jax.ShapeDtypeStruct((nb, 3, w3), F32)],
        scratch_shapes=[pltpu.VMEM((ns, nh, hd, hd), F32), pltpu.VMEM((ns, SUBLANES, w3), F32)],
        compiler_params=pltpu.CompilerParams(dimension_semantics=("arbitrary", "arbitrary"),
                                             vmem_limit_bytes=VMEM_LIMIT),
        name="gdn_prompt",
    )(gd3, *consts)


def _gdn_sample_kernel(*refs, nt, nbk, nh, hd, layer, has_acc):
    gd_ref, cv0_ref, s0_ref, cw_ref, alog_ref, dtb_ref, nw_ref = refs[:7]
    y_ref, s_out, cv_out, wk_s, kd_s, wv_s, ws_s = refs[7 + int(has_acc):]
    if not has_acc:
        for other in range(s_out.shape[0]):
            if other != layer:
                s_out[other] = jnp.zeros(s_out.shape[1:], F32)
        s_out = s_out.at[layer]
    w3 = 3 * nh * hd
    wz = nh * hd
    cw = cw_ref[...]
    scale = hd ** -0.5
    xs = [cv0_ref[j] for j in range(3)] + [gd_ref[t, :, 0:w3] for t in range(nt)]
    for j in range(3):
        cv_out[j] = xs[nt + j]
    q, k, v, gc, beta = [], [], [], [], []
    for t in range(nt):
        xc = xs[t] * cw[0:1] + xs[t + 1] * cw[1:2] + xs[t + 2] * cw[2:3] + xs[t + 3] * cw[3:4]
        xc = xc * _sigmoid(xc)
        ab = gd_ref[t, :, w3 + wz:w3 + wz + LANES]
        g_t = -jnp.exp(alog_ref[...]) * _softplus(ab + dtb_ref[...])
        gc.append(g_t if t == 0 else gc[-1] + g_t)
        beta.append(_sigmoid(ab))
        qn = _l2n_heads(xc[:, 0:wz], nh, hd, scale)
        kn = _l2n_heads(xc[:, wz:2 * wz], nh, hd, 1.0)
        q.append([qn[:, h * hd:(h + 1) * hd] for h in range(nh)])
        k.append([kn[:, h * hd:(h + 1) * hd] for h in range(nh)])
        v.append([xc[:, 2 * wz + h * hd:2 * wz + (h + 1) * hd] for h in range(nh)])

    def dot(a, b):
        return jnp.sum(a * b, axis=-1, keepdims=True)

    wk_rows = [[None] * nh for _ in range(nt)]
    qd_rows = [[None] * nh for _ in range(nt)]
    kd_rows = [[None] * nh for _ in range(nt)]
    wv_rows = [[None] * nh for _ in range(nt)]
    sd_rows = [None] * nh
    attn = [None] * nh
    for h in range(nh):
        gch = [gc[t][:, h:h + 1] for t in range(nt)]
        bh = [beta[t][:, nh + h:nh + h + 1] for t in range(nt)]
        e = [jnp.exp(x) for x in gch]
        kb = [k[t][h] * bh[t] for t in range(nt)]
        vb = [v[t][h] * bh[t] for t in range(nt)]
        kbe = [kb[t] * e[t] for t in range(nt)]
        dec = [[jnp.exp(gch[i] - gch[j]) for j in range(i)] for i in range(nt)]
        m = [[dot(kb[i], k[j][h]) * dec[i][j] for j in range(i)] for i in range(nt)]
        attn[h] = [[dot(q[i][h], k[j][h]) * (dec[i][j] if j < i else 1.0) for j in range(i + 1)]
                   for i in range(nt)]
        tinv = [[None] * nt for _ in range(nt)]
        for j in range(nt):
            for i in range(j + 1, nt):
                acc = m[i][j]
                for l in range(j + 1, i):
                    acc = acc + m[i][l] * tinv[l][j]
                tinv[i][j] = -acc
        for i in range(nt):
            wv_i, wk_i = vb[i], kbe[i]
            for j in range(i):
                wv_i = wv_i + tinv[i][j] * vb[j]
                wk_i = wk_i + tinv[i][j] * kbe[j]
            wv_rows[i][h] = wv_i
            wk_rows[i][h] = wk_i
            qd_rows[i][h] = q[i][h] * e[i]
            kd_rows[i][h] = k[i][h] * jnp.exp(gch[nt - 1] - gch[i])
        sd_rows[h] = jnp.broadcast_to(e[nt - 1], (nbk, hd))

    cat = lambda parts: jnp.concatenate(parts, axis=-1)
    to_seq = lambda slabs: jnp.swapaxes(jnp.stack(slabs, axis=0), 0, 1)
    zero = jnp.zeros((nbk, wz), F32)
    wk_s[...] = to_seq([cat(wk_rows[i]) for i in range(nt)] + [cat(qd_rows[i]) for i in range(nt)])
    kd_s[...] = to_seq([cat(kd_rows[i]) for i in range(nt)] + [zero] * nt)
    wv_s[...] = to_seq([cat(wv_rows[i]) for i in range(nt)] + [cat(sd_rows)] * nt)

    group = min(DECODE_SEQ_UNROLL, nbk)

    def per_group(gi, _):
        pairs = [(gi * group + u, h) for u in range(group) for h in range(nh)]
        tiles = {}
        for u in range(group):
            b = gi * group + u
            tiles[u] = (wk_s[b], kd_s[b], wv_s[b])
        cols = [slice(h * hd, (h + 1) * hd) for h in range(nh)]
        s_old = [s0_ref[b, h] for b, h in pairs]
        r = [_mm(tiles[n // nh][0][:, cols[h]], s_old[n]) for n, (b, h) in enumerate(pairs)]
        for n, (b, h) in enumerate(pairs):
            tkd, twv = tiles[n // nh][1][:, cols[h]], tiles[n // nh][2][:, cols[h]]
            s_out[b, h] = s_old[n] * twv[nt:nt + 1, :] + _mm_tn(tkd, twv - r[n])
            ws_s[b, :, cols[h]] = r[n]
        return 0

    lax.fori_loop(0, nbk // group, per_group, 0)

    ws_tm = jnp.swapaxes(ws_s[...], 0, 1)
    for h in range(nh):
        cols = slice(h * hd, (h + 1) * hd)
        v_new = [wv_rows[j][h] - ws_tm[j][:, cols] for j in range(nt)]
        for i in range(nt):
            o = ws_tm[nt + i][:, cols]
            for j in range(i + 1):
                o = o + attn[h][i][j] * v_new[j]
            z = gd_ref[i, :, w3 + h * hd:w3 + (h + 1) * hd]
            y_ref[i, :, cols] = _rms(o, nw_ref[...]) * (z * _sigmoid(z))


def _gdn_sample(gd3, cv0, s_all, s_acc, layer, p, *, nbk):
    nt, nb, n_gd = gd3.shape
    nh = p["gdn_nh"]
    hd = p["gdn_nw"].shape[-1]
    w3 = 3 * nh * hd
    wz = nh * hd
    s_spec = pl.BlockSpec((None, nbk, nh, hd, hd), lambda i: (layer, i, 0, 0, 0))
    consts = [p["gdn_cw"], p["gdn_alog"], p["gdn_dtb"], p["gdn_nw"]]
    args = [gd3, cv0, s_all] + consts
    in_specs = [pl.BlockSpec((nt, nbk, n_gd), lambda i: (0, i, 0)),
                pl.BlockSpec((3, nbk, w3), lambda i: (0, i, 0)),
                s_spec] + [_layer_spec(a, layer) for a in consts]
    aliases = {}
    s_out_spec = s_spec
    if s_acc is not None:
        aliases = {len(args): 1}
        args.append(s_acc)
        in_specs.append(pl.BlockSpec(memory_space=pl.ANY))
    else:
        s_out_spec = pl.BlockSpec((s_all.shape[0], nbk, nh, hd, hd), lambda i: (0, i, 0, 0, 0))
    return pl.pallas_call(
        functools.partial(_gdn_sample_kernel, nt=nt, nbk=nbk, nh=nh, hd=hd, layer=layer,
                          has_acc=s_acc is not None),
        grid=(nb // nbk,),
        in_specs=in_specs,
        out_specs=[pl.BlockSpec((nt, nbk, wz), lambda i: (0, i, 0)),
                   s_out_spec,
                   pl.BlockSpec((3, nbk, w3), lambda i: (0, i, 0))],
        out_shape=[jax.ShapeDtypeStruct((nt, nb, wz), F32),
                   jax.ShapeDtypeStruct(s_all.shape, F32),
                   jax.ShapeDtypeStruct((3, nb, w3), F32)],
        scratch_shapes=[pltpu.VMEM((nbk, 2 * nt, wz), F32) for _ in range(4)],
        input_output_aliases=aliases,
        compiler_params=pltpu.CompilerParams(dimension_semantics=("arbitrary",),
                                             vmem_limit_bytes=VMEM_LIMIT),
        name="gdn_sample",
    )(*args)


def _ffn_kernel(x_ref, ysl_ref, yg_ref, wo_ref, nmp_ref, nfp_ref, nfq_ref, wgu_ref, wd_ref, o_ref,
                *, from_time_major):
    nb, tc, d = x_ref.shape
    rows = nb * tc
    x = x_ref[...].reshape(rows, d)
    ysl = ysl_ref[...]
    if from_time_major:
        ysl = jnp.swapaxes(ysl, 0, 1)
    ysl = ysl.reshape(rows, ysl.shape[-1])
    yg = yg_ref[...].reshape(rows, yg_ref.shape[-1])
    mix = jnp.concatenate([ysl, yg], axis=-1)
    n_groups = max(1, rows // FFN_GROUP_ROWS)
    groups = [slice(k * rows // n_groups, (k + 1) * rows // n_groups) for k in range(n_groups)]
    hid = wd_ref.shape[0]
    o1 = [_mm(mix[r], wo_ref[...]) for r in groups]
    x1 = [x[r] + _rms(o1[k], nmp_ref[...]) for k, r in enumerate(groups)]
    gu = [_mm(_rms(x1[k], nfp_ref[...]), wgu_ref[...]) for k in range(n_groups)]
    f = [_mm(g[:, :hid] * _sigmoid(g[:, :hid]) * g[:, hid:], wd_ref[...]) for g in gu]
    out = jnp.concatenate([x1[k] + _rms(f[k], nfq_ref[...]) for k in range(n_groups)], axis=0)
    o_ref[...] = out.reshape(nb, tc, d)


def _ffn(x3, ysl3, yg3, p, layer, *, tc, from_time_major):
    nb, length, d = x3.shape
    n1 = ysl3.shape[-1]
    n2 = yg3.shape[-1]
    consts = [p["w_out"], p["n_mix_post"], p["n_ffn_pre"], p["n_ffn_post"], p["w_gu"], p["w_down"]]
    if from_time_major:
        ysl_spec = pl.BlockSpec((tc, nb, n1), lambda i: (i, 0, 0))
    else:
        ysl_spec = pl.BlockSpec((nb, tc, n1), lambda i: (0, i, 0))
    return pl.pallas_call(
        functools.partial(_ffn_kernel, from_time_major=from_time_major),
        grid=(length // tc,),
        in_specs=[pl.BlockSpec((nb, tc, d), lambda i: (0, i, 0)), ysl_spec,
                  pl.BlockSpec((nb, tc, n2), lambda i: (0, i, 0))]
        + [_layer_spec(a, layer) for a in consts],
        out_specs=pl.BlockSpec((nb, tc, d), lambda i: (0, i, 0)),
        out_shape=jax.ShapeDtypeStruct((nb, length, d), F32),
        compiler_params=pltpu.CompilerParams(dimension_semantics=("arbitrary",),
                                             vmem_limit_bytes=FFN_VMEM_LIMIT),
        name="ffn",
    )(x3, ysl3, yg3, *consts)


def _block_diag(blocks):
    dep, n, r, c = blocks.shape
    eye = jnp.eye(n, dtype=blocks.dtype)
    return (eye[None, :, None, :, None] * blocks[:, :, :, None, :]).reshape(dep, n * r, n * c)


def _params(norm_mix_pre, norm_mix_post, norm_ffn_pre, norm_ffn_post, w_in, s5_lambda_re,
            s5_lambda_im, s5_log_dt, s5_b_re, s5_b_im, s5_c_re, s5_c_im, s5_d, s5_w_glu, s5_b_glu,
            lru_conv_w, lru_conv_b, lru_w_a, lru_b_a, lru_w_x, lru_b_x, lru_lambda, gdn_conv_w,
            gdn_a_log, gdn_dt_bias, gdn_norm_w, w_out, ffn_w_gate_up, ffn_w_down):
    dep, g, pst, hgrp = s5_b_re.shape
    s5w = g * hgrp
    lw = lru_lambda.shape[1]
    nh = gdn_a_log.shape[1]
    row = lambda v: v[:, None, :]
    lanes = lambda v: jnp.pad(v, ((0, 0), (0, LANES - v.shape[1])))[:, None, :]
    return {
        "n_sl": s5w + 2 * lw,
        "n_mix_pre": row(norm_mix_pre), "n_mix_post": row(norm_mix_post),
        "n_ffn_pre": row(norm_ffn_pre), "n_ffn_post": row(norm_ffn_post),
        "w_in": w_in.astype(BF16),
        "s5_lre": s5_lambda_re.reshape(dep, 1, g * pst), "s5_lim": s5_lambda_im.reshape(dep, 1, g * pst),
        "s5_ldt": jnp.repeat(s5_log_dt, pst, axis=1)[:, None, :],
        "s5_bre": _block_diag(jnp.swapaxes(s5_b_re, 2, 3)),
        "s5_bim": _block_diag(jnp.swapaxes(s5_b_im, 2, 3)),
        "s5_cre": _block_diag(jnp.swapaxes(s5_c_re, 2, 3)),
        "s5_cim": _block_diag(jnp.swapaxes(s5_c_im, 2, 3)),
        "s5_d": row(s5_d), "s5_wglu": s5_w_glu.astype(BF16), "s5_bglu": row(s5_b_glu),
        "lru_cw": lru_conv_w, "lru_cb": row(lru_conv_b),
        "lru_wa": _block_diag(lru_w_a).astype(BF16), "lru_ba": row(lru_b_a),
        "lru_wx": _block_diag(lru_w_x).astype(BF16), "lru_bx": row(lru_b_x),
        "lru_lambda": row(lru_lambda),
        "gdn_cw": gdn_conv_w, "gdn_alog": lanes(gdn_a_log),
        "gdn_dtb": lanes(gdn_dt_bias), "gdn_nw": row(gdn_norm_w), "gdn_nh": nh,
        "w_out": w_out.astype(BF16), "w_gu": ffn_w_gate_up.astype(BF16),
        "w_down": ffn_w_down.astype(BF16),
    }


def _prompt_layer(x, p, layer, *, tc, rg):
    nb, length, _ = x.shape
    sl_tm, gd = _in_proj(x, p, layer, tc=min(IN_PROJ_TOKENS_PER_SEQ_STEP, length), to_time_major=True)
    lw = p["lru_lambda"].shape[-1]
    ysl, xr, xi, hl, cv = _scan(sl_tm.reshape(length * nb, -1), p, layer, None, nb=nb, tc=tc,
                                first_pos_is_zero=True)
    yg, s_new, gcv = _gdn_prompt(gd, p, layer, rg=rg, ns=GDN_SEQS_PER_STEP if nb % GDN_SEQS_PER_STEP == 0 else 1)
    x = _ffn(x, ysl.reshape(length, nb, -1), yg, p, layer, tc=min(FFN_TOKENS_PER_SEQ_STEP, length),
             from_time_major=True)
    lcv = jnp.swapaxes(cv.reshape(3, nb, lw), 0, 1)
    return x, (xr, xi, hl, lcv, s_new, gcv)


def _sample_layer(x_tm, st, gdn_s_all, gdn_s_acc, layer, p, *, nbk):
    s5_re, s5_im, lru_h, lru_conv, gdn_conv = st
    nb = s5_re.shape[0]
    rows = x_tm.shape[1]
    nt = rows // nb
    sl, gd = _in_proj(x_tm, p, layer, tc=rows, to_time_major=False)
    lw = p["lru_lambda"].shape[-1]
    cv0 = jnp.swapaxes(lru_conv, 0, 1).reshape(3 * nb, lw)
    states = [s5_re.reshape(nb, -1), s5_im.reshape(nb, -1), lru_h, cv0]
    ysl, xr, xi, hl, cv = _scan(sl.reshape(rows, -1), p, layer, states, nb=nb, tc=nt,
                                first_pos_is_zero=(PAST_LEN == 0))
    yg, s_acc, gcv = _gdn_sample(gd.reshape(nt, nb, -1), jnp.swapaxes(gdn_conv, 0, 1), gdn_s_all, gdn_s_acc,
                                 layer, p, nbk=nbk)
    x_tm = _ffn(x_tm, ysl.reshape(1, rows, -1), yg.reshape(1, rows, -1), p, layer, tc=rows,
                from_time_major=False)
    lcv = jnp.swapaxes(cv.reshape(3, nb, lw), 0, 1)
    return x_tm, (xr, xi, hl, lcv, jnp.swapaxes(gcv, 0, 1)), s_acc


def kernel(x_prompt, x_sample, state_s5_re, state_s5_im, state_lru_h, state_lru_conv, state_gdn_S, state_gdn_conv, norm_mix_pre, norm_mix_post, norm_ffn_pre, norm_ffn_post, w_in, s5_lambda_re, s5_lambda_im, s5_log_dt, s5_b_re, s5_b_im, s5_c_re, s5_c_im, s5_d, s5_w_glu, s5_b_glu, lru_conv_w, lru_conv_b, lru_w_a, lru_b_a, lru_w_x, lru_b_x, lru_lambda, gdn_conv_w, gdn_a_log, gdn_dt_bias, gdn_norm_w, w_out, ffn_w_gate_up, ffn_w_down):
    depth = w_in.shape[0]
    bp, lp, d = x_prompt.shape
    bs, ls, _ = x_sample.shape
    g, pst = s5_lambda_re.shape[1:]
    tc = min(TOKENS_PER_SEQ_STEP, lp)
    rg = min(GDN_ROWS_PER_STEP, lp)
    y_p = x_prompt
    y_s = jnp.swapaxes(x_sample, 0, 1).reshape(1, ls * bs, d)
    new_p = [[] for _ in range(6)]
    new_s = [[] for _ in range(5)]
    s_gdn_s = None
    p = _params(norm_mix_pre, norm_mix_post, norm_ffn_pre, norm_ffn_post, w_in, s5_lambda_re,
                s5_lambda_im, s5_log_dt, s5_b_re, s5_b_im, s5_c_re, s5_c_im, s5_d, s5_w_glu,
                s5_b_glu, lru_conv_w, lru_conv_b, lru_w_a, lru_b_a, lru_w_x, lru_b_x, lru_lambda,
                gdn_conv_w, gdn_a_log, gdn_dt_bias, gdn_norm_w, w_out, ffn_w_gate_up, ffn_w_down)
    for l in range(depth):
        y_p, sp = _prompt_layer(y_p, p, l, tc=tc, rg=rg)
        st_s = (state_s5_re[l], state_s5_im[l], state_lru_h[l], state_lru_conv[l], state_gdn_conv[l])
        y_s, ss, s_gdn_s = _sample_layer(y_s, st_s, state_gdn_S, s_gdn_s, l, p,
                                         nbk=min(DECODE_SEQS_PER_STEP, bs))
        for j in range(6):
            new_p[j].append(sp[j])
        for j in range(5):
            new_s[j].append(ss[j])
    y_s = jnp.swapaxes(y_s.reshape(ls, bs, d), 0, 1)
    outs_p = [jnp.stack(t, axis=0) for t in new_p]
    outs_s = [jnp.stack(t, axis=0) for t in new_s]
    outs_s.insert(4, s_gdn_s)
    outs_p[0] = outs_p[0].reshape(depth, bp, g, pst)
    outs_p[1] = outs_p[1].reshape(depth, bp, g, pst)
    outs_s[0] = outs_s[0].reshape(depth, bs, g, pst)
    outs_s[1] = outs_s[1].reshape(depth, bs, g, pst)
    return (y_p, y_s, *outs_p, *outs_s)
```

```python
import functools

import jax
import jax.numpy as jnp
from jax import lax
from jax.experimental import pallas as pl
from jax.experimental.pallas import tpu as pltpu

F32 = jnp.float32
BF16 = jnp.bfloat16
HIGHEST = lax.Precision.HIGHEST

NORM_EPS = 1e-6
LRU_C = 8.0
GDN_CHUNK = 64
PAST_LEN = 16384
LANES = 128
SUBLANES = 8
VMEM_LIMIT = 56 * 1024 * 1024
FFN_VMEM_LIMIT = 60 * 1024 * 1024

TOKENS_PER_SEQ_STEP = 64
IN_PROJ_TOKENS_PER_SEQ_STEP = 128
FFN_TOKENS_PER_SEQ_STEP = 128
FFN_GROUP_ROWS = 256
GDN_ROWS_PER_STEP = 4 * GDN_CHUNK
GDN_SEQS_PER_STEP = 4
DECODE_SEQS_PER_STEP = 16
DECODE_SEQ_UNROLL = 8
ROW_GROUPS = 2


def _rms(x, w):
    return x * lax.rsqrt(jnp.mean(x * x, axis=-1, keepdims=True) + NORM_EPS) * w


def _sigmoid(x):
    return jax.nn.sigmoid(x)


def _softplus(x):
    return jnp.maximum(x, 0.0) + jnp.log1p(jnp.exp(-jnp.abs(x)))


def _mm(a, b):
    return jnp.dot(a.astype(BF16), b.astype(BF16), preferred_element_type=F32)


def _mm_nt(a, b):
    return lax.dot_general(a.astype(BF16), b.astype(BF16), (((1,), (1,)), ((), ())),
                           preferred_element_type=F32)


def _mm_tn(a, b):
    return lax.dot_general(a.astype(BF16), b.astype(BF16), (((0,), (0,)), ((), ())),
                           preferred_element_type=F32)


def _const_spec(shape):
    n = len(shape)
    return pl.BlockSpec(shape, lambda *_: (0,) * n, pipeline_mode=pl.Buffered(1))


def _layer_spec(arr, layer):
    n = arr.ndim - 1
    return pl.BlockSpec((None,) + arr.shape[1:], lambda *_: (layer,) + (0,) * n,
                        pipeline_mode=pl.Buffered(1))


def _l2n_heads(x, nh, hd, mul):
    def one(h):
        xh = x[:, h * hd:(h + 1) * hd]
        return xh * (lax.rsqrt(jnp.sum(xh * xh, axis=-1, keepdims=True) + NORM_EPS) * mul)
    return jnp.concatenate([one(h) for h in range(nh)], axis=-1)


def _in_proj_kernel(x_ref, nw_ref, w_ref, sl_ref, gd_ref, *, n_sl, w3, n_ab, to_time_major):
    nb, tc, d = x_ref.shape
    rows = nb * tc
    x = x_ref[...].reshape(rows, d)
    groups = [slice(k * rows // ROW_GROUPS, (k + 1) * rows // ROW_GROUPS) for k in range(ROW_GROUPS)]
    p = jnp.concatenate([_mm(_rms(x[r], nw_ref[...]), w_ref[...]) for r in groups], axis=0)
    sl = p[:, :n_sl].reshape(nb, tc, n_sl)
    if to_time_major:
        sl = jnp.swapaxes(sl, 0, 1)
    sl_ref[...] = sl
    c_ab = n_sl + w3
    c_z = c_ab + n_ab
    wz = p.shape[1] - c_z
    gd_ref[:, :, 0:w3] = p[:, n_sl:c_ab].reshape(nb, tc, w3)
    gd_ref[:, :, w3:w3 + wz] = p[:, c_z:].reshape(nb, tc, wz)
    ab = jnp.concatenate([p[:, c_ab:c_z], jnp.zeros((rows, LANES - n_ab), F32)], axis=1)
    gd_ref[:, :, w3 + wz:] = ab.reshape(nb, tc, LANES)


def _in_proj(x3, p, layer, *, tc, to_time_major):
    nb, length, d = x3.shape
    nw, w, n_sl = p["n_mix_pre"], p["w_in"], p["n_sl"]
    n_tot = w.shape[-1]
    nh = p["gdn_nh"]
    w3 = 3 * nh * p["gdn_nw"].shape[-1]
    n_gd = n_tot - n_sl - 2 * nh + LANES
    if to_time_major:
        sl_shape, sl_spec = (length, nb, n_sl), pl.BlockSpec((tc, nb, n_sl), lambda i: (i, 0, 0))
    else:
        sl_shape, sl_spec = (nb, length, n_sl), pl.BlockSpec((nb, tc, n_sl), lambda i: (0, i, 0))
    return pl.pallas_call(
        functools.partial(_in_proj_kernel, n_sl=n_sl, w3=w3, n_ab=2 * nh, to_time_major=to_time_major),
        grid=(length // tc,),
        in_specs=[pl.BlockSpec((nb, tc, d), lambda i: (0, i, 0)),
                  _layer_spec(nw, layer), _layer_spec(w, layer)],
        out_specs=[sl_spec, pl.BlockSpec((nb, tc, n_gd), lambda i: (0, i, 0))],
        out_shape=[jax.ShapeDtypeStruct(sl_shape, F32), jax.ShapeDtypeStruct((nb, length, n_gd), F32)],
        compiler_params=pltpu.CompilerParams(dimension_semantics=("arbitrary",),
                                             vmem_limit_bytes=VMEM_LIMIT),
        name="in_proj",
    )(x3, nw, w)


def _scan_kernel(*refs, nb, tc, s5w, lw, first_pos_is_zero, zero_state):
    (sl_ref, lre_ref, lim_ref, ldt_ref, bre_ref, bim_ref, cre_ref, cim_ref, d_ref,
     wglu_ref, bglu_ref, cw_ref, cb_ref, wa_ref, ba_ref, wx_ref, bx_ref, lam_ref) = refs[:18]
    n_in = 18 if zero_state else 22
    (y_ref, xr_out, xi_out, h_out, cv_out,
     ar_s, ai_s, bbr_s, bbi_s, crb_s, cib_s, xr_c, xi_c, h_c, xr_s, xi_s, xpad_s, a_s, b_s) = refs[n_in:]
    i = pl.program_id(0)
    last = pl.num_programs(0) - 1
    rows = nb * tc

    @pl.when(i == 0)
    def _init():
        lr = jnp.minimum(lre_ref[...], -1e-4)
        li = lim_ref[...]
        dt = jnp.exp(ldt_ref[...])
        mag = jnp.exp(lr * dt)
        ar = mag * jnp.cos(li * dt)
        ai = mag * jnp.sin(li * dt)
        den = lr * lr + li * li
        fr = ((ar - 1.0) * lr + ai * li) / den
        fi = (ai * lr - (ar - 1.0) * li) / den
        ar_s[...] = ar
        ai_s[...] = ai
        bre = bre_ref[...]
        bim = bim_ref[...]
        bbr_s[...] = (fr * bre - fi * bim).astype(BF16)
        bbi_s[...] = (fr * bim + fi * bre).astype(BF16)
        crb_s[...] = cre_ref[...].astype(BF16)
        cib_s[...] = cim_ref[...].astype(BF16)
        if zero_state:
            xr_c[...] = jnp.zeros_like(xr_c)
            xi_c[...] = jnp.zeros_like(xi_c)
            h_c[...] = jnp.zeros_like(h_c)
            xpad_s[0:3 * nb, :] = jnp.zeros((3 * nb, lw), F32)
        else:
            xr0_ref, xi0_ref, h0_ref, cv0_ref = refs[18:22]
            xr_c[...] = xr0_ref[...]
            xi_c[...] = xi0_ref[...]
            h_c[...] = h0_ref[...]
            xpad_s[0:3 * nb, :] = cv0_ref[...]

    u = sl_ref[:, 0:s5w]
    xb = sl_ref[:, s5w:s5w + lw]
    gate = sl_ref[:, s5w + lw:s5w + 2 * lw]

    ub = u.astype(BF16)
    xr_s[...] = jnp.dot(ub, bbr_s[...], preferred_element_type=F32)
    xi_s[...] = jnp.dot(ub, bbi_s[...], preferred_element_type=F32)

    xpad_s[3 * nb:3 * nb + rows, :] = xb
    cw = cw_ref[...]
    xc = (xpad_s[0:rows, :] * cw[0:1] + xpad_s[nb:nb + rows, :] * cw[1:2]
          + xpad_s[2 * nb:2 * nb + rows, :] * cw[2:3] + xb * cw[3:4]) + cb_ref[...]
    new_cv = xpad_s[rows:rows + 3 * nb, :]
    xpad_s[0:3 * nb, :] = new_cv
    r = _sigmoid(_mm(xc, wa_ref[...]) + ba_ref[...])
    gi = _sigmoid(_mm(xc, wx_ref[...]) + bx_ref[...])
    log_a = -LRU_C * r * _softplus(-lam_ref[...])
    a = jnp.exp(log_a)
    m2 = -jnp.tanh(log_a) * (a * a + 1.0)
    mult = jnp.where(m2 > 0.0, m2 * lax.rsqrt(m2), 0.0)
    if first_pos_is_zero:
        rid = lax.broadcasted_iota(jnp.int32, (rows, 1), 0)
        mult = jnp.where(jnp.logical_and(rid < nb, i == 0), 1.0, mult)
    a_s[...] = a
    b_s[...] = mult * gi * xc

    ar = jnp.broadcast_to(ar_s[...], (nb, ar_s.shape[1]))
    ai = jnp.broadcast_to(ai_s[...], (nb, ai_s.shape[1]))

    def step(t, carry):
        xr, xi, h = carry
        sl_t = pl.ds(pl.multiple_of(t * nb, nb), nb)
        nxr = ar * xr - ai * xi + xr_s[sl_t, :]
        nxi = ar * xi + ai * xr + xi_s[sl_t, :]
        nh = a_s[sl_t, :] * h + b_s[sl_t, :]
        xr_s[sl_t, :] = nxr
        xi_s[sl_t, :] = nxi
        b_s[sl_t, :] = nh
        return nxr, nxi, nh

    xr, xi, h = lax.fori_loop(0, tc, step, (xr_c[...], xi_c[...], h_c[...]),
                              unroll=(True if tc <= 8 else 8))
    xr_c[...] = xr
    xi_c[...] = xi
    h_c[...] = h

    y = (jnp.dot(xr_s[...].astype(BF16), crb_s[...], preferred_element_type=F32)
         - jnp.dot(xi_s[...].astype(BF16), cib_s[...], preferred_element_type=F32))
    y = jax.nn.gelu(y + d_ref[...] * u)
    y = y * _sigmoid(_mm(y, wglu_ref[...]) + bglu_ref[...])
    y_ref[:, 0:s5w] = y
    y_ref[:, s5w:s5w + lw] = b_s[...] * jax.nn.gelu(gate)

    @pl.when(i == last)
    def _fin():
        xr_out[...] = xr
        xi_out[...] = xi
        h_out[...] = h
        cv_out[...] = new_cv


def _scan(sl2, p, layer, states, *, nb, tc, first_pos_is_zero):
    rows_total, n_sl = sl2.shape
    s5w = p["s5_d"].shape[-1]
    lw = p["lru_lambda"].shape[-1]
    n_state = p["s5_lre"].shape[-1]
    rows = nb * tc
    params = [p["s5_lre"], p["s5_lim"], p["s5_ldt"], p["s5_bre"], p["s5_bim"], p["s5_cre"], p["s5_cim"],
              p["s5_d"], p["s5_wglu"], p["s5_bglu"], p["lru_cw"], p["lru_cb"], p["lru_wa"], p["lru_ba"],
              p["lru_wx"], p["lru_bx"], p["lru_lambda"]]
    zero_state = states is None
    states = [] if zero_state else list(states)
    return pl.pallas_call(
        functools.partial(_scan_kernel, nb=nb, tc=tc, s5w=s5w, lw=lw, first_pos_is_zero=first_pos_is_zero,
                          zero_state=zero_state),
        grid=(rows_total // rows,),
        in_specs=[pl.BlockSpec((rows, n_sl), lambda i: (i, 0))]
        + [_layer_spec(a, layer) for a in params] + [_const_spec(a.shape) for a in states],
        out_specs=[pl.BlockSpec((rows, s5w + lw), lambda i: (i, 0)),
                   pl.BlockSpec((nb, n_state), lambda i: (0, 0)),
                   pl.BlockSpec((nb, n_state), lambda i: (0, 0)),
                   pl.BlockSpec((nb, lw), lambda i: (0, 0)),
                   pl.BlockSpec((3 * nb, lw), lambda i: (0, 0))],
        out_shape=[jax.ShapeDtypeStruct((rows_total, s5w + lw), F32),
                   jax.ShapeDtypeStruct((nb, n_state), F32),
                   jax.ShapeDtypeStruct((nb, n_state), F32),
                   jax.ShapeDtypeStruct((nb, lw), F32),
                   jax.ShapeDtypeStruct((3 * nb, lw), F32)],
        scratch_shapes=[pltpu.VMEM((1, n_state), F32), pltpu.VMEM((1, n_state), F32),
                        pltpu.VMEM((s5w, n_state), BF16), pltpu.VMEM((s5w, n_state), BF16),
                        pltpu.VMEM((n_state, s5w), BF16), pltpu.VMEM((n_state, s5w), BF16),
                        pltpu.VMEM((nb, n_state), F32), pltpu.VMEM((nb, n_state), F32),
                        pltpu.VMEM((nb, lw), F32),
                        pltpu.VMEM((rows, n_state), F32), pltpu.VMEM((rows, n_state), F32),
                        pltpu.VMEM((rows + 3 * nb, lw), F32),
                        pltpu.VMEM((rows, lw), F32), pltpu.VMEM((rows, lw), F32)],
        compiler_params=pltpu.CompilerParams(dimension_semantics=("arbitrary",),
                                             vmem_limit_bytes=VMEM_LIMIT),
        name="scan",
    )(sl2, *params, *states)


def _gdn_prompt_kernel(gd_ref, cw_ref, alog_ref, dtb_ref, nw_ref, y_ref, s_out, cv_out,
                       s_s, xp_s, *, ns, rg, nh, hd):
    i = pl.program_id(1)
    last = pl.num_programs(1) - 1
    w3 = 3 * nh * hd
    wz = nh * hd
    c = GDN_CHUNK
    nc = rg // c
    wc = nh * c
    rt = ns * rg

    @pl.when(i == 0)
    def _init():
        s_s[...] = jnp.zeros_like(s_s)
        xp_s[...] = jnp.zeros_like(xp_s)

    cw = cw_ref[...]
    xcs = []
    for s in range(ns):
        qkv = gd_ref[s, :, 0:w3]
        xe = jnp.concatenate([xp_s[s], qkv], axis=0)
        acc = qkv * cw[3:4]
        for j in range(1, 4):
            acc = acc + pltpu.roll(xe, j, 0)[SUBLANES:, :] * cw[3 - j:4 - j]
        xp_s[s] = qkv[rg - SUBLANES:, :]
        xcs.append(acc)
    xc = jnp.concatenate(xcs, axis=0)
    xc = xc * _sigmoid(xc)

    ab = jnp.concatenate([gd_ref[s, :, w3 + wz:w3 + wz + LANES] for s in range(ns)], axis=0)
    g_all = -jnp.exp(alog_ref[...]) * _softplus(ab + dtb_ref[...])
    beta_all = _sigmoid(ab)

    def per_head(fn):
        return jnp.concatenate([fn(h) for h in range(nh)], axis=-1)

    q_all = _l2n_heads(xc[:, 0:wz], nh, hd, hd ** -0.5)
    k_all = _l2n_heads(xc[:, wz:2 * wz], nh, hd, 1.0)
    v_all = xc[:, 2 * wz:3 * wz]
    beta_w = per_head(lambda h: jnp.broadcast_to(beta_all[:, nh + h:nh + h + 1], (rt, hd)))
    kb_all = k_all * beta_w
    vb_all = v_all * beta_w

    ri = lax.broadcasted_iota(jnp.int32, (c, wc), 0)
    cj = lax.broadcasted_iota(jnp.int32, (c, wc), 1) % c
    incl = ri >= cj
    strict = ri > cj
    eye = (ri == cj).astype(F32)
    r1 = lax.broadcasted_iota(jnp.int32, (c, c), 0)
    c1 = lax.broadcasted_iota(jnp.int32, (c, c), 1)
    ltri = (r1 >= c1).astype(F32)
    bd_cc = (lax.broadcasted_iota(jnp.int32, (wc, wc), 0) // c
             == lax.broadcasted_iota(jnp.int32, (wc, wc), 1) // c)
    bd_cd = (lax.broadcasted_iota(jnp.int32, (wc, wz), 0) // c
             == lax.broadcasted_iota(jnp.int32, (wc, wz), 1) // hd)

    def bdiag(x, mask):
        return jnp.where(mask, jnp.concatenate([x] * nh, axis=0), 0.0).astype(BF16)

    nc_all = ns * nc
    rows = [slice(cc * c, (cc + 1) * c) for cc in range(nc_all)]
    gcum = [jnp.dot(ltri, g_all[r], precision=HIGHEST, preferred_element_type=F32) for r in rows]
    cgc = [per_head(lambda h, x=x: jnp.broadcast_to(x[:, h:h + 1], (c, hd))) for x in gcum]
    gcol = [per_head(lambda h, x=x: jnp.broadcast_to(x[:, h:h + 1], (c, c))) for x in gcum]
    grow = [jnp.concatenate([xt[h:h + 1, :] for h in range(nh)], axis=1)
            for xt in [x.T for x in gcum]]
    every = range(nc_all)
    decay = [jnp.exp(jnp.where(incl, gcol[n] - grow[n], -jnp.inf)) for n in every]
    eg = [jnp.exp(x) for x in cgc]
    g_last = [x[c - 1:c, :] for x in cgc]
    kk = [_mm_nt(jnp.concatenate([kb_all[r], q_all[r]], axis=0), bdiag(k_all[r], bd_cd)) for r in rows]
    attn = [kk[n][c:2 * c] * decay[n] for n in every]
    rj = [-jnp.where(strict, kk[n][0:c] * decay[n], 0.0) for n in every]
    sj = [eye + rj[n] for n in every]
    rj = [_mm(rj[n], bdiag(rj[n], bd_cc)) for n in every]
    for _ in range(4):
        rs = [_mm(jnp.concatenate([rj[n], sj[n]], axis=0), bdiag(rj[n], bd_cc)) for n in every]
        rj = [x[0:c] for x in rs]
        sj = [sj[n] + rs[n][c:2 * c] for n in every]
    sj = [sj[n] + _mm(sj[n], bdiag(rj[n], bd_cc)) for n in every]
    w = [_mm(sj[n], jnp.concatenate([bdiag(vb_all[rows[n]], bd_cd),
                                     bdiag(kb_all[rows[n]] * eg[n], bd_cd)], axis=1)) for n in every]
    q_dec = [q_all[rows[n]] * eg[n] for n in every]
    k_dec = [k_all[rows[n]] * jnp.exp(g_last[n] - cgc[n]) for n in every]
    s_dec = [jnp.exp(x) for x in g_last]

    s_cur = [[s_s[s, h] for h in range(nh)] for s in range(ns)]
    hs = [slice(h * hd, (h + 1) * hd) for h in range(nh)]
    for m in range(nc):
        ws = [[_mm(jnp.concatenate([w[s * nc + m][:, wz + h * hd:wz + (h + 1) * hd],
                                    q_dec[s * nc + m][:, hs[h]]], axis=0), s_cur[s][h]) for h in range(nh)]
              for s in range(ns)]
        v_new = [jnp.concatenate([w[s * nc + m][:, hs[h]] - ws[s][h][0:c] for h in range(nh)], axis=-1)
                 for s in range(ns)]
        o = [jnp.concatenate([x[c:2 * c] for x in ws[s]], axis=-1) + _mm(attn[s * nc + m], bdiag(v_new[s], bd_cd))
             for s in range(ns)]
        s_cur = [[s_cur[s][h] * s_dec[s * nc + m][:, hs[h]]
                  + _mm_tn(k_dec[s * nc + m][:, hs[h]], v_new[s][:, hs[h]]) for h in range(nh)]
                 for s in range(ns)]
        for s in range(ns):
            for h in range(nh):
                z = gd_ref[s, m * c:(m + 1) * c, w3 + h * hd:w3 + (h + 1) * hd]
                y_ref[s, m * c:(m + 1) * c, hs[h]] = _rms(o[s][:, hs[h]], nw_ref[...]) * (z * _sigmoid(z))
    for s in range(ns):
        for h in range(nh):
            s_s[s, h] = s_cur[s][h]

    @pl.when(i == last)
    def _fin():
        for s in range(ns):
            for h in range(nh):
                s_out[s, h] = s_cur[s][h]
            cv_out[s] = xp_s[s, SUBLANES - 3:SUBLANES, :]


def _gdn_prompt(gd3, p, layer, *, rg, ns):
    nb, length, n_gd = gd3.shape
    nh = p["gdn_nh"]
    hd = p["gdn_nw"].shape[-1]
    w3 = 3 * nh * hd
    consts = [p["gdn_cw"], p["gdn_alog"], p["gdn_dtb"], p["gdn_nw"]]
    return pl.pallas_call(
        functools.partial(_gdn_prompt_kernel, ns=ns, rg=rg, nh=nh, hd=hd),
        grid=(nb // ns, length // rg),
        in_specs=[pl.BlockSpec((ns, rg, n_gd), lambda b, i: (b, i, 0))]
        + [_layer_spec(a, layer) for a in consts],
        out_specs=[pl.BlockSpec((ns, rg, nh * hd), lambda b, i: (b, i, 0)),
                   pl.BlockSpec((ns, nh, hd, hd), lambda b, i: (b, 0, 0, 0)),
                   pl.BlockSpec((ns, 3, w3), lambda b, i: (b, 0, 0))],
        out_shape=[jax.ShapeDtypeStruct((nb, length, nh * hd), F32),
                   jax.ShapeDtypeStruct((nb, nh, hd, hd), F32),
                   jax.ShapeDtypeStruct((nb, 3, w3), F32)],
        scratch_shapes=[pltpu.VMEM((ns, nh, hd, hd), F32), pltpu.VMEM((ns, SUBLANES, w3), F32)],
        compiler_params=pltpu.CompilerParams(dimension_semantics=("arbitrary", "arbitrary"),
                                             vmem_limit_bytes=VMEM_LIMIT),
        name="gdn_prompt",
    )(gd3, *consts)


def _gdn_sample_kernel(*refs, nt, nbk, nh, hd, layer, has_acc):
    gd_ref, cv0_ref, s0_ref, cw_ref, alog_ref, dtb_ref, nw_ref = refs[:7]
    y_ref, s_out, cv_out, wk_s, kd_s, wv_s, ws_s = refs[7 + int(has_acc):]
    if not has_acc:
        for other in range(s_out.shape[0]):
            if other != layer:
                s_out[other] = jnp.zeros(s_out.shape[1:], F32)
        s_out = s_out.at[layer]
    w3 = 3 * nh * hd
    wz = nh * hd
    cw = cw_ref[...]
    scale = hd ** -0.5
    xs = [cv0_ref[j] for j in range(3)] + [gd_ref[t, :, 0:w3] for t in range(nt)]
    for j in range(3):
        cv_out[j] = xs[nt + j]
    q, k, v, gc, beta = [], [], [], [], []
    for t in range(nt):
        xc = xs[t] * cw[0:1] + xs[t + 1] * cw[1:2] + xs[t + 2] * cw[2:3] + xs[t + 3] * cw[3:4]
        xc = xc * _sigmoid(xc)
        ab = gd_ref[t, :, w3 + wz:w3 + wz + LANES]
        g_t = -jnp.exp(alog_ref[...]) * _softplus(ab + dtb_ref[...])
        gc.append(g_t if t == 0 else gc[-1] + g_t)
        beta.append(_sigmoid(ab))
        qn = _l2n_heads(xc[:, 0:wz], nh, hd, scale)
        kn = _l2n_heads(xc[:, wz:2 * wz], nh, hd, 1.0)
        q.append([qn[:, h * hd:(h + 1) * hd] for h in range(nh)])
        k.append([kn[:, h * hd:(h + 1) * hd] for h in range(nh)])
        v.append([xc[:, 2 * wz + h * hd:2 * wz + (h + 1) * hd] for h in range(nh)])

    def dot(a, b):
        return jnp.sum(a * b, axis=-1, keepdims=True)

    wk_rows = [[None] * nh for _ in range(nt)]
    qd_rows = [[None] * nh for _ in range(nt)]
    kd_rows = [[None] * nh for _ in range(nt)]
    wv_rows = [[None] * nh for _ in range(nt)]
    sd_rows = [None] * nh
    attn = [None] * nh
    for h in range(nh):
        gch = [gc[t][:, h:h + 1] for t in range(nt)]
        bh = [beta[t][:, nh + h:nh + h + 1] for t in range(nt)]
        e = [jnp.exp(x) for x in gch]
        kb = [k[t][h] * bh[t] for t in range(nt)]
        vb = [v[t][h] * bh[t] for t in range(nt)]
        kbe = [kb[t] * e[t] for t in range(nt)]
        dec = [[jnp.exp(gch[i] - gch[j]) for j in range(i)] for i in range(nt)]
        m = [[dot(kb[i], k[j][h]) * dec[i][j] for j in range(i)] for i in range(nt)]
        attn[h] = [[dot(q[i][h], k[j][h]) * (dec[i][j] if j < i else 1.0) for j in range(i + 1)]
                   for i in range(nt)]
        tinv = [[None] * nt for _ in range(nt)]
        for j in range(nt):
            for i in range(j + 1, nt):
                acc = m[i][j]
                for l in range(j + 1, i):
                    acc = acc + m[i][l] * tinv[l][j]
                tinv[i][j] = -acc
        for i in range(nt):
            wv_i, wk_i = vb[i], kbe[i]
            for j in range(i):
                wv_i = wv_i + tinv[i][j] * vb[j]
                wk_i = wk_i + tinv[i][j] * kbe[j]
            wv_rows[i][h] = wv_i
            wk_rows[i][h] = wk_i
            qd_rows[i][h] = q[i][h] * e[i]
            kd_rows[i][h] = k[i][h] * jnp.exp(gch[nt - 1] - gch[i])
        sd_rows[h] = jnp.broadcast_to(e[nt - 1], (nbk, hd))

    cat = lambda parts: jnp.concatenate(parts, axis=-1)
    to_seq = lambda slabs: jnp.swapaxes(jnp.stack(slabs, axis=0), 0, 1)
    zero = jnp.zeros((nbk, wz), F32)
    wk_s[...] = to_seq([cat(wk_rows[i]) for i in range(nt)] + [cat(qd_rows[i]) for i in range(nt)])
    kd_s[...] = to_seq([cat(kd_rows[i]) for i in range(nt)] + [zero] * nt)
    wv_s[...] = to_seq([cat(wv_rows[i]) for i in range(nt)] + [cat(sd_rows)] * nt)

    group = min(DECODE_SEQ_UNROLL, nbk)

    def per_group(gi, _):
        pairs = [(gi * group + u, h) for u in range(group) for h in range(nh)]
        tiles = {}
        for u in range(group):
            b = gi * group + u
            tiles[u] = (wk_s[b], kd_s[b], wv_s[b])
        cols = [slice(h * hd, (h + 1) * hd) for h in range(nh)]
        s_old = [s0_ref[b, h] for b, h in pairs]
        r = [_mm(tiles[n // nh][0][:, cols[h]], s_old[n]) for n, (b, h) in enumerate(pairs)]
        for n, (b, h) in enumerate(pairs):
            tkd, twv = tiles[n // nh][1][:, cols[h]], tiles[n // nh][2][:, cols[h]]
            s_out[b, h] = s_old[n] * twv[nt:nt + 1, :] + _mm_tn(tkd, twv - r[n])
            ws_s[b, :, cols[h]] = r[n]
        return 0

    lax.fori_loop(0, nbk // group, per_group, 0)

    ws_tm = jnp.swapaxes(ws_s[...], 0, 1)
    for h in range(nh):
        cols = slice(h * hd, (h + 1) * hd)
        v_new = [wv_rows[j][h] - ws_tm[j][:, cols] for j in range(nt)]
        for i in range(nt):
            o = ws_tm[nt + i][:, cols]
            for j in range(i + 1):
                o = o + attn[h][i][j] * v_new[j]
            z = gd_ref[i, :, w3 + h * hd:w3 + (h + 1) * hd]
            y_ref[i, :, cols] = _rms(o, nw_ref[...]) * (z * _sigmoid(z))


def _gdn_sample(gd3, cv0, s_all, s_acc, layer, p, *, nbk):
    nt, nb, n_gd = gd3.shape
    nh = p["gdn_nh"]
    hd = p["gdn_nw"].shape[-1]
    w3 = 3 * nh * hd
    wz = nh * hd
    s_spec = pl.BlockSpec((None, nbk, nh, hd, hd), lambda i: (layer, i, 0, 0, 0))
    consts = [p["gdn_cw"], p["gdn_alog"], p["gdn_dtb"], p["gdn_nw"]]
    args = [gd3, cv0, s_all] + consts
    in_specs = [pl.BlockSpec((nt, nbk, n_gd), lambda i: (0, i, 0)),
                pl.BlockSpec((3, nbk, w3), lambda i: (0, i, 0)),
                s_spec] + [_layer_spec(a, layer) for a in consts]
    aliases = {}
    s_out_spec = s_spec
    if s_acc is not None:
        aliases = {len(args): 1}
        args.append(s_acc)
        in_specs.append(pl.BlockSpec(memory_space=pl.ANY))
    else:
        s_out_spec = pl.BlockSpec((s_all.shape[0], nbk, nh, hd, hd), lambda i: (0, i, 0, 0, 0))
    return pl.pallas_call(
        functools.partial(_gdn_sample_kernel, nt=nt, nbk=nbk, nh=nh, hd=hd, layer=layer,
                          has_acc=s_acc is not None),
        grid=(nb // nbk,),
        in_specs=in_specs,
        out_specs=[pl.BlockSpec((nt, nbk, wz), lambda i: (0, i, 0)),
                   s_out_spec,
                   pl.BlockSpec((3, nbk, w3), lambda i: (0, i, 0))],
        out_shape=[jax.ShapeDtypeStruct((nt, nb, wz), F32),
                   jax.ShapeDtypeStruct(s_all.shape, F32),
                   jax.ShapeDtypeStruct((3, nb, w3), F32)],
        scratch_shapes=[pltpu.VMEM((nbk, 2 * nt, wz), F32) for _ in range(4)],
        input_output_aliases=aliases,
        compiler_params=pltpu.CompilerParams(dimension_semantics=("arbitrary",),
                                             vmem_limit_bytes=VMEM_LIMIT),
        name="gdn_sample",
    )(*args)


def _ffn_kernel(x_ref, ysl_ref, yg_ref, wo_ref, nmp_ref, nfp_ref, nfq_ref, wgu_ref, wd_ref, o_ref,
                *, from_time_major):
    nb, tc, d = x_ref.shape
    rows = nb * tc
    x = x_ref[...].reshape(rows, d)
    ysl = ysl_ref[...]
    if from_time_major:
        ysl = jnp.swapaxes(ysl, 0, 1)
    ysl = ysl.reshape(rows, ysl.shape[-1])
    yg = yg_ref[...].reshape(rows, yg_ref.shape[-1])
    mix = jnp.concatenate([ysl, yg], axis=-1)
    n_groups = max(1, rows // FFN_GROUP_ROWS)
    groups = [slice(k * rows // n_groups, (k + 1) * rows // n_groups) for k in range(n_groups)]
    hid = wd_ref.shape[0]
    o1 = [_mm(mix[r], wo_ref[...]) for r in groups]
    x1 = [x[r] + _rms(o1[k], nmp_ref[...]) for k, r in enumerate(groups)]
    hh = hid // 2
    hb = [_rms(x1[k], nfp_ref[...]).astype(BF16) for k in range(n_groups)]
    f = [None] * n_groups
    for j in range(2):
        gt = [jnp.dot(hb[k], wgu_ref[:, j * hh:(j + 1) * hh], preferred_element_type=F32)
              for k in range(n_groups)]
        up = [jnp.dot(hb[k], wgu_ref[:, hid + j * hh:hid + (j + 1) * hh], preferred_element_type=F32)
              for k in range(n_groups)]
        part = [_mm(gt[k] * _sigmoid(gt[k]) * up[k], wd_ref[j * hh:(j + 1) * hh, :]) for k in range(n_groups)]
        f = part if j == 0 else [f[k] + part[k] for k in range(n_groups)]
    out = jnp.concatenate([x1[k] + _rms(f[k], nfq_ref[...]) for k in range(n_groups)], axis=0)
    o_ref[...] = out.reshape(nb, tc, d)


def _ffn(x3, ysl3, yg3, p, layer, *, tc, from_time_major):
    nb, length, d = x3.shape
    n1 = ysl3.shape[-1]
    n2 = yg3.shape[-1]
    consts = [p["w_out"], p["n_mix_post"], p["n_ffn_pre"], p["n_ffn_post"], p["w_gu"], p["w_down"]]
    if from_time_major:
        ysl_spec = pl.BlockSpec((tc, nb, n1), lambda i: (i, 0, 0))
    else:
        ysl_spec = pl.BlockSpec((nb, tc, n1), lambda i: (0, i, 0))
    return pl.pallas_call(
        functools.partial(_ffn_kernel, from_time_major=from_time_major),
        grid=(length // tc,),
        in_specs=[pl.BlockSpec((nb, tc, d), lambda i: (0, i, 0)), ysl_spec,
                  pl.BlockSpec((nb, tc, n2), lambda i: (0, i, 0))]
        + [_layer_spec(a, layer) for a in consts],
        out_specs=pl.BlockSpec((nb, tc, d), lambda i: (0, i, 0)),
        out_shape=jax.ShapeDtypeStruct((nb, length, d), F32),
        compiler_params=pltpu.CompilerParams(dimension_semantics=("arbitrary",),
                                             vmem_limit_bytes=FFN_VMEM_LIMIT),
        name="ffn",
    )(x3, ysl3, yg3, *consts)


def _block_diag(blocks):
    dep, n, r, c = blocks.shape
    eye = jnp.eye(n, dtype=blocks.dtype)
    return (eye[None, :, None, :, None] * blocks[:, :, :, None, :]).reshape(dep, n * r, n * c)


def _params(norm_mix_pre, norm_mix_post, norm_ffn_pre, norm_ffn_post, w_in, s5_lambda_re,
            s5_lambda_im, s5_log_dt, s5_b_re, s5_b_im, s5_c_re, s5_c_im, s5_d, s5_w_glu, s5_b_glu,
            lru_conv_w, lru_conv_b, lru_w_a, lru_b_a, lru_w_x, lru_b_x, lru_lambda, gdn_conv_w,
            gdn_a_log, gdn_dt_bias, gdn_norm_w, w_out, ffn_w_gate_up, ffn_w_down):
    dep, g, pst, hgrp = s5_b_re.shape
    s5w = g * hgrp
    lw = lru_lambda.shape[1]
    nh = gdn_a_log.shape[1]
    row = lambda v: v[:, None, :]
    lanes = lambda v: jnp.pad(v, ((0, 0), (0, LANES - v.shape[1])))[:, None, :]
    return {
        "n_sl": s5w + 2 * lw,
        "n_mix_pre": row(norm_mix_pre), "n_mix_post": row(norm_mix_post),
        "n_ffn_pre": row(norm_ffn_pre), "n_ffn_post": row(norm_ffn_post),
        "w_in": w_in.astype(BF16),
        "s5_lre": s5_lambda_re.reshape(dep, 1, g * pst), "s5_lim": s5_lambda_im.reshape(dep, 1, g * pst),
        "s5_ldt": jnp.repeat(s5_log_dt, pst, axis=1)[:, None, :],
        "s5_bre": _block_diag(jnp.swapaxes(s5_b_re, 2, 3)),
        "s5_bim": _block_diag(jnp.swapaxes(s5_b_im, 2, 3)),
        "s5_cre": _block_diag(jnp.swapaxes(s5_c_re, 2, 3)),
        "s5_cim": _block_diag(jnp.swapaxes(s5_c_im, 2, 3)),
        "s5_d": row(s5_d), "s5_wglu": s5_w_glu.astype(BF16), "s5_bglu": row(s5_b_glu),
        "lru_cw": lru_conv_w, "lru_cb": row(lru_conv_b),
        "lru_wa": _block_diag(lru_w_a).astype(BF16), "lru_ba": row(lru_b_a),
        "lru_wx": _block_diag(lru_w_x).astype(BF16), "lru_bx": row(lru_b_x),
        "lru_lambda": row(lru_lambda),
        "gdn_cw": gdn_conv_w, "gdn_alog": lanes(gdn_a_log),
        "gdn_dtb": lanes(gdn_dt_bias), "gdn_nw": row(gdn_norm_w), "gdn_nh": nh,
        "w_out": w_out.astype(BF16), "w_gu": ffn_w_gate_up.astype(BF16),
        "w_down": ffn_w_down.astype(BF16),
    }


def _prompt_layer(x, p, layer, *, tc, rg):
    nb, length, _ = x.shape
    sl_tm, gd = _in_proj(x, p, layer, tc=min(IN_PROJ_TOKENS_PER_SEQ_STEP, length), to_time_major=True)
    lw = p["lru_lambda"].shape[-1]
    ysl, xr, xi, hl, cv = _scan(sl_tm.reshape(length * nb, -1), p, layer, None, nb=nb, tc=tc,
                                first_pos_is_zero=True)
    yg, s_new, gcv = _gdn_prompt(gd, p, layer, rg=rg, ns=GDN_SEQS_PER_STEP if nb % GDN_SEQS_PER_STEP == 0 else 1)
    x = _ffn(x, ysl.reshape(length, nb, -1), yg, p, layer, tc=min(FFN_TOKENS_PER_SEQ_STEP, length),
             from_time_major=True)
    lcv = jnp.swapaxes(cv.reshape(3, nb, lw), 0, 1)
    return x, (xr, xi, hl, lcv, s_new, gcv)


def _sample_layer(x_tm, st, gdn_s_all, gdn_s_acc, layer, p, *, nbk):
    s5_re, s5_im, lru_h, lru_conv, gdn_conv = st
    nb = s5_re.shape[0]
    rows = x_tm.shape[1]
    nt = rows // nb
    sl, gd = _in_proj(x_tm, p, layer, tc=rows, to_time_major=False)
    lw = p["lru_lambda"].shape[-1]
    cv0 = jnp.swapaxes(lru_conv, 0, 1).reshape(3 * nb, lw)
    states = [s5_re.reshape(nb, -1), s5_im.reshape(nb, -1), lru_h, cv0]
    ysl, xr, xi, hl, cv = _scan(sl.reshape(rows, -1), p, layer, states, nb=nb, tc=nt,
                                first_pos_is_zero=(PAST_LEN == 0))
    yg, s_acc, gcv = _gdn_sample(gd.reshape(nt, nb, -1), jnp.swapaxes(gdn_conv, 0, 1), gdn_s_all, gdn_s_acc,
                                 layer, p, nbk=nbk)
    x_tm = _ffn(x_tm, ysl.reshape(1, rows, -1), yg.reshape(1, rows, -1), p, layer, tc=rows,
                from_time_major=False)
    lcv = jnp.swapaxes(cv.reshape(3, nb, lw), 0, 1)
    return x_tm, (xr, xi, hl, lcv, jnp.swapaxes(gcv, 0, 1)), s_acc


def kernel(x_prompt, x_sample, state_s5_re, state_s5_im, state_lru_h, state_lru_conv, state_gdn_S, state_gdn_conv, norm_mix_pre, norm_mix_post, norm_ffn_pre, norm_ffn_post, w_in, s5_lambda_re, s5_lambda_im, s5_log_dt, s5_b_re, s5_b_im, s5_c_re, s5_c_im, s5_d, s5_w_glu, s5_b_glu, lru_conv_w, lru_conv_b, lru_w_a, lru_b_a, lru_w_x, lru_b_x, lru_lambda, gdn_conv_w, gdn_a_log, gdn_dt_bias, gdn_norm_w, w_out, ffn_w_gate_up, ffn_w_down):
    depth = w_in.shape[0]
    bp, lp, d = x_prompt.shape
    bs, ls, _ = x_sample.shape
    g, pst = s5_lambda_re.shape[1:]
    tc = min(TOKENS_PER_SEQ_STEP, lp)
    rg = min(GDN_ROWS_PER_STEP, lp)
    y_p = x_prompt
    y_s = jnp.swapaxes(x_sample, 0, 1).reshape(1, ls * bs, d)
    new_p = [[] for _ in range(6)]
    new_s = [[] for _ in range(5)]
    s_gdn_s = None
    p = _params(norm_mix_pre, norm_mix_post, norm_ffn_pre, norm_ffn_post, w_in, s5_lambda_re,
                s5_lambda_im, s5_log_dt, s5_b_re, s5_b_im, s5_c_re, s5_c_im, s5_d, s5_w_glu,
                s5_b_glu, lru_conv_w, lru_conv_b, lru_w_a, lru_b_a, lru_w_x, lru_b_x, lru_lambda,
                gdn_conv_w, gdn_a_log, gdn_dt_bias, gdn_norm_w, w_out, ffn_w_gate_up, ffn_w_down)
    for l in range(depth):
        y_p, sp = _prompt_layer(y_p, p, l, tc=tc, rg=rg)
        st_s = (state_s5_re[l], state_s5_im[l], state_lru_h[l], state_lru_conv[l], state_gdn_conv[l])
        y_s, ss, s_gdn_s = _sample_layer(y_s, st_s, state_gdn_S, s_gdn_s, l, p,
                                         nbk=min(DECODE_SEQS_PER_STEP, bs))
        for j in range(6):
            new_p[j].append(sp[j])
        for j in range(5):
            new_s[j].append(ss[j])
    y_s = jnp.swapaxes(y_s.reshape(ls, bs, d), 0, 1)
    outs_p = [jnp.stack(t, axis=0) for t in new_p]
    outs_s = [jnp.stack(t, axis=0) for t in new_s]
    outs_s.insert(4, s_gdn_s)
    outs_p[0] = outs_p[0].reshape(depth, bp, g, pst)
    outs_p[1] = outs_p[1].reshape(depth, bp, g, pst)
    outs_s[0] = outs_s[0].reshape(depth, bs, g, pst)
    outs_s[1] = outs_s[1].reshape(depth, bs, g, pst)
    return (y_p, y_s, *outs_p, *outs_s)
```
